```python
import math, functools
import jax, jax.numpy as jnp
from jax import lax
import numpy as np

D_MODEL = 4096
BATCH = 4
SEQ = 2048
DEPTH = 2
DEC_BATCH = 8
DEC_SEQ = 8
PAST_LEN = 16384
PAGE_SIZE = 128

D_FF = 11008
C_A = 2048
N_A = 64
H_A = C_A // N_A
D_DECAY = 96
D_AAA = 96
D_GATE = 256
A_COLS = 3 * C_A + D_DECAY + D_AAA + D_GATE
GN_EPS = 64e-5
POOL_WINDOWS = (2, 4, 8, 16)
POOL_GROUP = 256
C_B = len(POOL_WINDOWS) * POOL_GROUP
POOL_HIST = max(POOL_WINDOWS) - 1
H_C = 8
DH_C = 128
C_C = H_C * DH_C
H_IDX = 32
D_IDX = 64
TOPK_MAX = 256
IDX_SCALE = (H_IDX * D_IDX) ** -0.5
Q_BLOCK = 64
N_BRANCH = 3
IN_COLS = A_COLS + C_B + 3 * C_C + H_IDX * D_IDX + D_IDX + H_IDX + N_BRANCH * D_MODEL
ROPE_THETA = 10000.0
RMS_EPS = 1e-6

kernel_name = "hybrid_rwkv7_pool_dsa_step"


def _split_points(sizes):
    pts, acc = [], 0
    for s in sizes[:-1]:
        acc += s
        pts.append(acc)
    return pts


def _rmsnorm(x, g):
    xf = x.astype(jnp.float32)
    y = xf * lax.rsqrt(jnp.mean(xf * xf, axis=-1, keepdims=True) + RMS_EPS) * g.astype(jnp.float32)
    return y.astype(x.dtype)


def _swiglu(x, wg, wu, wd):
    return (jax.nn.silu(x @ wg) * (x @ wu)) @ wd


def _rope(x, pos):
    d = x.shape[-1]
    inv = ROPE_THETA ** (-jnp.arange(0, d, 2, dtype=jnp.float32) / d)
    ang = pos.astype(jnp.float32)[:, None] * inv[None, :]
    cos = jnp.cos(ang)[:, None, :]
    sin = jnp.sin(ang)[:, None, :]
    xf = x.astype(jnp.float32)
    x1, x2 = xf[..., : d // 2], xf[..., d // 2:]
    return jnp.concatenate([x1 * cos - x2 * sin, x1 * sin + x2 * cos], axis=-1).astype(x.dtype)


def _gather_rows(rows, idx):
    return jax.vmap(lambda r, i: r[i])(rows, idx)


def _rwkv_branch(pa, shift_prev, s0, mu, w0, w2, a0, a2, g2, k_k, k_a, r_k, ln_w, ln_b):
    f32 = jnp.float32
    B, T, _ = pa.shape
    prev = jnp.concatenate([shift_prev[:, None, :].astype(pa.dtype), pa[:, :-1]], axis=1)
    xs = pa + mu * (prev - pa)
    r, k, v, wl, al, gl = jnp.split(xs, _split_points([C_A, C_A, C_A, D_DECAY, D_AAA, D_GATE]), axis=-1)
    w_log = -jax.nn.softplus(-(w0 + jnp.tanh(wl) @ w2).astype(f32)) - 0.5
    decay = jnp.exp(-jnp.exp(w_log)).reshape(B, T, H_A, N_A)
    a = jax.nn.sigmoid((a0 + al @ a2).astype(f32)).reshape(B, T, H_A, N_A)
    g = (jax.nn.sigmoid(gl) @ g2).astype(f32)
    hs = lambda t: t.astype(f32).reshape(B, T, H_A, N_A)
    kk = hs(k * k_k)
    kk = kk / jnp.maximum(jnp.sqrt(jnp.sum(kk * kk, axis=-1, keepdims=True)), 1e-12)
    ka = k_a.astype(f32).reshape(H_A, N_A)
    kh = hs(k) * (1.0 + (a - 1.0) * ka)
    rh, vh = hs(r), hs(v)

    def step(S, inp):
        r_t, w_t, k_t, v_t, kk_t, a_t = inp
        sa = jnp.einsum('bhvk,bhk->bhv', S, -kk_t)
        S = S * w_t[:, :, None, :] + sa[..., None] * (kk_t * a_t)[:, :, None, :] + v_t[..., None] * k_t[:, :, None, :]
        return S, jnp.einsum('bhvk,bhk->bhv', S, r_t)

    tm = lambda t: jnp.moveaxis(t, 1, 0)
    s_T, ys = lax.scan(step, s0.astype(f32), (tm(rh), tm(decay), tm(kh), tm(vh), tm(kk), tm(a)))
    y = jnp.moveaxis(ys, 0, 1)
    mean = jnp.mean(y, axis=-1, keepdims=True)
    var = jnp.mean(jnp.square(y - mean), axis=-1, keepdims=True)
    y = ((y - mean) * lax.rsqrt(var + GN_EPS)).reshape(B, T, C_A) * ln_w.astype(f32) + ln_b.astype(f32)
    bonus = jnp.sum(rh * kh * r_k.astype(f32), axis=-1, keepdims=True) * vh
    y = (y + bonus.reshape(B, T, C_A)) * g
    return y.astype(pa.dtype), pa[:, -1], s_T.astype(s0.dtype)


def _pool_branch(z, hist, pos0, pool_w, pool_scale):
    f32 = jnp.float32
    B, T, _ = z.shape
    P = POOL_HIST
    zc = jnp.concatenate([hist.astype(z.dtype), z], axis=1)
    cs = jnp.cumsum(zc.astype(f32), axis=1)
    cs = jnp.concatenate([jnp.zeros((B, 1, C_B), f32), cs], axis=1)
    pos = pos0 + jnp.arange(T)
    outs = []
    for gi, w in enumerate(POOL_WINDOWS):
        lo, hi = gi * POOL_GROUP, (gi + 1) * POOL_GROUP
        win_sum = cs[:, P + 1:P + T + 1, lo:hi] - cs[:, P + 1 - w:P + T + 1 - w, lo:hi]
        cnt = jnp.minimum(pos + 1, w).astype(f32)[None, :, None]
        d = win_sum / cnt - zc[:, P:, lo:hi].astype(f32)
        outs.append(jnp.einsum('btc,cd->btd', d, pool_w[gi].astype(f32)))
    y = jnp.concatenate(outs, axis=-1) * pool_scale.astype(f32)
    return y.astype(z.dtype), zc[:, -P:]


def _index_scores(qi, wi, ki):
    s = jnp.einsum('bqhd,bsd->bqhs', qi.astype(jnp.float32), ki.astype(jnp.float32))
    return jnp.einsum('bqhs,bqh->bqs', jax.nn.relu(s), wi.astype(jnp.float32))


def _sparse_attend(q, q_pos, sel, k_sel, v_sel):
    logits = jnp.einsum('bqhd,bqkhd->bqhk', q.astype(jnp.float32), k_sel.astype(jnp.float32)) * (DH_C ** -0.5)
    valid = (sel <= q_pos[None, :, None])[:, :, None, :]
    p = jax.nn.softmax(jnp.where(valid, logits, -jnp.inf), axis=-1)
    return jnp.einsum('bqhk,bqkhd->bqhd', p, v_sel.astype(jnp.float32)).astype(q.dtype)


def _dsa_prompt(q, k, v, qi, ki, wi):
    B, T = q.shape[:2]
    topk = min(TOPK_MAX, T // 4)
    n_blocks = T // Q_BLOCK
    spos = jnp.arange(T)

    def blk(i):
        t0 = i * Q_BLOCK
        qb = lax.dynamic_slice_in_dim(q, t0, Q_BLOCK, axis=1)
        qib = lax.dynamic_slice_in_dim(qi, t0, Q_BLOCK, axis=1)
        wib = lax.dynamic_slice_in_dim(wi, t0, Q_BLOCK, axis=1)
        qpos = t0 + jnp.arange(Q_BLOCK)
        sc = _index_scores(qib, wib, ki)
        sc = jnp.where(spos[None, None, :] <= qpos[None, :, None], sc, -jnp.inf)
        _, sel = lax.top_k(sc, topk)
        return _sparse_attend(qb, qpos, sel, _gather_rows(k, sel), _gather_rows(v, sel))

    o = lax.map(blk, jnp.arange(n_blocks))
    return jnp.moveaxis(o, 0, 1).reshape(B, T, C_C)


def _dsa_sample(q, k, v, qi, ki, wi, ck, cv, cki, page_table):
    B, T = q.shape[:2]
    past = page_table.shape[1] * PAGE_SIZE
    L = past + T
    topk = min(TOPK_MAX, L // 4)
    ki_past = cki[page_table].reshape(B, past, D_IDX)
    ki_all = jnp.concatenate([ki_past, ki.astype(ki_past.dtype)], axis=1)
    qpos = past + jnp.arange(T)
    sc = _index_scores(qi, wi, ki_all)
    sc = jnp.where(jnp.arange(L)[None, None, :] <= qpos[None, :, None], sc, -jnp.inf)
    _, sel = lax.top_k(sc, topk)
    is_new = (sel >= past)[..., None, None]
    sp = jnp.minimum(sel, past - 1)
    phys = jax.vmap(lambda pt, ix: pt[ix])(page_table, sp // PAGE_SIZE) * PAGE_SIZE + sp % PAGE_SIZE
    sn = jnp.clip(sel - past, 0, T - 1)
    ck_flat = ck.reshape(-1, H_C, DH_C)
    cv_flat = cv.reshape(-1, H_C, DH_C)
    k_sel = jnp.where(is_new, _gather_rows(k, sn).astype(ck.dtype), ck_flat[phys])
    v_sel = jnp.where(is_new, _gather_rows(v, sn).astype(cv.dtype), cv_flat[phys])
    return _sparse_attend(q, qpos, sel, k_sel, v_sel).reshape(B, T, C_C)


def _layer(x, P, l, pos0, attn_fn, shift_prev, s0, pool_hist):
    B, T, _ = x.shape
    h = x + 0.5 * _swiglu(_rmsnorm(x, P['ffn1_norm'][l]), P['ffn1_w_gate'][l], P['ffn1_w_up'][l], P['ffn1_w_down'][l])
    u = _rmsnorm(h, P['mix_norm'][l])
    sizes = [A_COLS, C_B, C_C, C_C, C_C, H_IDX * D_IDX, D_IDX, H_IDX, N_BRANCH * D_MODEL]
    pa, zb, q, k, v, qi, ki, wi, gl = jnp.split(u @ P['w_in'][l], _split_points(sizes), axis=-1)
    pos = pos0 + jnp.arange(T)
    q = _rope(q.reshape(B, T, H_C, DH_C), pos)
    k = _rope(k.reshape(B, T, H_C, DH_C), pos)
    v = v.reshape(B, T, H_C, DH_C)
    qi = _rope(qi.reshape(B, T, H_IDX, D_IDX), pos)
    ki = _rope(ki[:, :, None, :], pos)[:, :, 0, :]
    wi = wi * IDX_SCALE
    ya, new_shift, new_wkv = _rwkv_branch(
        pa, shift_prev, s0, P['rwkv_mu'][l], P['rwkv_w0'][l], P['rwkv_w2'][l], P['rwkv_a0'][l],
        P['rwkv_a2'][l], P['rwkv_g2'][l], P['rwkv_k_k'][l], P['rwkv_k_a'][l], P['rwkv_r_k'][l],
        P['rwkv_ln_w'][l], P['rwkv_ln_b'][l])
    yb, new_pool = _pool_branch(zb, pool_hist, pos0, P['pool_w'][l], P['pool_scale'][l])
    yc = attn_fn(q, k, v, qi, ki, wi)
    g = jax.nn.sigmoid(gl.reshape(B, T, N_BRANCH, D_MODEL))
    merged = (g[:, :, 0] * (ya @ P['w_br_a'][l]) + g[:, :, 1] * (yb @ P['w_br_b'][l])
              + g[:, :, 2] * (yc @ P['w_br_c'][l]))
    h = h + merged @ P['w_out'][l]
    h = h + 0.5 * _swiglu(_rmsnorm(h, P['ffn2_norm'][l]), P['ffn2_w_gate'][l], P['ffn2_w_up'][l], P['ffn2_w_down'][l])
    return h, (k, v, ki, new_wkv, new_shift, new_pool)


def setup_inputs(seed: int = 0) -> dict:
    key = jax.random.key(seed)
    ks = iter(jax.random.split(key, 64))
    f32 = jnp.float32
    nrm = lambda shape, scale: jax.random.normal(next(ks), shape, f32) * scale
    gain = lambda shape: 1.0 + nrm(shape, 0.05)
    L = DEPTH
    n_pages = PAST_LEN // PAGE_SIZE
    n_phys = (DEC_BATCH * n_pages * 5) // 4
    perm = jax.random.permutation(next(ks), n_phys)[: DEC_BATCH * n_pages]
    page_table = perm.reshape(DEC_BATCH, n_pages).astype(jnp.int32)
    return {
        'x_prompt': nrm((BATCH, SEQ, D_MODEL), 1.0),
        'x_sample': nrm((DEC_BATCH, DEC_SEQ, D_MODEL), 1.0),
        'cache_k': nrm((L, n_phys, PAGE_SIZE, H_C, DH_C), 1.0),
        'cache_v': nrm((L, n_phys, PAGE_SIZE, H_C, DH_C), 1.0),
        'cache_kidx': nrm((L, n_phys, PAGE_SIZE, D_IDX), 1.0),
        'state_wkv': nrm((L, DEC_BATCH, H_A, N_A, N_A), 0.5),
        'state_shift': nrm((L, DEC_BATCH, A_COLS), 1.0),
        'state_pool': nrm((L, DEC_BATCH, POOL_HIST, C_B), 1.0),
        'page_table': page_table,
        'ffn1_norm': gain((L, D_MODEL)),
        'ffn1_w_gate': nrm((L, D_MODEL, D_FF), D_MODEL ** -0.5),
        'ffn1_w_up': nrm((L, D_MODEL, D_FF), D_MODEL ** -0.5),
        'ffn1_w_down': nrm((L, D_FF, D_MODEL), D_FF ** -0.5),
        'mix_norm': gain((L, D_MODEL)),
        'w_in': nrm((L, D_MODEL, IN_COLS), D_MODEL ** -0.5),
        'rwkv_mu': jax.random.uniform(next(ks), (L, A_COLS), f32),
        'rwkv_w0': -0.5 + nrm((L, C_A), 0.5),
        'rwkv_w2': nrm((L, D_DECAY, C_A), 0.5 * D_DECAY ** -0.5),
        'rwkv_a0': nrm((L, C_A), 0.1),
        'rwkv_a2': nrm((L, D_AAA, C_A), 0.5 * D_AAA ** -0.5),
        'rwkv_g2': nrm((L, D_GATE, C_A), D_GATE ** -0.5),
        'rwkv_k_k': 0.85 + nrm((L, C_A), 0.05),
        'rwkv_k_a': gain((L, C_A)),
        'rwkv_r_k': nrm((L, H_A, N_A), 0.1),
        'rwkv_ln_w': gain((L, C_A)),
        'rwkv_ln_b': nrm((L, C_A), 0.01),
        'w_br_a': nrm((L, C_A, D_MODEL), C_A ** -0.5),
        'pool_w': nrm((L, len(POOL_WINDOWS), POOL_GROUP, POOL_GROUP), POOL_GROUP ** -0.5),
        'pool_scale': gain((L, C_B)),
        'w_br_b': nrm((L, C_B, D_MODEL), C_B ** -0.5),
        'w_br_c': nrm((L, C_C, D_MODEL), C_C ** -0.5),
        'w_out': nrm((L, D_MODEL, D_MODEL), D_MODEL ** -0.5),
        'ffn2_norm': gain((L, D_MODEL)),
        'ffn2_w_gate': nrm((L, D_MODEL, D_FF), D_MODEL ** -0.5),
        'ffn2_w_up': nrm((L, D_MODEL, D_FF), D_MODEL ** -0.5),
        'ffn2_w_down': nrm((L, D_FF, D_MODEL), D_FF ** -0.5),
        'final_norm': gain((D_MODEL,)),
    }


def reference(x_prompt, x_sample, cache_k, cache_v, cache_kidx, state_wkv, state_shift, state_pool, page_table,
              ffn1_norm, ffn1_w_gate, ffn1_w_up, ffn1_w_down, mix_norm, w_in, rwkv_mu, rwkv_w0, rwkv_w2,
              rwkv_a0, rwkv_a2, rwkv_g2, rwkv_k_k, rwkv_k_a, rwkv_r_k, rwkv_ln_w, rwkv_ln_b, w_br_a, pool_w,
              pool_scale, w_br_b, w_br_c, w_out, ffn2_norm, ffn2_w_gate, ffn2_w_up, ffn2_w_down, final_norm):
    P = dict(ffn1_norm=ffn1_norm, ffn1_w_gate=ffn1_w_gate, ffn1_w_up=ffn1_w_up, ffn1_w_down=ffn1_w_down,
             mix_norm=mix_norm, w_in=w_in, rwkv_mu=rwkv_mu, rwkv_w0=rwkv_w0, rwkv_w2=rwkv_w2, rwkv_a0=rwkv_a0,
             rwkv_a2=rwkv_a2, rwkv_g2=rwkv_g2, rwkv_k_k=rwkv_k_k, rwkv_k_a=rwkv_k_a, rwkv_r_k=rwkv_r_k,
             rwkv_ln_w=rwkv_ln_w, rwkv_ln_b=rwkv_ln_b, w_br_a=w_br_a, pool_w=pool_w, pool_scale=pool_scale,
             w_br_b=w_br_b, w_br_c=w_br_c, w_out=w_out, ffn2_norm=ffn2_norm, ffn2_w_gate=ffn2_w_gate,
             ffn2_w_up=ffn2_w_up, ffn2_w_down=ffn2_w_down)
    bp = x_prompt.shape[0]
    past = page_table.shape[1] * PAGE_SIZE
    yp, ys = x_prompt, x_sample
    st_p, st_s = [], []
    for l in range(DEPTH):
        zero_shift = jnp.zeros((bp, A_COLS), x_prompt.dtype)
        zero_wkv = jnp.zeros((bp, H_A, N_A, N_A), jnp.float32)
        zero_pool = jnp.zeros((bp, POOL_HIST, C_B), x_prompt.dtype)
        yp, sp = _layer(yp, P, l, 0, _dsa_prompt, zero_shift, zero_wkv, zero_pool)
        attn_s = functools.partial(_dsa_sample, ck=cache_k[l], cv=cache_v[l], cki=cache_kidx[l], page_table=page_table)
        ys, ss = _layer(ys, P, l, past, attn_s, state_shift[l], state_wkv[l], state_pool[l])
        st_p.append(sp)
        st_s.append(ss)
    y_prompt = _rmsnorm(yp, final_norm)
    y_sample = _rmsnorm(ys, final_norm)
    stk = lambda sts, i: jnp.stack([s[i] for s in sts])
    return (y_prompt, y_sample,
            stk(st_p, 0), stk(st_p, 1), stk(st_p, 2), stk(st_p, 3), stk(st_p, 4), stk(st_p, 5),
            stk(st_s, 0), stk(st_s, 1), stk(st_s, 2), stk(st_s, 3), stk(st_s, 4), stk(st_s, 5))
```

```python
import functools
import math

import jax
import jax.numpy as jnp
from jax import lax
from jax.experimental import pallas as pl
from jax.experimental.pallas import tpu as pltpu

F32 = jnp.float32
BF16 = jnp.bfloat16

LANE = 128
SUBLANE = 8
VMEM_LIMIT_BYTES = 56 * 1024 * 1024

N_A = 64
GN_EPS = 64e-5
POOL_WINDOWS = (2, 4, 8, 16)
POOL_HIST = max(POOL_WINDOWS) - 1
TOPK_MAX = 256
ROPE_THETA = 10000.0
RMS_EPS = 1e-6
PAGE = 128
NEG_BIG = -1e30


def _cparams(n_axes):
    return pltpu.CompilerParams(dimension_semantics=("arbitrary",) * n_axes,
                                vmem_limit_bytes=VMEM_LIMIT_BYTES)


def _tile(n, target, mult):
    best = None
    for t in range(mult, min(n, target) + 1, mult):
        if n % t == 0:
            best = t
    return best if best is not None else n


def _roundup(n, m):
    return -(-n // m) * m


def _dot(a, b):
    return jnp.dot(a.astype(BF16), b.astype(BF16), preferred_element_type=F32)


def _dot_nt(a, b):
    return lax.dot_general(a.astype(BF16), b.astype(BF16), (((1,), (1,)), ((), ())),
                           preferred_element_type=F32)


def _split3(x):
    h1 = x.astype(BF16)
    r1 = x - h1.astype(F32)
    h2 = r1.astype(BF16)
    h3 = (r1 - h2.astype(F32)).astype(BF16)
    return h1, h2, h3


def _dot_exact_rhs(x, m_bf16):
    h1, h2, h3 = _split3(x)
    return (jnp.dot(h1, m_bf16, preferred_element_type=F32)
            + jnp.dot(h2, m_bf16, preferred_element_type=F32)
            + jnp.dot(h3, m_bf16, preferred_element_type=F32))


def _dot_exact_lhs(m_bf16, x):
    h1, h2, h3 = _split3(x)
    return (jnp.dot(m_bf16, h1, preferred_element_type=F32)
            + jnp.dot(m_bf16, h2, preferred_element_type=F32)
            + jnp.dot(m_bf16, h3, preferred_element_type=F32))


def _rmsnorm_kernel(x_ref, g_ref, o_ref):
    x = x_ref[...]
    ms = jnp.mean(x * x, axis=-1, keepdims=True)
    o_ref[...] = (x * lax.rsqrt(ms + RMS_EPS) * g_ref[...]).astype(o_ref.dtype)


def _rmsnorm(x2d, g, out_dtype):
    m, d = x2d.shape
    tm = _tile(m, 256, SUBLANE)
    return pl.pallas_call(
        _rmsnorm_kernel,
        grid=(m // tm,),
        in_specs=[pl.BlockSpec((tm, d), lambda i: (i, 0)), pl.BlockSpec((1, d), lambda i: (0, 0))],
        out_specs=pl.BlockSpec((tm, d), lambda i: (i, 0)),
        out_shape=jax.ShapeDtypeStruct((m, d), out_dtype),
        compiler_params=_cparams(1),
        name="rmsnorm",
    )(x2d, g.reshape(1, d))


def _mm_kernel(*refs, nk, has_res, res_scale, act):
    x_ref, w_ref = refs[0], refs[1]
    res_ref = refs[2] if has_res else None
    o_ref = refs[2 + int(has_res)]

    def epilogue(acc):
        if act == "sigmoid":
            acc = jax.nn.sigmoid(acc)
        if has_res:
            acc = res_ref[...] + res_scale * acc
        o_ref[...] = acc.astype(o_ref.dtype)

    part = jnp.dot(x_ref[...], w_ref[...], preferred_element_type=F32)
    if nk == 1:
        epilogue(part)
    else:
        acc_ref = refs[3 + int(has_res)]
        k = pl.program_id(2)

        @pl.when(k == 0)
        def _():
            acc_ref[...] = part

        @pl.when(k > 0)
        def _():
            acc_ref[...] += part

        @pl.when(k == nk - 1)
        def _():
            epilogue(acc_ref[...])


def _mm(x, w, *, out_dtype, res=None, res_scale=1.0, act=None, tm_target=1024, tn_target=1024,
        tk_target=4096, name="mm"):
    m, kdim = x.shape
    n = w.shape[1]
    tm = _tile(m, tm_target, SUBLANE)
    tn = _tile(n, tn_target, LANE)
    tk = _tile(kdim, tk_target, LANE)
    nk = kdim // tk
    in_specs = [pl.BlockSpec((tm, tk), lambda i, j, k: (i, k)),
                pl.BlockSpec((tk, tn), lambda i, j, k: (k, j))]
    args = [x, w]
    if res is not None:
        in_specs.append(pl.BlockSpec((tm, tn), lambda i, j, k: (i, j)))
        args.append(res)
    scratch = [pltpu.VMEM((tm, tn), F32)] if nk > 1 else []
    return pl.pallas_call(
        functools.partial(_mm_kernel, nk=nk, has_res=res is not None, res_scale=res_scale, act=act),
        grid=(m // tm, n // tn, nk),
        in_specs=in_specs,
        out_specs=pl.BlockSpec((tm, tn), lambda i, j, k: (i, j)),
        out_shape=jax.ShapeDtypeStruct((m, n), out_dtype),
        scratch_shapes=scratch,
        compiler_params=_cparams(3),
        name=name,
    )(*args)


def _swiglu_up_kernel(x_ref, wg_ref, wu_ref, o_ref):
    x = x_ref[...]
    g = jnp.dot(x, wg_ref[...], preferred_element_type=F32)
    u = jnp.dot(x, wu_ref[...], preferred_element_type=F32)
    o_ref[...] = (g * jax.nn.sigmoid(g) * u).astype(o_ref.dtype)


def _swiglu_up(x, wg, wu):
    m, d = x.shape
    f = wg.shape[1]
    tm = _tile(m, 1024, SUBLANE)
    tn = _tile(f, 512, LANE)
    return pl.pallas_call(
        _swiglu_up_kernel,
        grid=(m // tm, f // tn),
        in_specs=[pl.BlockSpec((tm, d), lambda i, j: (i, 0)),
                  pl.BlockSpec((d, tn), lambda i, j: (0, j)),
                  pl.BlockSpec((d, tn), lambda i, j: (0, j))],
        out_specs=pl.BlockSpec((tm, tn), lambda i, j: (i, j)),
        out_shape=jax.ShapeDtypeStruct((m, f), BF16),
        compiler_params=_cparams(2),
        name="swiglu_up",
    )(x, wg, wu)


def _ffn(x2d, norm_g, wg, wu, wd):
    xn = _rmsnorm(x2d, norm_g, BF16)
    a = _swiglu_up(xn, wg, wu)
    tk = _tile(a.shape[1], 5632, LANE)
    return _mm(a, wd, out_dtype=F32, res=x2d, res_scale=0.5, tm_target=1024, tn_target=512,
               tk_target=tk, name="ffn_down")


def _merge_kernel(ya_ref, yb_ref, yc_ref, wa_ref, wb_ref, wc_ref, ga_ref, gb_ref, gc_ref, o_ref):
    a = jnp.dot(ya_ref[...], wa_ref[...], preferred_element_type=F32)
    b = jnp.dot(yb_ref[...], wb_ref[...], preferred_element_type=F32)
    c = jnp.dot(yc_ref[...], wc_ref[...], preferred_element_type=F32)
    out = ga_ref[...].astype(F32) * a + gb_ref[...].astype(F32) * b + gc_ref[...].astype(F32) * c
    o_ref[...] = out.astype(o_ref.dtype)


def _merge(ya, yb, yc, wa, wb, wc, gates):
    m = ya.shape[0]
    d = wa.shape[1]
    tm = _tile(m, 1024, SUBLANE)
    tn = _tile(d, 512, LANE)
    nb = d // tn
    return pl.pallas_call(
        _merge_kernel,
        grid=(m // tm, nb),
        in_specs=[pl.BlockSpec((tm, ya.shape[1]), lambda i, j: (i, 0)),
                  pl.BlockSpec((tm, yb.shape[1]), lambda i, j: (i, 0)),
                  pl.BlockSpec((tm, yc.shape[1]), lambda i, j: (i, 0)),
                  pl.BlockSpec((wa.shape[0], tn), lambda i, j: (0, j)),
                  pl.BlockSpec((wb.shape[0], tn), lambda i, j: (0, j)),
                  pl.BlockSpec((wc.shape[0], tn), lambda i, j: (0, j)),
                  pl.BlockSpec((tm, tn), lambda i, j: (i, j)),
                  pl.BlockSpec((tm, tn), lambda i, j: (i, nb + j)),
                  pl.BlockSpec((tm, tn), lambda i, j: (i, 2 * nb + j))],
        out_specs=pl.BlockSpec((tm, tn), lambda i, j: (i, j)),
        out_shape=jax.ShapeDtypeStruct((m, d), BF16),
        compiler_params=_cparams(2),
        name="merge",
    )(ya, yb, yc, wa, wb, wc, gates, gates, gates)


def _rope_kernel(q_ref, k_ref, v_ref, qi_ref, kw_ref, c128_ref, s128_ref, c64_ref, s64_ref,
                 qo_ref, ko_ref, kb_ref, vb_ref, qio_ref, kwo_ref, *, n_heads, n_idx_groups, d_idx, idx_scale):
    c128, s128 = c128_ref[...], s128_ref[...]
    c64, s64 = c64_ref[...], s64_ref[...]
    lane = lax.broadcasted_iota(jnp.int32, c64.shape, 1)
    first_half = (lane % d_idx) < (d_idx // 2)

    def rope128(x):
        return x * c128 + pltpu.roll(x, LANE // 2, axis=1) * s128

    def rope64(x):
        rot = jnp.where(first_half, pltpu.roll(x, LANE - d_idx // 2, axis=1), pltpu.roll(x, d_idx // 2, axis=1))
        return x * c64 + rot * s64

    for h in range(n_heads):
        sl = slice(h * LANE, (h + 1) * LANE)
        qo_ref[0, :, sl] = rope128(q_ref[0, :, sl]).astype(qo_ref.dtype)
        kr = rope128(k_ref[0, :, sl])
        ko_ref[0, :, sl] = kr
        kb_ref[0, :, sl] = kr.astype(kb_ref.dtype)
    vb_ref[0] = v_ref[0].astype(vb_ref.dtype)
    for g in range(n_idx_groups):
        sl = slice(g * LANE, (g + 1) * LANE)
        qio_ref[0, :, sl] = rope64(qi_ref[0, :, sl])
    kw = kw_ref[0]
    kwo_ref[0] = jnp.where(lane < d_idx, rope64(kw), kw * idx_scale)


def _rope_tables(pos, d):
    inv = ROPE_THETA ** (-jnp.arange(0, d, 2, dtype=F32) / d)
    ang = pos.astype(F32)[:, None] * inv[None, :]
    cos, sin = jnp.cos(ang), jnp.sin(ang)
    reps = LANE // d
    c = jnp.tile(jnp.concatenate([cos, cos], axis=-1), (1, reps))
    s = jnp.tile(jnp.concatenate([-sin, sin], axis=-1), (1, reps))
    return c, s


def _rope_all(mid, pos0, dims):
    b, t, _ = mid.shape
    cb, cc, qiw, d_idx, h_idx = dims["C_B"], dims["C_C"], dims["QIW"], dims["D_IDX"], dims["H_IDX"]
    assert dims["DH_C"] == LANE and LANE % d_idx == 0
    assert cb % cc == 0 and (cb + 3 * cc) % qiw == 0
    tm = _tile(t, 256, SUBLANE)
    pos = pos0 + jnp.arange(t)
    c128, s128 = _rope_tables(pos, LANE)
    c64, s64 = _rope_tables(pos, d_idx)
    ob = cb // cc
    row = lambda width, idx: pl.BlockSpec((1, tm, width), lambda i, j: (i, j, idx))
    tab = pl.BlockSpec((tm, LANE), lambda i, j: (j, 0))
    out = lambda width: pl.BlockSpec((1, tm, width), lambda i, j: (i, j, 0))
    return pl.pallas_call(
        functools.partial(_rope_kernel, n_heads=cc // LANE, n_idx_groups=qiw // LANE, d_idx=d_idx,
                          idx_scale=(h_idx * d_idx) ** -0.5),
        grid=(b, t // tm),
        in_specs=[row(cc, ob + 1), row(cc, ob + 2), row(cc, ob), row(qiw, (cb + 3 * cc) // qiw),
                  row(LANE, (cb + 3 * cc + qiw) // LANE), tab, tab, tab, tab],
        out_specs=[out(cc), out(cc), out(cc), out(cc), out(qiw), out(LANE)],
        out_shape=[jax.ShapeDtypeStruct((b, t, cc), BF16), jax.ShapeDtypeStruct((b, t, cc), F32),
                   jax.ShapeDtypeStruct((b, t, cc), BF16), jax.ShapeDtypeStruct((b, t, cc), BF16),
                   jax.ShapeDtypeStruct((b, t, qiw), F32), jax.ShapeDtypeStruct((b, t, LANE), F32)],
        compiler_params=_cparams(2),
        name="rope",
    )(mid, mid, mid, mid, mid, c128, s128, c64, s64)


def _head_sum(x, jmat):
    parts = []
    for g in range(x.shape[1] // LANE):
        parts.append(_dot_exact_rhs(x[:, g * LANE:(g + 1) * LANE], jmat))
    return parts[0] if len(parts) == 1 else jnp.concatenate(parts, axis=1)


def _head_ones():
    r = lax.broadcasted_iota(jnp.int32, (LANE, LANE), 0)
    c = lax.broadcasted_iota(jnp.int32, (LANE, LANE), 1)
    return jnp.where((r // N_A) == (c // N_A), 1.0, 0.0).astype(BF16)


def _rwkv_prep_kernel(pa_ref, halo_ref, shift_ref, mu_ref, w0_ref, w2_ref, a0_ref, a2_ref, g2_ref,
                      kk_ref, ka_ref, rk_ref,
                      rt_ref, at_ref, bt_ref, kt_ref, v_ref, gam_ref, bonus_ref, g_ref, *, ca, chunk, tm):
    j = pl.program_id(1)
    prev_last = jnp.where(j == 0, shift_ref[0], halo_ref[0, SUBLANE - 1:SUBLANE, :])
    row = lax.broadcasted_iota(jnp.int32, (tm, 1), 0)

    def shifted(lo, hi):
        x = pa_ref[0, :, lo:hi]
        prev = jnp.where(row == 0, prev_last[:, lo:hi], pltpu.roll(x, 1, axis=0))
        return x + mu_ref[:, lo:hi] * (prev - x)

    r = shifted(0, ca)
    k = shifted(ca, 2 * ca)
    v = shifted(2 * ca, 3 * ca)
    wl = shifted(3 * ca, 3 * ca + LANE)
    al = shifted(3 * ca + LANE, 3 * ca + 2 * LANE)
    gl = shifted(3 * ca + 2 * LANE, pa_ref.shape[2])

    z = -(w0_ref[...] + _dot(jnp.tanh(wl), w2_ref[...]))
    softplus = jnp.maximum(z, 0.0) + jnp.log(1.0 + jnp.exp(-jnp.abs(z)))
    log_decay = -jnp.exp(-softplus - 0.5)
    rr = lax.broadcasted_iota(jnp.int32, (tm, tm), 0)
    cc = lax.broadcasted_iota(jnp.int32, (tm, tm), 1)
    tril = jnp.where(((rr // chunk) == (cc // chunk)) & (cc <= rr), 1.0, 0.0).astype(BF16)
    cum = _dot_exact_lhs(tril, log_decay)
    gam = jnp.exp(cum)
    gam_inv = jnp.exp(-cum)
    gam_prev = jnp.exp(cum - log_decay)

    a = jax.nn.sigmoid(a0_ref[...] + _dot(al, a2_ref[...]))
    jmat = _head_ones()
    kk = k * kk_ref[...]
    kk = kk / jnp.maximum(jnp.sqrt(_head_sum(kk * kk, jmat)), 1e-12)
    kh = k * (1.0 + (a - 1.0) * ka_ref[...])
    bonus_ref[0] = _head_sum(r * kh * rk_ref[...], jmat) * v
    g_ref[0] = _dot(jax.nn.sigmoid(gl), g2_ref[...])
    rt_ref[0] = r * gam
    at_ref[0] = -kk * gam_prev
    bt_ref[0] = kk * a * gam_inv
    kt_ref[0] = kh * gam_inv
    v_ref[0] = v
    gam_ref[0] = gam


def _rwkv_prep(pa, shift_p, P, l, dims, chunk):
    b, t, paw = pa.shape
    ca = dims["C_A"]
    tm = chunk * max(1, min(128, t) // chunk)
    assert t % tm == 0
    hb = tm // SUBLANE
    row = pl.BlockSpec((1, tm, ca), lambda i, j: (i, j, 0))
    vec = lambda w: pl.BlockSpec((1, w), lambda i, j: (0, 0))
    mat = lambda r: pl.BlockSpec((r, ca), lambda i, j: (0, 0))
    outs = pl.pallas_call(
        functools.partial(_rwkv_prep_kernel, ca=ca, chunk=chunk, tm=tm),
        grid=(b, t // tm),
        in_specs=[pl.BlockSpec((1, tm, paw), lambda i, j: (i, j, 0)),
                  pl.BlockSpec((1, SUBLANE, paw), lambda i, j: (i, jnp.maximum(j * hb - 1, 0), 0)),
                  pl.BlockSpec((1, 1, paw), lambda i, j: (i, 0, 0)),
                  vec(paw), vec(ca), mat(LANE), vec(ca), mat(LANE), mat(P["g2"].shape[1]),
                  vec(ca), vec(ca), vec(ca)],
        out_specs=[row] * 8,
        out_shape=[jax.ShapeDtypeStruct((b, t, ca), F32)] * 8,
        compiler_params=_cparams(2),
        name="rwkv_prep",
    )(pa, pa, shift_p.reshape(b, 1, paw), P["mu"][l], P["w0"][l], P["w2"][l], P["a0"][l], P["a2"][l],
      P["g2"][l], P["k_k"][l], P["k_a"][l], P["r_k"][l])
    return outs


def _rwkv_scan_kernel(at_ref, rt_ref, vt_ref, b_ref, k_ref, gc_ref, s0_ref, yt_ref, s_ref, *, n_heads, chunk):
    c = pl.program_id(1)

    @pl.when(c == 0)
    def _():
        s_ref[...] = s0_ref[...]

    rr = lax.broadcasted_iota(jnp.int32, (chunk, chunk), 0)
    cc = lax.broadcasted_iota(jnp.int32, (chunk, chunk), 1)
    upper_strict = rr < cc
    upper_incl = rr <= cc
    eye = jnp.where(rr == cc, 1.0, 0.0)
    n_double = max(int(math.log2(chunk)) - 1, 0)

    def head(h, carry):
        rows = pl.ds(pl.multiple_of(h * N_A, N_A), N_A)
        a_t = at_ref[0, 0, rows, :]
        r_t = rt_ref[0, 0, rows, :]
        v_t = vt_ref[0, 0, rows, :]
        bm = b_ref[0, 0, h]
        km = k_ref[0, 0, h]
        s = s_ref[0, h]
        n_ab = jnp.where(upper_strict, _dot(bm, a_t), 0.0)
        n_ak = jnp.where(upper_strict, _dot(km, a_t), 0.0)
        m_rb = jnp.where(upper_incl, _dot(bm, r_t), 0.0)
        m_rk = jnp.where(upper_incl, _dot(km, r_t), 0.0)
        inv = eye + n_ab
        npow = n_ab
        for _ in range(n_double):
            npow = _dot(npow, npow)
            inv = inv + _dot(inv, npow)
        x_t = _dot(s, a_t) + _dot(v_t, n_ak)
        u_t = _dot(x_t, inv)
        yt_ref[0, 0, rows, :] = _dot(s, r_t) + _dot(u_t, m_rb) + _dot(v_t, m_rk)
        s_ref[0, h] = (s + _dot(u_t, bm) + _dot(v_t, km)) * gc_ref[0, 0, h]
        return carry

    lax.fori_loop(0, n_heads, head, 0)


def _rwkv_scan(rt, at, bt, kt, v, gam, s0, chunk):
    b, t, ca = rt.shape
    h = ca // N_A
    nc = t // chunk
    to_cm = lambda x: jnp.swapaxes(x.reshape(b, nc, chunk, ca), 2, 3)
    to_hm = lambda x: jnp.transpose(x.reshape(b, nc, chunk, h, N_A), (0, 1, 3, 2, 4))
    gc = gam.reshape(b, nc, chunk, h, 1, N_A)[:, :, chunk - 1]
    cm = pl.BlockSpec((1, 1, ca, chunk), lambda i, j: (i, j, 0, 0))
    hm = pl.BlockSpec((1, 1, h, chunk, N_A), lambda i, j: (i, j, 0, 0, 0))
    st = pl.BlockSpec((1, h, N_A, N_A), lambda i, j: (i, 0, 0, 0))
    yt, s_new = pl.pallas_call(
        functools.partial(_rwkv_scan_kernel, n_heads=h, chunk=chunk),
        grid=(b, nc),
        in_specs=[cm, cm, cm, hm, hm, pl.BlockSpec((1, 1, h, 1, N_A), lambda i, j: (i, j, 0, 0, 0)), st],
        out_specs=[cm, st],
        out_shape=[jax.ShapeDtypeStruct((b, nc, ca, chunk), F32), jax.ShapeDtypeStruct((b, h, N_A, N_A), F32)],
        compiler_params=_cparams(2),
        name="rwkv_scan",
    )(to_cm(at), to_cm(rt), to_cm(v), to_hm(bt), to_hm(kt), gc, s0)
    return jnp.swapaxes(yt, 2, 3).reshape(b, t, ca), s_new


def _rwkv_post_kernel(y_ref, bonus_ref, g_ref, lnw_ref, lnb_ref, o_ref):
    jmat = _head_ones()
    y = y_ref[...]
    mean = _head_sum(y, jmat) * (1.0 / N_A)
    d = y - mean
    var = _head_sum(d * d, jmat) * (1.0 / N_A)
    out = (d * lax.rsqrt(var + GN_EPS) * lnw_ref[...] + lnb_ref[...] + bonus_ref[...]) * g_ref[...]
    o_ref[...] = out.astype(o_ref.dtype)


def _rwkv_post(y, bonus, g, ln_w, ln_b):
    m, ca = y.shape
    tm = _tile(m, 256, SUBLANE)
    row = pl.BlockSpec((tm, ca), lambda i: (i, 0))
    vec = pl.BlockSpec((1, ca), lambda i: (0, 0))
    return pl.pallas_call(
        _rwkv_post_kernel,
        grid=(m // tm,),
        in_specs=[row, row, row, vec, vec],
        out_specs=row,
        out_shape=jax.ShapeDtypeStruct((m, ca), BF16),
        compiler_params=_cparams(1),
        name="rwkv_post",
    )(y, bonus, g, ln_w, ln_b)


def _pool_kernel(z_ref, hist_ref, w_ref, scale_ref, o_ref, x_scr, *, t, tm, group, pos0):
    halo = 2 * SUBLANE
    x_scr[0:halo, :] = hist_ref[0]
    x_scr[halo:halo + t, :] = z_ref[0]
    for i in range(t // tm):
        r0 = i * tm
        pos = pos0 + r0 + lax.broadcasted_iota(jnp.int32, (tm, 1), 0)
        for gi, win in enumerate(POOL_WINDOWS):
            lo, hi = gi * group, (gi + 1) * group
            cur = x_scr[halo + r0:halo + r0 + tm, lo:hi]
            tot = cur
            for back in range(1, win):
                tot = tot + x_scr[halo + r0 - back:halo + r0 - back + tm, lo:hi]
            cnt = jnp.minimum(pos + 1, win).astype(F32)
            d = tot / cnt - cur
            y = _dot(d, w_ref[gi]) * scale_ref[:, lo:hi]
            o_ref[0, r0:r0 + tm, lo:hi] = y.astype(o_ref.dtype)


def _pool(mid, hist16, pool_w, pool_scale, pos0, dims):
    b, t, _ = mid.shape
    cb = dims["C_B"]
    group = pool_w.shape[1]
    assert len(POOL_WINDOWS) * group == cb
    tm = _tile(t, 256, SUBLANE)
    return pl.pallas_call(
        functools.partial(_pool_kernel, t=t, tm=tm, group=group, pos0=pos0),
        grid=(b,),
        in_specs=[pl.BlockSpec((1, t, cb), lambda i: (i, 0, 0)),
                  pl.BlockSpec((1, 2 * SUBLANE, cb), lambda i: (i, 0, 0)),
                  pl.BlockSpec(pool_w.shape, lambda i: (0, 0, 0)),
                  pl.BlockSpec((1, cb), lambda i: (0, 0))],
        out_specs=pl.BlockSpec((1, t, cb), lambda i: (i, 0, 0)),
        out_shape=jax.ShapeDtypeStruct((b, t, cb), BF16),
        scratch_shapes=[pltpu.VMEM((t + 2 * SUBLANE, cb), F32)],
        compiler_params=_cparams(1),
        name="pool",
    )(mid, hist16, pool_w, pool_scale.reshape(1, cb))


def _monotone_key(x):
    x = jnp.where(x == 0.0, 0.0, x)
    bits = lax.bitcast_convert_type(x, jnp.int32)
    return jnp.where(bits < 0, bits ^ jnp.int32(0x7FFFFFFF), bits)


def _kth_largest_key(key, k):
    def body(i, tau):
        cand = tau + jnp.left_shift(jnp.int32(1), jnp.int32(31) - i)
        cnt = jnp.sum(jnp.where(key >= cand, 1.0, 0.0), axis=-1, keepdims=True)
        return jnp.where(cnt >= k, cand, tau)

    tau0 = jnp.full((key.shape[0], 1), -2 ** 31, jnp.int32)
    return lax.fori_loop(0, 32, body, tau0)


def _prefix_count(ind):
    r = lax.broadcasted_iota(jnp.int32, (LANE, LANE), 0)
    c = lax.broadcasted_iota(jnp.int32, (LANE, LANE), 1)
    tri = jnp.where(r <= c, 1.0, 0.0).astype(BF16)
    run = jnp.zeros((ind.shape[0], 1), F32)
    outs = []
    for j in range(ind.shape[1] // LANE):
        pj = jnp.dot(ind[:, j * LANE:(j + 1) * LANE].astype(BF16), tri, preferred_element_type=F32)
        outs.append(pj + run)
        run = run + pj[:, LANE - 1:LANE]
    return outs[0] if len(outs) == 1 else jnp.concatenate(outs, axis=1)


def _topk_mask(scores, k):
    key = _monotone_key(scores)
    tau = _kth_largest_key(key, float(k))
    gt = key > tau
    eq = key == tau
    need = float(k) - jnp.sum(jnp.where(gt, 1.0, 0.0), axis=-1, keepdims=True)
    prefix = _prefix_count(jnp.where(eq, 1.0, 0.0))
    return gt | (eq & (prefix <= need))


def _dsa_prompt_kernel(q_ref, k_ref, v_ref, qi_ref, ki_ref, wi_ref, o_ref, *, tq, t, topk, h_idx, d_idx,
                       n_heads, dh):
    q0 = pl.program_id(1) * tq
    ki = ki_ref[0].astype(BF16)
    sc = jnp.zeros((tq, t), F32)
    for h in range(h_idx):
        s = _dot_nt(qi_ref[0, :, h * d_idx:(h + 1) * d_idx], ki)
        sc = sc + jnp.maximum(s, 0.0) * wi_ref[0, :, h:h + 1]
    qpos = q0 + lax.broadcasted_iota(jnp.int32, (tq, 1), 0)
    spos = lax.broadcasted_iota(jnp.int32, (1, t), 1)
    causal = spos <= qpos
    valid = _topk_mask(jnp.where(causal, sc, -jnp.inf), topk) & causal
    scale = dh ** -0.5
    for h in range(n_heads):
        sl = slice(h * dh, (h + 1) * dh)
        lg = jnp.where(valid, _dot_nt(q_ref[0, :, sl], k_ref[0, :, sl]) * scale, NEG_BIG)
        p = jnp.exp(lg - jnp.max(lg, axis=-1, keepdims=True))
        den = jnp.sum(p, axis=-1, keepdims=True)
        o_ref[0, :, sl] = (_dot(p, v_ref[0, :, sl]) / den).astype(o_ref.dtype)


def _dsa_prompt(q, k, v, qi, ki, wi, dims):
    b, t, cc = q.shape
    tq = _tile(t, 128, SUBLANE)
    topk = min(TOPK_MAX, t // 4)
    h_idx, d_idx = dims["H_IDX"], dims["D_IDX"]
    qb = lambda w: pl.BlockSpec((1, tq, w), lambda i, j: (i, j, 0))
    full = lambda w: pl.BlockSpec((1, t, w), lambda i, j: (i, 0, 0))
    return pl.pallas_call(
        functools.partial(_dsa_prompt_kernel, tq=tq, t=t, topk=topk, h_idx=h_idx, d_idx=d_idx,
                          n_heads=dims["H_C"], dh=dims["DH_C"]),
        grid=(b, t // tq),
        in_specs=[qb(cc), full(cc), full(cc), qb(h_idx * d_idx), full(d_idx), qb(h_idx)],
        out_specs=qb(cc),
        out_shape=jax.ShapeDtypeStruct((b, t, cc), BF16),
        compiler_params=_cparams(2),
        name="dsa_prompt",
    )(q, k, v, qi, ki, wi)


def _idx_rows_scores(qi2, wi2, keys, tq, h_idx):
    s = jnp.maximum(_dot_nt(qi2, keys), 0.0) * wi2
    acc = s[0:tq]
    for h in range(1, h_idx):
        acc = acc + s[h * tq:(h + 1) * tq]
    return acc


def _dsa_sample_scores_kernel(pt_ref, qi_ref, wi_ref, kc_ref, o_ref, *, tq, h_idx):
    o_ref[0] = _idx_rows_scores(qi_ref[0], wi_ref[0], kc_ref[0, 0], tq, h_idx)


def _dsa_sample_scores(page_table, qi2, wi2, cache_kidx, l, tq, h_idx):
    b, n_pages = page_table.shape
    d_idx = cache_kidx.shape[-1]
    rows = qi2.shape[1]
    grid_spec = pltpu.PrefetchScalarGridSpec(
        num_scalar_prefetch=1,
        grid=(b, n_pages),
        in_specs=[pl.BlockSpec((1, rows, d_idx), lambda i, p, pt: (i, 0, 0)),
                  pl.BlockSpec((1, rows, 1), lambda i, p, pt: (i, 0, 0)),
                  pl.BlockSpec((1, 1, PAGE, d_idx), lambda i, p, pt: (l, pt[i, p], 0, 0))],
        out_specs=pl.BlockSpec((1, tq, PAGE), lambda i, p, pt: (i, 0, p)),
    )
    return pl.pallas_call(
        functools.partial(_dsa_sample_scores_kernel, tq=tq, h_idx=h_idx),
        grid_spec=grid_spec,
        out_shape=jax.ShapeDtypeStruct((b, tq, n_pages * PAGE), F32),
        compiler_params=_cparams(2),
        name="dsa_sample_scores",
    )(page_table, qi2, wi2, cache_kidx)


def _dsa_sample_select_kernel(sc_ref, qi_ref, wi_ref, kn_ref, o_ref, *, tq, h_idx, topk, past):
    new = _idx_rows_scores(qi_ref[0], wi_ref[0], kn_ref[0], tq, h_idx)
    qrow = lax.broadcasted_iota(jnp.int32, (tq, PAGE), 0)
    col = lax.broadcasted_iota(jnp.int32, (tq, PAGE), 1)
    new_ok = col <= qrow
    scores = jnp.concatenate([sc_ref[0], jnp.where(new_ok, new, -jnp.inf)], axis=1)
    sel = _topk_mask(scores, topk)
    ok = jnp.concatenate([jnp.ones((tq, past), F32), jnp.where(new_ok, 1.0, 0.0)], axis=1)
    o_ref[0] = jnp.where(sel, ok, 0.0)


def _dsa_sample_select(scores, qi2, wi2, ki_new_pad, tq, h_idx):
    b, _, past = scores.shape
    topk = min(TOPK_MAX, (past + tq) // 4)
    rows, d_idx = qi2.shape[1], qi2.shape[2]
    return pl.pallas_call(
        functools.partial(_dsa_sample_select_kernel, tq=tq, h_idx=h_idx, topk=topk, past=past),
        grid=(b,),
        in_specs=[pl.BlockSpec((1, tq, past), lambda i: (i, 0, 0)),
                  pl.BlockSpec((1, rows, d_idx), lambda i: (i, 0, 0)),
                  pl.BlockSpec((1, rows, 1), lambda i: (i, 0, 0)),
                  pl.BlockSpec((1, PAGE, d_idx), lambda i: (i, 0, 0))],
        out_specs=pl.BlockSpec((1, tq, past + PAGE), lambda i: (i, 0, 0)),
        out_shape=jax.ShapeDtypeStruct((b, tq, past + PAGE), F32),
        compiler_params=_cparams(1),
        name="dsa_sample_select",
    )(scores, qi2, wi2, ki_new_pad)


def _dsa_sample_attn_kernel(pt_ref, q_ref, kc_ref, vc_ref, kn_ref, vn_ref, m_ref, o_ref,
                            q2_s, m_s, l_s, acc_s, *, n_pages, n_heads, dh, tq):
    p = pl.program_id(1)
    rows = n_heads * tq
    cols = PAGE * n_heads

    @pl.when(p == 0)
    def _():
        for h in range(n_heads):
            q2_s[h * tq:(h + 1) * tq, :] = q_ref[0, :, h * dh:(h + 1) * dh]
        m_s[...] = jnp.full(m_s.shape, NEG_BIG, F32)
        l_s[...] = jnp.zeros(l_s.shape, F32)
        acc_s[...] = jnp.zeros(acc_s.shape, F32)

    def update(kmat, vmat):
        lg = _dot_nt(q2_s[...], kmat) * (dh ** -0.5)
        rr = lax.broadcasted_iota(jnp.int32, (rows, cols), 0)
        cc = lax.broadcasted_iota(jnp.int32, (rows, cols), 1)
        msk = jnp.concatenate([m_ref[0]] * n_heads, axis=0) > 0.5
        msk = msk & ((cc % n_heads) == (rr // tq))
        lgm = jnp.where(msk, lg, NEG_BIG)
        m_new = jnp.maximum(m_s[...], jnp.max(lgm, axis=-1, keepdims=True))
        alpha = jnp.exp(m_s[...] - m_new)
        pm = jnp.where(msk, jnp.exp(lgm - m_new), 0.0)
        l_s[...] = alpha * l_s[...] + jnp.sum(pm, axis=-1, keepdims=True)
        acc_s[...] = alpha * acc_s[...] + _dot(pm, vmat)
        m_s[...] = m_new

    @pl.when(p < n_pages)
    def _():
        update(kc_ref[0, 0].reshape(cols, dh), vc_ref[0, 0].reshape(cols, dh))

    @pl.when(p == n_pages)
    def _():
        update(kn_ref[0].reshape(cols, dh), vn_ref[0].reshape(cols, dh))
        res = acc_s[...] / l_s[...]
        for h in range(n_heads):
            o_ref[0, :, h * dh:(h + 1) * dh] = res[h * tq:(h + 1) * tq, :].astype(o_ref.dtype)


def _dsa_sample_attn(page_table, q, cache_k, cache_v, k_new_pad, v_new_pad, mask8, l, dims):
    b, n_pages = page_table.shape
    tq = q.shape[1]
    n_heads, dh = dims["H_C"], dims["DH_C"]
    cc = n_heads * dh
    page = lambda i, p, pt: (l, pt[i, jnp.minimum(p, n_pages - 1)], 0, 0, 0)
    grid_spec = pltpu.PrefetchScalarGridSpec(
        num_scalar_prefetch=1,
        grid=(b, n_pages + 1),
        in_specs=[pl.BlockSpec((1, tq, cc), lambda i, p, pt: (i, 0, 0)),
                  pl.BlockSpec((1, 1, PAGE, n_heads, dh), page),
                  pl.BlockSpec((1, 1, PAGE, n_heads, dh), page),
                  pl.BlockSpec((1, PAGE, n_heads, dh), lambda i, p, pt: (i, 0, 0, 0)),
                  pl.BlockSpec((1, PAGE, n_heads, dh), lambda i, p, pt: (i, 0, 0, 0)),
                  pl.BlockSpec((1, tq, PAGE * n_heads), lambda i, p, pt: (i, 0, p))],
        out_specs=pl.BlockSpec((1, tq, cc), lambda i, p, pt: (i, 0, 0)),
        scratch_shapes=[pltpu.VMEM((n_heads * tq, dh), BF16), pltpu.VMEM((n_heads * tq, 1), F32),
                        pltpu.VMEM((n_heads * tq, 1), F32), pltpu.VMEM((n_heads * tq, dh), F32)],
    )
    return pl.pallas_call(
        functools.partial(_dsa_sample_attn_kernel, n_pages=n_pages, n_heads=n_heads, dh=dh, tq=tq),
        grid_spec=grid_spec,
        out_shape=jax.ShapeDtypeStruct((b, tq, cc), BF16),
        compiler_params=_cparams(2),
        name="dsa_sample_attn",
    )(page_table, q, cache_k, cache_v, k_new_pad, v_new_pad, mask8)


def _dsa_sample(q, k_f32, v_f32, qi, ki, wi, cache_k, cache_v, cache_kidx, page_table, l, dims):
    b, tq, cc = q.shape
    h_idx, d_idx, n_heads, dh = dims["H_IDX"], dims["D_IDX"], dims["H_C"], dims["DH_C"]
    qi2 = jnp.transpose(qi.reshape(b, tq, h_idx, d_idx), (0, 2, 1, 3)).reshape(b, h_idx * tq, d_idx)
    wi2 = jnp.transpose(wi, (0, 2, 1)).reshape(b, h_idx * tq, 1)
    pad_rows = lambda x: jnp.pad(x, ((0, 0), (0, PAGE - tq)) + ((0, 0),) * (x.ndim - 2))
    scores = _dsa_sample_scores(page_table, qi2, wi2, cache_kidx, l, tq, h_idx)
    mask = _dsa_sample_select(scores, qi2, wi2, pad_rows(ki), tq, h_idx)
    mask8 = jnp.repeat(mask, n_heads, axis=-1)
    k_new = pad_rows(k_f32.reshape(b, tq, n_heads, dh))
    v_new = pad_rows(v_f32.reshape(b, tq, n_heads, dh))
    return _dsa_sample_attn(page_table, q, cache_k, cache_v, k_new, v_new, mask8, l, dims)


def _prepare_weights(raw, dims):
    ca, cb, cc, qiw, d_idx, h_idx, d = (dims[k] for k in ("C_A", "C_B", "C_C", "QIW", "D_IDX", "H_IDX", "D"))
    dd, da, dg = dims["D_DECAY"], dims["D_AAA"], dims["D_GATE"]
    assert dd <= LANE and da <= LANE and dg % LANE == 0
    a_cols = 3 * ca + dd + da + dg
    o = [0, a_cols, a_cols + cb, a_cols + cb + cc, a_cols + cb + 2 * cc, a_cols + cb + 3 * cc]
    o += [o[-1] + qiw, o[-1] + qiw + d_idx, o[-1] + qiw + d_idx + h_idx]
    w_in = raw["w_in"]

    def pack_pa(x):
        zeros = lambda n: jnp.zeros(x.shape[:-1] + (n,), x.dtype)
        return jnp.concatenate([x[..., :3 * ca], x[..., 3 * ca:3 * ca + dd], zeros(LANE - dd),
                                x[..., 3 * ca + dd:3 * ca + dd + da], zeros(LANE - da),
                                x[..., 3 * ca + dd + da:a_cols]], axis=-1)

    kiw = LANE - d_idx - h_idx
    assert kiw >= 0
    w_pa = pack_pa(w_in[..., :a_cols]).astype(BF16)
    w_mid = jnp.concatenate([w_in[..., o[1]:o[2]], w_in[..., o[4]:o[5]], w_in[..., o[2]:o[4]],
                             w_in[..., o[5]:o[8]], jnp.zeros(w_in.shape[:-1] + (kiw,), w_in.dtype)],
                            axis=-1).astype(BF16)
    w_gl = w_in[..., o[8]:].astype(BF16)
    pad_rows = lambda x: jnp.pad(x, ((0, 0), (0, LANE - x.shape[1]), (0, 0)))
    depth = w_in.shape[0]
    vec = lambda x: x.reshape(depth, 1, -1)
    P = dict(
        w_pa=w_pa, w_mid=w_mid, w_gl=w_gl,
        mu=vec(pack_pa(raw["rwkv_mu"])), w0=vec(raw["rwkv_w0"]), w2=pad_rows(raw["rwkv_w2"]).astype(BF16),
        a0=vec(raw["rwkv_a0"]), a2=pad_rows(raw["rwkv_a2"]).astype(BF16), g2=raw["rwkv_g2"].astype(BF16),
        k_k=vec(raw["rwkv_k_k"]), k_a=vec(raw["rwkv_k_a"]), r_k=vec(raw["rwkv_r_k"]),
        ln_w=vec(raw["rwkv_ln_w"]), ln_b=vec(raw["rwkv_ln_b"]),
        pool_w=raw["pool_w"].astype(BF16), pool_scale=raw["pool_scale"],
    )
    for name in ("ffn1_w_gate", "ffn1_w_up", "ffn1_w_down", "ffn2_w_gate", "ffn2_w_up", "ffn2_w_down",
                 "w_br_a", "w_br_b", "w_br_c", "w_out"):
        P[name] = raw[name].astype(BF16)
    for name in ("ffn1_norm", "mix_norm", "ffn2_norm"):
        P[name] = raw[name]
    P["pack_pa"] = pack_pa
    return P


def _unpack_pa(x, dims):
    ca, dd, da = dims["C_A"], dims["D_DECAY"], dims["D_AAA"]
    return jnp.concatenate([x[..., :3 * ca], x[..., 3 * ca:3 * ca + dd],
                            x[..., 3 * ca + LANE:3 * ca + LANE + da], x[..., 3 * ca + 2 * LANE:]], axis=-1)


def _layer(x, P, l, pos0, shift_prev, s0, pool_hist, dims, sample_ctx):
    b, t, d = x.shape
    m = b * t
    ca, cb, cc = dims["C_A"], dims["C_B"], dims["C_C"]
    h = _ffn(x.reshape(m, d), P["ffn1_norm"][l], P["ffn1_w_gate"][l], P["ffn1_w_up"][l], P["ffn1_w_down"][l])
    u = _rmsnorm(h, P["mix_norm"][l], BF16)
    pa = _mm(u, P["w_pa"][l], out_dtype=F32, name="proj_pa").reshape(b, t, -1)
    mid = _mm(u, P["w_mid"][l], out_dtype=F32, name="proj_mid").reshape(b, t, -1)
    gates = _mm(u, P["w_gl"][l], out_dtype=BF16, act="sigmoid", name="proj_gate")

    chunk = min(N_A, t)
    rt, at, bt, kt, v_a, gam, bonus, g = _rwkv_prep(pa, P["pack_pa"](shift_prev), P, l, dims, chunk)
    y_a, new_wkv = _rwkv_scan(rt, at, bt, kt, v_a, gam, s0, chunk)
    ya = _rwkv_post(y_a.reshape(m, ca), bonus.reshape(m, ca), g.reshape(m, ca), P["ln_w"][l], P["ln_b"][l])
    new_shift = _unpack_pa(pa[:, -1], dims)

    hist16 = jnp.pad(pool_hist, ((0, 0), (2 * SUBLANE - POOL_HIST, 0), (0, 0)))
    yb = _pool(mid, hist16, P["pool_w"][l], P["pool_scale"][l], pos0, dims)
    zb = mid[..., :cb]
    new_pool = jnp.concatenate([pool_hist, zb], axis=1)[:, -POOL_HIST:]

    q, k, k_bf, v_bf, qi, kiwi = _rope_all(mid, pos0, dims)
    v = mid[..., cb:cb + cc]
    ki = kiwi[..., :dims["D_IDX"]]
    wi = kiwi[..., dims["D_IDX"]:dims["D_IDX"] + dims["H_IDX"]]
    if sample_ctx is None:
        yc = _dsa_prompt(q, k_bf, v_bf, qi, ki, wi, dims)
    else:
        cache_k, cache_v, cache_kidx, page_table = sample_ctx
        yc = _dsa_sample(q, k, v, qi, ki, wi, cache_k, cache_v, cache_kidx, page_table, l, dims)

    merged = _merge(ya, yb.reshape(m, cb), yc.reshape(m, cc), P["w_br_a"][l], P["w_br_b"][l], P["w_br_c"][l], gates)
    h = _mm(merged, P["w_out"][l], out_dtype=F32, res=h, name="w_out")
    h = _ffn(h, P["ffn2_norm"][l], P["ffn2_w_gate"][l], P["ffn2_w_up"][l], P["ffn2_w_down"][l])
    n_heads, dh = dims["H_C"], dims["DH_C"]
    state = (k.reshape(b, t, n_heads, dh), v.reshape(b, t, n_heads, dh), ki, new_wkv, new_shift, new_pool)
    return h.reshape(b, t, d), state


def kernel(x_prompt, x_sample, cache_k, cache_v, cache_kidx, state_wkv, state_shift, state_pool, page_table,
           ffn1_norm, ffn1_w_gate, ffn1_w_up, ffn1_w_down, mix_norm, w_in, rwkv_mu, rwkv_w0, rwkv_w2,
           rwkv_a0, rwkv_a2, rwkv_g2, rwkv_k_k, rwkv_k_a, rwkv_r_k, rwkv_ln_w, rwkv_ln_b, w_br_a, pool_w,
           pool_scale, w_br_b, w_br_c, w_out, ffn2_norm, ffn2_w_gate, ffn2_w_up, ffn2_w_down, final_norm):
    raw = dict(ffn1_norm=ffn1_norm, ffn1_w_gate=ffn1_w_gate, ffn1_w_up=ffn1_w_up, ffn1_w_down=ffn1_w_down,
               mix_norm=mix_norm, w_in=w_in, rwkv_mu=rwkv_mu, rwkv_w0=rwkv_w0, rwkv_w2=rwkv_w2, rwkv_a0=rwkv_a0,
               rwkv_a2=rwkv_a2, rwkv_g2=rwkv_g2, rwkv_k_k=rwkv_k_k, rwkv_k_a=rwkv_k_a, rwkv_r_k=rwkv_r_k,
               rwkv_ln_w=rwkv_ln_w, rwkv_ln_b=rwkv_ln_b, w_br_a=w_br_a, pool_w=pool_w, pool_scale=pool_scale,
               w_br_b=w_br_b, w_br_c=w_br_c, w_out=w_out, ffn2_norm=ffn2_norm, ffn2_w_gate=ffn2_w_gate,
               ffn2_w_up=ffn2_w_up, ffn2_w_down=ffn2_w_down)
    depth, d = mix_norm.shape
    ca = rwkv_w0.shape[-1]
    cb = pool_scale.shape[-1]
    n_heads, dh = cache_k.shape[3], cache_k.shape[4]
    cc = n_heads * dh
    d_idx = cache_kidx.shape[-1]
    dd, da, dg = rwkv_w2.shape[1], rwkv_a2.shape[1], rwkv_g2.shape[1]
    a_cols = 3 * ca + dd + da + dg
    h_idx = (w_in.shape[-1] - a_cols - cb - 3 * cc - d_idx - 3 * d) // (d_idx + 1)
    dims = dict(D=d, C_A=ca, C_B=cb, C_C=cc, H_C=n_heads, DH_C=dh, D_IDX=d_idx, H_IDX=h_idx,
                QIW=h_idx * d_idx, D_DECAY=dd, D_AAA=da, D_GATE=dg)
    assert w_in.shape[-1] == a_cols + cb + 3 * cc + h_idx * d_idx + d_idx + h_idx + 3 * d
    P = _prepare_weights(raw, dims)

    bp = x_prompt.shape[0]
    past = page_table.shape[1] * PAGE
    yp, ys = x_prompt, x_sample
    st_p, st_s = [], []
    for l in range(depth):
        zero_shift = jnp.zeros((bp, a_cols), F32)
        zero_wkv = jnp.zeros((bp, ca // N_A, N_A, N_A), F32)
        zero_pool = jnp.zeros((bp, POOL_HIST, cb), F32)
        yp, sp = _layer(yp, P, l, 0, zero_shift, zero_wkv, zero_pool, dims, None)
        ys, ss = _layer(ys, P, l, past, state_shift[l], state_wkv[l], state_pool[l], dims,
                        (cache_k, cache_v, cache_kidx, page_table))
        st_p.append(sp)
        st_s.append(ss)
    fin = lambda y: _rmsnorm(y.reshape(-1, d), final_norm, F32).reshape(y.shape)
    stk = lambda sts, i: jnp.stack([s[i] for s in sts])
    return (fin(yp), fin(ys),
            stk(st_p, 0), stk(st_p, 1), stk(st_p, 2), stk(st_p, 3), stk(st_p, 4), stk(st_p, 5),
            stk(st_s, 0), stk(st_s, 1), stk(st_s, 2), stk(st_s, 3), stk(st_s, 4), stk(st_s, 5))
```

```python
import functools
import math

import jax
import jax.numpy as jnp
from jax import lax
from jax.experimental import pallas as pl
from jax.experimental.pallas import tpu as pltpu

F32 = jnp.float32
BF16 = jnp.bfloat16

LANE = 128
SUBLANE = 8
VMEM_LIMIT_BYTES = 56 * 1024 * 1024

N_A = 64
GN_EPS = 64e-5
POOL_WINDOWS = (2, 4, 8, 16)
POOL_HIST = max(POOL_WINDOWS) - 1
TOPK_MAX = 256
ROPE_THETA = 10000.0
RMS_EPS = 1e-6
PAGE = 128
NEG_BIG = -1e30


def _cparams(n_axes):
    return pltpu.CompilerParams(dimension_semantics=("arbitrary",) * n_axes,
                                vmem_limit_bytes=VMEM_LIMIT_BYTES)


def _tile(n, target, mult):
    best = None
    for t in range(mult, min(n, target) + 1, mult):
        if n % t == 0:
            best = t
    return best if best is not None else n


def _roundup(n, m):
    return -(-n // m) * m


def _dot(a, b):
    return jnp.dot(a.astype(BF16), b.astype(BF16), preferred_element_type=F32)


def _dot_nt(a, b):
    return lax.dot_general(a.astype(BF16), b.astype(BF16), (((1,), (1,)), ((), ())),
                           preferred_element_type=F32)


def _split3(x):
    h1 = x.astype(BF16)
    r1 = x - h1.astype(F32)
    h2 = r1.astype(BF16)
    h3 = (r1 - h2.astype(F32)).astype(BF16)
    return h1, h2, h3


def _dot_exact_rhs(x, m_bf16):
    h1, h2, h3 = _split3(x)
    return (jnp.dot(h1, m_bf16, preferred_element_type=F32)
            + jnp.dot(h2, m_bf16, preferred_element_type=F32)
            + jnp.dot(h3, m_bf16, preferred_element_type=F32))


def _dot_exact_lhs(m_bf16, x):
    h1, h2, h3 = _split3(x)
    return (jnp.dot(m_bf16, h1, preferred_element_type=F32)
            + jnp.dot(m_bf16, h2, preferred_element_type=F32)
            + jnp.dot(m_bf16, h3, preferred_element_type=F32))


def _rmsnorm_kernel(x_ref, g_ref, o_ref):
    x = x_ref[...]
    ms = jnp.mean(x * x, axis=-1, keepdims=True)
    o_ref[...] = (x * lax.rsqrt(ms + RMS_EPS) * g_ref[...]).astype(o_ref.dtype)


def _rmsnorm(x2d, g, out_dtype):
    m, d = x2d.shape
    tm = _tile(m, 256, SUBLANE)
    return pl.pallas_call(
        _rmsnorm_kernel,
        grid=(m // tm,),
        in_specs=[pl.BlockSpec((tm, d), lambda i: (i, 0)), pl.BlockSpec((1, d), lambda i: (0, 0))],
        out_specs=pl.BlockSpec((tm, d), lambda i: (i, 0)),
        out_shape=jax.ShapeDtypeStruct((m, d), out_dtype),
        compiler_params=_cparams(1),
        name="rmsnorm",
    )(x2d, g.reshape(1, d))


def _mm_kernel(*refs, nk, has_res, res_scale, act):
    x_ref, w_ref = refs[0], refs[1]
    res_ref = refs[2] if has_res else None
    o_ref = refs[2 + int(has_res)]

    def epilogue(acc):
        if act == "sigmoid":
            acc = jax.nn.sigmoid(acc)
        if has_res:
            acc = res_ref[...] + res_scale * acc
        o_ref[...] = acc.astype(o_ref.dtype)

    part = jnp.dot(x_ref[...], w_ref[...], preferred_element_type=F32)
    if nk == 1:
        epilogue(part)
    else:
        acc_ref = refs[3 + int(has_res)]
        k = pl.program_id(2)

        @pl.when(k == 0)
        def _():
            acc_ref[...] = part

        @pl.when(k > 0)
        def _():
            acc_ref[...] += part

        @pl.when(k == nk - 1)
        def _():
            epilogue(acc_ref[...])


def _mm(x, w, *, out_dtype, res=None, res_scale=1.0, act=None, tm_target=1024, tn_target=1024,
        tk_target=4096, name="mm"):
    m, kdim = x.shape
    n = w.shape[1]
    tm = _tile(m, tm_target, SUBLANE)
    tn = _tile(n, tn_target, LANE)
    tk = _tile(kdim, tk_target, LANE)
    nk = kdim // tk
    in_specs = [pl.BlockSpec((tm, tk), lambda i, j, k: (i, k)),
                pl.BlockSpec((tk, tn), lambda i, j, k: (k, j))]
    args = [x, w]
    if res is not None:
        in_specs.append(pl.BlockSpec((tm, tn), lambda i, j, k: (i, j)))
        args.append(res)
    scratch = [pltpu.VMEM((tm, tn), F32)] if nk > 1 else []
    return pl.pallas_call(
        functools.partial(_mm_kernel, nk=nk, has_res=res is not None, res_scale=res_scale, act=act),
        grid=(m // tm, n // tn, nk),
        in_specs=in_specs,
        out_specs=pl.BlockSpec((tm, tn), lambda i, j, k: (i, j)),
        out_shape=jax.ShapeDtypeStruct((m, n), out_dtype),
        scratch_shapes=scratch,
        compiler_params=_cparams(3),
        name=name,
    )(*args)


def _swiglu_up_kernel(x_ref, wg_ref, wu_ref, o_ref):
    x = x_ref[...]
    g = jnp.dot(x, wg_ref[...], preferred_element_type=F32)
    u = jnp.dot(x, wu_ref[...], preferred_element_type=F32)
    o_ref[...] = (g * jax.nn.sigmoid(g) * u).astype(o_ref.dtype)


def _swiglu_up(x, wg, wu):
    m, d = x.shape
    f = wg.shape[1]
    tm = _tile(m, 1024, SUBLANE)
    tn = _tile(f, 512, LANE)
    return pl.pallas_call(
        _swiglu_up_kernel,
        grid=(m // tm, f // tn),
        in_specs=[pl.BlockSpec((tm, d), lambda i, j: (i, 0)),
                  pl.BlockSpec((d, tn), lambda i, j: (0, j)),
                  pl.BlockSpec((d, tn), lambda i, j: (0, j))],
        out_specs=pl.BlockSpec((tm, tn), lambda i, j: (i, j)),
        out_shape=jax.ShapeDtypeStruct((m, f), BF16),
        compiler_params=_cparams(2),
        name="swiglu_up",
    )(x, wg, wu)


def _ffn(x2d, norm_g, wg, wu, wd):
    xn = _rmsnorm(x2d, norm_g, BF16)
    a = _swiglu_up(xn, wg, wu)
    tk = _tile(a.shape[1], 5632, LANE)
    return _mm(a, wd, out_dtype=F32, res=x2d, res_scale=0.5, tm_target=1024, tn_target=512,
               tk_target=tk, name="ffn_down")


def _merge_kernel(ya_ref, yb_ref, yc_ref, wa_ref, wb_ref, wc_ref, ga_ref, gb_ref, gc_ref, o_ref):
    a = jnp.dot(ya_ref[...], wa_ref[...], preferred_element_type=F32)
    b = jnp.dot(yb_ref[...], wb_ref[...], preferred_element_type=F32)
    c = jnp.dot(yc_ref[...], wc_ref[...], preferred_element_type=F32)
    out = ga_ref[...].astype(F32) * a + gb_ref[...].astype(F32) * b + gc_ref[...].astype(F32) * c
    o_ref[...] = out.astype(o_ref.dtype)


def _merge(ya, yb, yc, wa, wb, wc, gates):
    m = ya.shape[0]
    d = wa.shape[1]
    tm = _tile(m, 1024, SUBLANE)
    tn = _tile(d, 512, LANE)
    nb = d // tn
    return pl.pallas_call(
        _merge_kernel,
        grid=(m // tm, nb),
        in_specs=[pl.BlockSpec((tm, ya.shape[1]), lambda i, j: (i, 0)),
                  pl.BlockSpec((tm, yb.shape[1]), lambda i, j: (i, 0)),
                  pl.BlockSpec((tm, yc.shape[1]), lambda i, j: (i, 0)),
                  pl.BlockSpec((wa.shape[0], tn), lambda i, j: (0, j)),
                  pl.BlockSpec((wb.shape[0], tn), lambda i, j: (0, j)),
                  pl.BlockSpec((wc.shape[0], tn), lambda i, j: (0, j)),
                  pl.BlockSpec((tm, tn), lambda i, j: (i, j)),
                  pl.BlockSpec((tm, tn), lambda i, j: (i, nb + j)),
                  pl.BlockSpec((tm, tn), lambda i, j: (i, 2 * nb + j))],
        out_specs=pl.BlockSpec((tm, tn), lambda i, j: (i, j)),
        out_shape=jax.ShapeDtypeStruct((m, d), BF16),
        compiler_params=_cparams(2),
        name="merge",
    )(ya, yb, yc, wa, wb, wc, gates, gates, gates)


def _rope_kernel(q_ref, k_ref, v_ref, qi_ref, kw_ref, c128_ref, s128_ref, c64_ref, s64_ref,
                 qo_ref, ko_ref, vo_ref, kb_ref, vb_ref, qio_ref, kwo_ref, *, n_heads, n_idx_groups, d_idx,
                 idx_scale):
    c128, s128 = c128_ref[...], s128_ref[...]
    c64, s64 = c64_ref[...], s64_ref[...]
    lane = lax.broadcasted_iota(jnp.int32, c64.shape, 1)
    first_half = (lane % d_idx) < (d_idx // 2)

    def rope128(x):
        return x * c128 + pltpu.roll(x, LANE // 2, axis=1) * s128

    def rope64(x):
        rot = jnp.where(first_half, pltpu.roll(x, LANE - d_idx // 2, axis=1), pltpu.roll(x, d_idx // 2, axis=1))
        return x * c64 + rot * s64

    for h in range(n_heads):
        sl = slice(h * LANE, (h + 1) * LANE)
        qo_ref[0, :, sl] = rope128(q_ref[0, :, sl]).astype(qo_ref.dtype)
        kr = rope128(k_ref[0, :, sl])
        ko_ref[0, :, h, :] = kr
        kb_ref[0, :, sl] = kr.astype(kb_ref.dtype)
        vo_ref[0, :, h, :] = v_ref[0, :, sl]
    vb_ref[0] = v_ref[0].astype(vb_ref.dtype)
    for g in range(n_idx_groups):
        sl = slice(g * LANE, (g + 1) * LANE)
        qio_ref[0, :, sl] = rope64(qi_ref[0, :, sl])
    kw = kw_ref[0]
    kwo_ref[0] = jnp.where(lane < d_idx, rope64(kw), kw * idx_scale)


def _rope_tables(pos, d):
    inv = ROPE_THETA ** (-jnp.arange(0, d, 2, dtype=F32) / d)
    ang = pos.astype(F32)[:, None] * inv[None, :]
    cos, sin = jnp.cos(ang), jnp.sin(ang)
    reps = LANE // d
    c = jnp.tile(jnp.concatenate([cos, cos], axis=-1), (1, reps))
    s = jnp.tile(jnp.concatenate([-sin, sin], axis=-1), (1, reps))
    return c, s


def _rope_all(mid, pos0, dims):
    b, t, _ = mid.shape
    cb, cc, qiw, d_idx, h_idx = dims["C_B"], dims["C_C"], dims["QIW"], dims["D_IDX"], dims["H_IDX"]
    assert dims["DH_C"] == LANE and LANE % d_idx == 0
    assert cb % cc == 0 and (cb + 3 * cc) % qiw == 0
    tm = _tile(t, 256, SUBLANE)
    pos = pos0 + jnp.arange(t)
    c128, s128 = _rope_tables(pos, LANE)
    c64, s64 = _rope_tables(pos, d_idx)
    ob = cb // cc
    row = lambda width, idx: pl.BlockSpec((1, tm, width), lambda i, j: (i, j, idx))
    tab = pl.BlockSpec((tm, LANE), lambda i, j: (j, 0))
    out = lambda width: pl.BlockSpec((1, tm, width), lambda i, j: (i, j, 0))
    split = pl.BlockSpec((1, tm, cc // LANE, LANE), lambda i, j: (i, j, 0, 0))
    return pl.pallas_call(
        functools.partial(_rope_kernel, n_heads=cc // LANE, n_idx_groups=qiw // LANE, d_idx=d_idx,
                          idx_scale=(h_idx * d_idx) ** -0.5),
        grid=(b, t // tm),
        in_specs=[row(cc, ob + 1), row(cc, ob + 2), row(cc, ob), row(qiw, (cb + 3 * cc) // qiw),
                  row(LANE, (cb + 3 * cc + qiw) // LANE), tab, tab, tab, tab],
        out_specs=[out(cc), split, split, out(cc), out(cc), out(qiw), out(LANE)],
        out_shape=[jax.ShapeDtypeStruct((b, t, cc), BF16), jax.ShapeDtypeStruct((b, t, cc // LANE, LANE), F32),
                   jax.ShapeDtypeStruct((b, t, cc // LANE, LANE), F32),
                   jax.ShapeDtypeStruct((b, t, cc), BF16), jax.ShapeDtypeStruct((b, t, cc), BF16),
                   jax.ShapeDtypeStruct((b, t, qiw), F32), jax.ShapeDtypeStruct((b, t, LANE), F32)],
        compiler_params=_cparams(2),
        name="rope",
    )(mid, mid, mid, mid, mid, c128, s128, c64, s64)


def _head_sum(x, jmat):
    parts = []
    for g in range(x.shape[1] // LANE):
        parts.append(_dot_exact_rhs(x[:, g * LANE:(g + 1) * LANE], jmat))
    return parts[0] if len(parts) == 1 else jnp.concatenate(parts, axis=1)


def _head_ones():
    r = lax.broadcasted_iota(jnp.int32, (LANE, LANE), 0)
    c = lax.broadcasted_iota(jnp.int32, (LANE, LANE), 1)
    return jnp.where((r // N_A) == (c // N_A), 1.0, 0.0).astype(BF16)


def _rwkv_prep_kernel(pa_ref, halo_ref, shift_ref, mu_ref, w0_ref, w2_ref, a0_ref, a2_ref, g2_ref,
                      kk_ref, ka_ref, rk_ref,
                      rt_ref, at_ref, bt_ref, kt_ref, v_ref, bkt_ref, gct_ref, bonus_ref, g_ref, *, ca, chunk, tm):
    j = pl.program_id(1)
    prev_last = jnp.where(j == 0, shift_ref[0], halo_ref[0, SUBLANE - 1:SUBLANE, :])
    row = lax.broadcasted_iota(jnp.int32, (tm, 1), 0)

    def shifted(lo, hi):
        x = pa_ref[0, :, lo:hi]
        prev = jnp.where(row == 0, prev_last[:, lo:hi], pltpu.roll(x, 1, axis=0))
        return x + mu_ref[:, lo:hi] * (prev - x)

    r = shifted(0, ca)
    k = shifted(ca, 2 * ca)
    v = shifted(2 * ca, 3 * ca)
    wl = shifted(3 * ca, 3 * ca + LANE)
    al = shifted(3 * ca + LANE, 3 * ca + 2 * LANE)
    gl = shifted(3 * ca + 2 * LANE, pa_ref.shape[2])

    z = -(w0_ref[...] + _dot(jnp.tanh(wl), w2_ref[...]))
    softplus = jnp.maximum(z, 0.0) + jnp.log(1.0 + jnp.exp(-jnp.abs(z)))
    log_decay = -jnp.exp(-softplus - 0.5)
    rr = lax.broadcasted_iota(jnp.int32, (tm, tm), 0)
    cc = lax.broadcasted_iota(jnp.int32, (tm, tm), 1)
    same_chunk = (rr // chunk) == (cc // chunk)
    tril = jnp.where(same_chunk & (cc <= rr), 1.0, 0.0).astype(BF16)
    ones_blk = jnp.where(same_chunk, 1.0, 0.0).astype(BF16)
    cum = _dot_exact_lhs(tril, log_decay)
    cum_c = _dot_exact_lhs(ones_blk, log_decay)
    fwd = jnp.exp(cum_c - cum)
    back = jnp.exp(cum - cum_c)
    back_prev = jnp.exp(cum - log_decay - cum_c)
    gam_c = jnp.exp(cum_c)

    a = jax.nn.sigmoid(a0_ref[...] + _dot(al, a2_ref[...]))
    jmat = _head_ones()
    kk = k * kk_ref[...]
    kk = kk / jnp.maximum(jnp.sqrt(_head_sum(kk * kk, jmat)), 1e-12)
    kh = k * (1.0 + (a - 1.0) * ka_ref[...])
    bonus_ref[0] = _head_sum(r * kh * rk_ref[...], jmat) * v
    g_ref[0] = _dot(jax.nn.sigmoid(gl), g2_ref[...])
    bh = kk * a * fwd
    kf = kh * fwd
    rt_ref[0] = (r * back).astype(rt_ref.dtype)
    at_ref[0] = (-kk * back_prev).astype(at_ref.dtype)
    bt_ref[0] = bh.astype(bt_ref.dtype)
    kt_ref[0] = kf.astype(kt_ref.dtype)
    v_ref[0] = v.astype(v_ref.dtype)
    pad = N_A - chunk
    for c in range(tm // chunk):
        rows = slice(c * chunk, (c + 1) * chunk)
        if pad:
            zeros = jnp.zeros((pad, ca), F32)
            stacked = jnp.concatenate([bh[rows], zeros, kf[rows], zeros], axis=0)
        else:
            stacked = jnp.concatenate([bh[rows], kf[rows]], axis=0)
        bkt_ref[0, c] = stacked.T.astype(bkt_ref.dtype)
        gct_ref[0, c] = jnp.broadcast_to(gam_c[c * chunk:c * chunk + 1], (LANE, ca)).T


def _rwkv_prep(pa, shift_p, P, l, dims, chunk):
    b, t, paw = pa.shape
    ca = dims["C_A"]
    tm = chunk * max(1, min(128, t) // chunk)
    assert t % tm == 0 and 2 * N_A == LANE
    hb = tm // SUBLANE
    nct = tm // chunk
    row = pl.BlockSpec((1, tm, ca), lambda i, j: (i, j, 0))
    cmaj = pl.BlockSpec((1, nct, ca, LANE), lambda i, j: (i, j, 0, 0))
    vec = lambda w: pl.BlockSpec((1, w), lambda i, j: (0, 0))
    mat = lambda r: pl.BlockSpec((r, ca), lambda i, j: (0, 0))
    tmaj = lambda dt: jax.ShapeDtypeStruct((b, t, ca), dt)
    return pl.pallas_call(
        functools.partial(_rwkv_prep_kernel, ca=ca, chunk=chunk, tm=tm),
        grid=(b, t // tm),
        in_specs=[pl.BlockSpec((1, tm, paw), lambda i, j: (i, j, 0)),
                  pl.BlockSpec((1, SUBLANE, paw), lambda i, j: (i, jnp.maximum(j * hb - 1, 0), 0)),
                  pl.BlockSpec((1, 1, paw), lambda i, j: (i, 0, 0)),
                  vec(paw), vec(ca), mat(LANE), vec(ca), mat(LANE), mat(P["g2"].shape[1]),
                  vec(ca), vec(ca), vec(ca)],
        out_specs=[row] * 5 + [cmaj, cmaj, row, row],
        out_shape=[tmaj(BF16)] * 5 + [jax.ShapeDtypeStruct((b, t // chunk, ca, LANE), BF16),
                                      jax.ShapeDtypeStruct((b, t // chunk, ca, LANE), F32), tmaj(F32), tmaj(F32)],
        compiler_params=_cparams(2),
        name="rwkv_prep",
    )(pa, pa, shift_p.reshape(b, 1, paw), P["mu"][l], P["w0"][l], P["w2"][l], P["a0"][l], P["a2"][l],
      P["g2"][l], P["k_k"][l], P["k_a"][l], P["r_k"][l])


def _rwkv_scan_kernel(at_ref, rt_ref, v_ref, bt_ref, kt_ref, bkt_ref, gct_ref, s0_ref, y_ref, s_ref, *,
                      n_pairs, chunk, unroll):
    c = pl.program_id(1)

    @pl.when(c == 0)
    def _():
        s_ref[...] = s0_ref[...]

    rr = lax.broadcasted_iota(jnp.int32, (chunk, chunk), 0)
    cc = lax.broadcasted_iota(jnp.int32, (chunk, chunk), 1)
    lower_strict = cc < rr
    lower_incl = cc <= rr
    eye = jnp.where(rr == cc, 1.0, 0.0)
    first = lax.broadcasted_iota(jnp.int32, (chunk, LANE), 1) < N_A
    br = lax.broadcasted_iota(jnp.int32, (LANE, LANE), 0)
    bc = lax.broadcasted_iota(jnp.int32, (LANE, LANE), 1)
    block_diag = (br // N_A) == (bc // N_A)
    n_double = max(int(math.log2(chunk)) - 1, 0)
    pad = N_A - chunk

    def load(p):
        sl = pl.ds(pl.multiple_of(p * LANE, LANE), LANE)
        return (sl,
                at_ref[0, :, sl],
                rt_ref[0, :, sl],
                v_ref[0, :, sl],
                bt_ref[0, :, sl],
                kt_ref[0, :, sl],
                bkt_ref[0, 0, sl, :],
                s_ref[0, sl, :] * gct_ref[0, 0, sl, :])

    def group(g, carry):
        loaded = [load(g * unroll + j) for j in range(unroll)]
        sls, a, r, v, b, k, bk_t, sb = (list(col) for col in zip(*loaded))
        pairs = range(unroll)
        heads = [(i, hh) for i in pairs for hh in range(2)]
        own = lambda hh: first if hh == 0 else jnp.logical_not(first)
        sb16 = [sb[i].astype(BF16) for i in pairs]
        x = [jnp.dot(a[i], sb16[i], preferred_element_type=F32) for i in pairs]
        y = [jnp.dot(r[i], sb16[i], preferred_element_type=F32) for i in pairs]
        a_h = [jnp.where(own(hh), a[i], jnp.zeros_like(a[i])) for i, hh in heads]
        r_h = [jnp.where(own(hh), r[i], jnp.zeros_like(r[i])) for i, hh in heads]
        n_ab = [jnp.where(lower_strict, _dot_nt(a_h[j], b[i]), 0.0) for j, (i, _) in enumerate(heads)]
        n_ak = [jnp.where(lower_strict, _dot_nt(a_h[j], k[i]), 0.0) for j, (i, _) in enumerate(heads)]
        m_rb = [jnp.where(lower_incl, _dot_nt(r_h[j], b[i]), 0.0) for j, (i, _) in enumerate(heads)]
        m_rk = [jnp.where(lower_incl, _dot_nt(r_h[j], k[i]), 0.0) for j, (i, _) in enumerate(heads)]
        inv = [eye + n for n in n_ab]
        npow = n_ab
        for _ in range(n_double):
            npow = [_dot(n, n) for n in npow]
            inv = [iv + _dot(iv, n) for iv, n in zip(inv, npow)]
        w = [x[i] + _dot(n_ak[j], v[i]) for j, (i, _) in enumerate(heads)]
        u = [_dot(inv[j], w[j]) for j in range(len(heads))]
        y_h = [_dot(m_rb[j], u[j]) + _dot(m_rk[j], v[i]) for j, (i, _) in enumerate(heads)]
        for i in pairs:
            u_p = jnp.where(first, u[2 * i], u[2 * i + 1])
            y_new = y[i] + jnp.where(first, y_h[2 * i], y_h[2 * i + 1])
            v32 = v[i].astype(F32)
            if pad:
                zeros = jnp.zeros((pad, LANE), F32)
                stacked = jnp.concatenate([u_p, zeros, v32, zeros], axis=0)
            else:
                stacked = jnp.concatenate([u_p, v32], axis=0)
            s_new = sb[i] + jnp.where(block_diag, _dot(bk_t[i], stacked), 0.0)
            y_ref[0, :, sls[i]] = y_new
            s_ref[0, sls[i], :] = s_new
        return carry

    lax.fori_loop(0, n_pairs // unroll, group, 0)


def _rwkv_scan(rt, at, bt, kt, v, bkt, gct, s0, chunk):
    b, t, ca = rt.shape
    h = ca // N_A
    nc = t // chunk
    eye2 = jnp.eye(2, dtype=F32)
    s0t = jnp.swapaxes(s0, -1, -2).reshape(b, h // 2, 2, N_A, 1, N_A)
    sb0 = (s0t * eye2[None, None, :, None, :, None]).reshape(b, ca, LANE)
    tmaj = pl.BlockSpec((1, chunk, ca), lambda i, j: (i, j, 0))
    cmaj = pl.BlockSpec((1, 1, ca, LANE), lambda i, j: (i, j, 0, 0))
    st = pl.BlockSpec((1, ca, LANE), lambda i, j: (i, 0, 0))
    y, sb = pl.pallas_call(
        functools.partial(_rwkv_scan_kernel, n_pairs=h // 2, chunk=chunk, unroll=_tile(h // 2, 8, 1)),
        grid=(b, nc),
        in_specs=[tmaj, tmaj, tmaj, tmaj, tmaj, cmaj, cmaj, st],
        out_specs=[tmaj, st],
        out_shape=[jax.ShapeDtypeStruct((b, t, ca), F32), jax.ShapeDtypeStruct((b, ca, LANE), F32)],
        compiler_params=_cparams(2),
        name="rwkv_scan",
    )(at, rt, v, bt, kt, bkt, gct, sb0)
    sb = sb.reshape(b, h // 2, 2, N_A, 2, N_A)
    s_t = jnp.stack([sb[:, :, 0, :, 0, :], sb[:, :, 1, :, 1, :]], axis=2).reshape(b, h, N_A, N_A)
    return y, jnp.swapaxes(s_t, -1, -2)


def _rwkv_post_kernel(y_ref, bonus_ref, g_ref, lnw_ref, lnb_ref, o_ref):
    jmat = _head_ones()
    y = y_ref[...]
    mean = _head_sum(y, jmat) * (1.0 / N_A)
    d = y - mean
    var = _head_sum(d * d, jmat) * (1.0 / N_A)
    out = (d * lax.rsqrt(var + GN_EPS) * lnw_ref[...] + lnb_ref[...] + bonus_ref[...]) * g_ref[...]
    o_ref[...] = out.astype(o_ref.dtype)


def _rwkv_post(y, bonus, g, ln_w, ln_b):
    m, ca = y.shape
    tm = _tile(m, 256, SUBLANE)
    row = pl.BlockSpec((tm, ca), lambda i: (i, 0))
    vec = pl.BlockSpec((1, ca), lambda i: (0, 0))
    return pl.pallas_call(
        _rwkv_post_kernel,
        grid=(m // tm,),
        in_specs=[row, row, row, vec, vec],
        out_specs=row,
        out_shape=jax.ShapeDtypeStruct((m, ca), BF16),
        compiler_params=_cparams(1),
        name="rwkv_post",
    )(y, bonus, g, ln_w, ln_b)


def _pool_kernel(z_ref, hist_ref, w_ref, scale_ref, o_ref, x_scr, *, t, tm, group, pos0):
    halo = 2 * SUBLANE
    x_scr[0:halo, :] = hist_ref[0]
    x_scr[halo:halo + t, :] = z_ref[0]
    for i in range(t // tm):
        r0 = i * tm
        pos = pos0 + r0 + lax.broadcasted_iota(jnp.int32, (tm, 1), 0)
        for gi, win in enumerate(POOL_WINDOWS):
            lo, hi = gi * group, (gi + 1) * group
            cur = x_scr[halo + r0:halo + r0 + tm, lo:hi]
            tot = cur
            for back in range(1, win):
                tot = tot + x_scr[halo + r0 - back:halo + r0 - back + tm, lo:hi]
            cnt = jnp.minimum(pos + 1, win).astype(F32)
            d = tot / cnt - cur
            y = _dot(d, w_ref[gi]) * scale_ref[:, lo:hi]
            o_ref[0, r0:r0 + tm, lo:hi] = y.astype(o_ref.dtype)


def _pool(mid, hist16, pool_w, pool_scale, pos0, dims):
    b, t, _ = mid.shape
    cb = dims["C_B"]
    group = pool_w.shape[1]
    assert len(POOL_WINDOWS) * group == cb
    tm = _tile(t, 256, SUBLANE)
    return pl.pallas_call(
        functools.partial(_pool_kernel, t=t, tm=tm, group=group, pos0=pos0),
        grid=(b,),
        in_specs=[pl.BlockSpec((1, t, cb), lambda i: (i, 0, 0)),
                  pl.BlockSpec((1, 2 * SUBLANE, cb), lambda i: (i, 0, 0)),
                  pl.BlockSpec(pool_w.shape, lambda i: (0, 0, 0)),
                  pl.BlockSpec((1, cb), lambda i: (0, 0))],
        out_specs=pl.BlockSpec((1, t, cb), lambda i: (i, 0, 0)),
        out_shape=jax.ShapeDtypeStruct((b, t, cb), BF16),
        scratch_shapes=[pltpu.VMEM((t + 2 * SUBLANE, cb), F32)],
        compiler_params=_cparams(1),
        name="pool",
    )(mid, hist16, pool_w, pool_scale.reshape(1, cb))


def _monotone_key(x):
    x = jnp.where(x == 0.0, 0.0, x)
    bits = lax.bitcast_convert_type(x, jnp.int32)
    return jnp.where(bits < 0, bits ^ jnp.int32(0x7FFFFFFF), bits)


def _kth_largest_key(key, k):
    def body(i, tau):
        cand = tau + jnp.left_shift(jnp.int32(1), jnp.int32(31) - i)
        cnt = jnp.sum(jnp.where(key >= cand, 1.0, 0.0), axis=-1, keepdims=True)
        return jnp.where(cnt >= k, cand, tau)

    tau0 = jnp.full((key.shape[0], 1), -2 ** 31, jnp.int32)
    return lax.fori_loop(0, 32, body, tau0)


def _prefix_count(ind):
    r = lax.broadcasted_iota(jnp.int32, (LANE, LANE), 0)
    c = lax.broadcasted_iota(jnp.int32, (LANE, LANE), 1)
    tri = jnp.where(r <= c, 1.0, 0.0).astype(BF16)
    run = jnp.zeros((ind.shape[0], 1), F32)
    outs = []
    for j in range(ind.shape[1] // LANE):
        pj = jnp.dot(ind[:, j * LANE:(j + 1) * LANE].astype(BF16), tri, preferred_element_type=F32)
        outs.append(pj + run)
        run = run + pj[:, LANE - 1:LANE]
    return outs[0] if len(outs) == 1 else jnp.concatenate(outs, axis=1)


def _topk_mask(scores, k):
    key = _monotone_key(scores)
    tau = _kth_largest_key(key, float(k))
    gt = key > tau
    eq = key == tau
    need = float(k) - jnp.sum(jnp.where(gt, 1.0, 0.0), axis=-1, keepdims=True)
    prefix = _prefix_count(jnp.where(eq, 1.0, 0.0))
    return gt | (eq & (prefix <= need))


def _dsa_prompt_kernel(q_ref, k_ref, v_ref, qi_ref, ki_ref, wi_ref, o_ref, sc_s, m_s, l_s, acc_s, *,
                       tq, t, kc, topk, h_idx, d_idx, n_heads, dh):
    q0 = pl.program_id(1) * tq
    n_kc = t // kc
    needed = (q0 + tq + kc - 1) // kc
    qpos = q0 + lax.broadcasted_iota(jnp.int32, (tq, 1), 0)
    sc_s[...] = jnp.full(sc_s.shape, -jnp.inf, F32)
    qi = [qi_ref[0, :, h * d_idx:(h + 1) * d_idx].astype(BF16) for h in range(h_idx)]
    wi = [wi_ref[0, :, h:h + 1] for h in range(h_idx)]

    def score_chunk(c, carry):
        keys = ki_ref[0, pl.ds(pl.multiple_of(c * kc, kc), kc), :].astype(BF16)
        acc = jnp.zeros((tq, kc), F32)
        for h in range(h_idx):
            acc = acc + jnp.maximum(_dot_nt(qi[h], keys), 0.0) * wi[h]
        spos = c * kc + lax.broadcasted_iota(jnp.int32, (1, kc), 1)
        sc_s[c] = jnp.where(spos <= qpos, acc, -jnp.inf)
        return carry

    lax.fori_loop(0, needed, score_chunk, 0)
    scores = jnp.concatenate([sc_s[c] for c in range(n_kc)], axis=1) if n_kc > 1 else sc_s[0]
    causal = lax.broadcasted_iota(jnp.int32, (1, t), 1) <= qpos
    valid = jnp.where(_topk_mask(scores, topk) & causal, 1.0, 0.0)
    for c in range(n_kc):
        sc_s[c] = valid[:, c * kc:(c + 1) * kc]

    m_s[...] = jnp.full(m_s.shape, NEG_BIG, F32)
    l_s[...] = jnp.zeros(l_s.shape, F32)
    acc_s[...] = jnp.zeros(acc_s.shape, F32)
    scale = dh ** -0.5

    def attn_chunk(c, carry):
        rows = pl.ds(pl.multiple_of(c * kc, kc), kc)
        ok = sc_s[c] > 0.5
        for h in range(n_heads):
            sl = slice(h * dh, (h + 1) * dh)
            lg = jnp.where(ok, _dot_nt(q_ref[0, :, sl], k_ref[0, rows, sl]) * scale, NEG_BIG)
            m_old = m_s[h]
            m_new = jnp.maximum(m_old, jnp.max(lg, axis=-1, keepdims=True))
            alpha = jnp.exp(m_old - m_new)
            p = jnp.where(ok, jnp.exp(lg - m_new), 0.0)
            l_s[h] = alpha * l_s[h] + jnp.sum(p, axis=-1, keepdims=True)
            acc_s[:, sl] = alpha * acc_s[:, sl] + _dot(p, v_ref[0, rows, sl])
            m_s[h] = m_new
        return carry

    lax.fori_loop(0, needed, attn_chunk, 0)
    for h in range(n_heads):
        sl = slice(h * dh, (h + 1) * dh)
        o_ref[0, :, sl] = (acc_s[:, sl] / l_s[h]).astype(o_ref.dtype)


def _dsa_prompt(q, k, v, qi, ki, wi, dims):
    b, t, cc = q.shape
    tq = _tile(t, 128, SUBLANE)
    kc = _tile(t, 512, tq)
    topk = min(TOPK_MAX, t // 4)
    h_idx, d_idx, n_heads = dims["H_IDX"], dims["D_IDX"], dims["H_C"]
    qb = lambda w: pl.BlockSpec((1, tq, w), lambda i, j: (i, j, 0))
    full = lambda w: pl.BlockSpec((1, t, w), lambda i, j: (i, 0, 0))
    return pl.pallas_call(
        functools.partial(_dsa_prompt_kernel, tq=tq, t=t, kc=kc, topk=topk, h_idx=h_idx, d_idx=d_idx,
                          n_heads=n_heads, dh=dims["DH_C"]),
        grid=(b, t // tq),
        in_specs=[qb(cc), full(cc), full(cc), qb(h_idx * d_idx), full(d_idx), qb(h_idx)],
        out_specs=qb(cc),
        out_shape=jax.ShapeDtypeStruct((b, t, cc), BF16),
        scratch_shapes=[pltpu.VMEM((t // kc, tq, kc), F32), pltpu.VMEM((n_heads, tq, 1), F32),
                        pltpu.VMEM((n_heads, tq, 1), F32), pltpu.VMEM((tq, cc), F32)],
        compiler_params=_cparams(2),
        name="dsa_prompt",
    )(q, k, v, qi, ki, wi)


def _idx_rows_scores(qi2, wi2, keys, tq, h_idx):
    s = jnp.maximum(_dot_nt(qi2, keys), 0.0) * wi2
    acc = s[0:tq]
    for h in range(1, h_idx):
        acc = acc + s[h * tq:(h + 1) * tq]
    return acc


def _dsa_sample_scores_kernel(pt_ref, qi_ref, wi_ref, *refs, tq, h_idx):
    o_ref = refs[-1]
    for g, kc_ref in enumerate(refs[:-1]):
        o_ref[0, :, g * PAGE:(g + 1) * PAGE] = _idx_rows_scores(qi_ref[0], wi_ref[0], kc_ref[0, 0], tq, h_idx)


def _dsa_sample_scores(page_table, qi2, wi2, cache_kidx, l, tq, h_idx):
    b, n_pages = page_table.shape
    d_idx = cache_kidx.shape[-1]
    rows = qi2.shape[1]
    group = _tile(n_pages, 16, 1)
    page = lambda g: pl.BlockSpec((1, 1, PAGE, d_idx), lambda i, p, pt: (l, pt[i, p * group + g], 0, 0))
    grid_spec = pltpu.PrefetchScalarGridSpec(
        num_scalar_prefetch=1,
        grid=(b, n_pages // group),
        in_specs=[pl.BlockSpec((1, rows, d_idx), lambda i, p, pt: (i, 0, 0)),
                  pl.BlockSpec((1, rows, 1), lambda i, p, pt: (i, 0, 0))] + [page(g) for g in range(group)],
        out_specs=pl.BlockSpec((1, tq, group * PAGE), lambda i, p, pt: (i, 0, p)),
    )
    return pl.pallas_call(
        functools.partial(_dsa_sample_scores_kernel, tq=tq, h_idx=h_idx),
        grid_spec=grid_spec,
        out_shape=jax.ShapeDtypeStruct((b, tq, n_pages * PAGE), F32),
        compiler_params=_cparams(2),
        name="dsa_sample_scores",
    )(page_table, qi2, wi2, *([cache_kidx] * group))


def _dsa_sample_select_kernel(sc_ref, qi_ref, wi_ref, kn_ref, o_ref, on_ref, *, tq, h_idx, topk, past):
    new = _idx_rows_scores(qi_ref[0], wi_ref[0], kn_ref[0], tq, h_idx)
    qrow = lax.broadcasted_iota(jnp.int32, (tq, PAGE), 0)
    col = lax.broadcasted_iota(jnp.int32, (tq, PAGE), 1)
    new_ok = col <= qrow
    scores = jnp.concatenate([sc_ref[0], jnp.where(new_ok, new, -jnp.inf)], axis=1)
    sel = jnp.where(_topk_mask(scores, topk), 1.0, 0.0)
    o_ref[0] = sel[:, :past]
    on_ref[0] = jnp.where(new_ok, sel[:, past:], 0.0)


def _dsa_sample_select(scores, qi2, wi2, ki_new_pad, tq, h_idx):
    b, _, past = scores.shape
    topk = min(TOPK_MAX, (past + tq) // 4)
    rows, d_idx = qi2.shape[1], qi2.shape[2]
    return pl.pallas_call(
        functools.partial(_dsa_sample_select_kernel, tq=tq, h_idx=h_idx, topk=topk, past=past),
        grid=(b,),
        in_specs=[pl.BlockSpec((1, tq, past), lambda i: (i, 0, 0)),
                  pl.BlockSpec((1, rows, d_idx), lambda i: (i, 0, 0)),
                  pl.BlockSpec((1, rows, 1), lambda i: (i, 0, 0)),
                  pl.BlockSpec((1, PAGE, d_idx), lambda i: (i, 0, 0))],
        out_specs=[pl.BlockSpec((1, tq, past), lambda i: (i, 0, 0)), pl.BlockSpec((1, tq, PAGE), lambda i: (i, 0, 0))],
        out_shape=[jax.ShapeDtypeStruct((b, tq, past), F32), jax.ShapeDtypeStruct((b, tq, PAGE), F32)],
        compiler_params=_cparams(1),
        name="dsa_sample_select",
    )(scores, qi2, wi2, ki_new_pad)


def _dsa_sample_attn_kernel(pt_ref, q_ref, kn_ref, vn_ref, m_ref, mn_ref, *refs, n_steps, group, n_heads, dh, tq):
    kc_refs, vc_refs = refs[:group], refs[group:2 * group]
    o_ref, q2_s, exp_s, hm_s, m_s, l_s, acc_s = refs[2 * group:]
    p = pl.program_id(1)
    rows = n_heads * tq
    cols = PAGE * n_heads

    @pl.when(p == 0)
    def _():
        for h in range(n_heads):
            q2_s[h * tq:(h + 1) * tq, :] = q_ref[0, :, h * dh:(h + 1) * dh]
        pos = lax.broadcasted_iota(jnp.int32, (PAGE, cols), 0)
        col = lax.broadcasted_iota(jnp.int32, (PAGE, cols), 1)
        exp_s[...] = jnp.where(col // n_heads == pos, 1.0, 0.0).astype(exp_s.dtype)
        rr = lax.broadcasted_iota(jnp.int32, (rows, cols), 0)
        cc = lax.broadcasted_iota(jnp.int32, (rows, cols), 1)
        hm_s[...] = jnp.where((cc % n_heads) == (rr // tq), 1.0, 0.0)
        m_s[...] = jnp.full(m_s.shape, NEG_BIG, F32)
        l_s[...] = jnp.zeros(l_s.shape, F32)
        acc_s[...] = jnp.zeros(acc_s.shape, F32)

    def update(kmat, vmat, sel):
        lg = _dot_nt(q2_s[...], kmat) * (dh ** -0.5)
        sel8 = jnp.dot(sel.astype(BF16), exp_s[...], preferred_element_type=F32)
        msk = jnp.concatenate([sel8] * n_heads, axis=0) * hm_s[...] > 0.5
        lgm = jnp.where(msk, lg, NEG_BIG)
        m_new = jnp.maximum(m_s[...], jnp.max(lgm, axis=-1, keepdims=True))
        alpha = jnp.exp(m_s[...] - m_new)
        pm = jnp.where(msk, jnp.exp(lgm - m_new), 0.0)
        l_s[...] = alpha * l_s[...] + jnp.sum(pm, axis=-1, keepdims=True)
        acc_s[...] = alpha * acc_s[...] + _dot(pm, vmat)
        m_s[...] = m_new

    @pl.when(p < n_steps)
    def _():
        for g in range(group):
            update(kc_refs[g][0, 0].reshape(cols, dh), vc_refs[g][0, 0].reshape(cols, dh),
                   m_ref[0, :, g * PAGE:(g + 1) * PAGE])

    @pl.when(p == n_steps)
    def _():
        update(kn_ref[0].reshape(cols, dh), vn_ref[0].reshape(cols, dh), mn_ref[0])
        res = acc_s[...] / l_s[...]
        for h in range(n_heads):
            o_ref[0, :, h * dh:(h + 1) * dh] = res[h * tq:(h + 1) * tq, :].astype(o_ref.dtype)


def _dsa_sample_attn(page_table, q, cache_k, cache_v, k_new_pad, v_new_pad, mask, mask_new, l, dims):
    b, n_pages = page_table.shape
    tq = q.shape[1]
    n_heads, dh = dims["H_C"], dims["DH_C"]
    cc = n_heads * dh
    group = _tile(n_pages, 8, 1)
    n_steps = n_pages // group
    page = lambda g: pl.BlockSpec(
        (1, 1, PAGE, n_heads, dh),
        lambda i, p, pt: (l, pt[i, jnp.minimum(p * group + g, n_pages - 1)], 0, 0, 0))
    new = pl.BlockSpec((1, PAGE, n_heads, dh), lambda i, p, pt: (i, 0, 0, 0))
    grid_spec = pltpu.PrefetchScalarGridSpec(
        num_scalar_prefetch=1,
        grid=(b, n_steps + 1),
        in_specs=[pl.BlockSpec((1, tq, cc), lambda i, p, pt: (i, 0, 0)), new, new,
                  pl.BlockSpec((1, tq, group * PAGE), lambda i, p, pt: (i, 0, jnp.minimum(p, n_steps - 1))),
                  pl.BlockSpec((1, tq, PAGE), lambda i, p, pt: (i, 0, 0))]
                 + [page(g) for g in range(group)] * 2,
        out_specs=pl.BlockSpec((1, tq, cc), lambda i, p, pt: (i, 0, 0)),
        scratch_shapes=[pltpu.VMEM((n_heads * tq, dh), BF16), pltpu.VMEM((PAGE, PAGE * n_heads), BF16),
                        pltpu.VMEM((n_heads * tq, PAGE * n_heads), F32), pltpu.VMEM((n_heads * tq, 1), F32),
                        pltpu.VMEM((n_heads * tq, 1), F32), pltpu.VMEM((n_heads * tq, dh), F32)],
    )
    return pl.pallas_call(
        functools.partial(_dsa_sample_attn_kernel, n_steps=n_steps, group=group, n_heads=n_heads, dh=dh, tq=tq),
        grid_spec=grid_spec,
        out_shape=jax.ShapeDtypeStruct((b, tq, cc), BF16),
        compiler_params=_cparams(2),
        name="dsa_sample_attn",
    )(page_table, q, k_new_pad, v_new_pad, mask, mask_new, *([cache_k] * group), *([cache_v] * group))


def _dsa_sample(q, k_f32, v_f32, qi, ki, wi, cache_k, cache_v, cache_kidx, page_table, l, dims):
    b, tq, cc = q.shape
    h_idx, d_idx, n_heads, dh = dims["H_IDX"], dims["D_IDX"], dims["H_C"], dims["DH_C"]
    qi2 = jnp.transpose(qi.reshape(b, tq, h_idx, d_idx), (0, 2, 1, 3)).reshape(b, h_idx * tq, d_idx)
    wi2 = jnp.transpose(wi, (0, 2, 1)).reshape(b, h_idx * tq, 1)
    pad_rows = lambda x: jnp.pad(x, ((0, 0), (0, PAGE - tq)) + ((0, 0),) * (x.ndim - 2))
    scores = _dsa_sample_scores(page_table, qi2, wi2, cache_kidx, l, tq, h_idx)
    mask, mask_new = _dsa_sample_select(scores, qi2, wi2, pad_rows(ki), tq, h_idx)
    return _dsa_sample_attn(page_table, q, cache_k, cache_v, pad_rows(k_f32), pad_rows(v_f32), mask, mask_new, l,
                            dims)


def _prepare_weights(raw, dims):
    ca, cb, cc, qiw, d_idx, h_idx, d = (dims[k] for k in ("C_A", "C_B", "C_C", "QIW", "D_IDX", "H_IDX", "D"))
    dd, da, dg = dims["D_DECAY"], dims["D_AAA"], dims["D_GATE"]
    assert dd <= LANE and da <= LANE and dg % LANE == 0
    a_cols = 3 * ca + dd + da + dg
    o = [0, a_cols, a_cols + cb, a_cols + cb + cc, a_cols + cb + 2 * cc, a_cols + cb + 3 * cc]
    o += [o[-1] + qiw, o[-1] + qiw + d_idx, o[-1] + qiw + d_idx + h_idx]
    w_in = raw["w_in"]

    def pack_pa(x):
        zeros = lambda n: jnp.zeros(x.shape[:-1] + (n,), x.dtype)
        return jnp.concatenate([x[..., :3 * ca], x[..., 3 * ca:3 * ca + dd], zeros(LANE - dd),
                                x[..., 3 * ca + dd:3 * ca + dd + da], zeros(LANE - da),
                                x[..., 3 * ca + dd + da:a_cols]], axis=-1)

    kiw = LANE - d_idx - h_idx
    assert kiw >= 0
    w_pa = pack_pa(w_in[..., :a_cols]).astype(BF16)
    w_mid = jnp.concatenate([w_in[..., o[1]:o[2]], w_in[..., o[4]:o[5]], w_in[..., o[2]:o[4]],
                             w_in[..., o[5]:o[8]], jnp.zeros(w_in.shape[:-1] + (kiw,), w_in.dtype)],
                            axis=-1).astype(BF16)
    w_gl = w_in[..., o[8]:].astype(BF16)
    pad_rows = lambda x: jnp.pad(x, ((0, 0), (0, LANE - x.shape[1]), (0, 0)))
    depth = w_in.shape[0]
    vec = lambda x: x.reshape(depth, 1, -1)
    P = dict(
        w_pa=w_pa, w_mid=w_mid, w_gl=w_gl,
        mu=vec(pack_pa(raw["rwkv_mu"])), w0=vec(raw["rwkv_w0"]), w2=pad_rows(raw["rwkv_w2"]).astype(BF16),
        a0=vec(raw["rwkv_a0"]), a2=pad_rows(raw["rwkv_a2"]).astype(BF16), g2=raw["rwkv_g2"].astype(BF16),
        k_k=vec(raw["rwkv_k_k"]), k_a=vec(raw["rwkv_k_a"]), r_k=vec(raw["rwkv_r_k"]),
        ln_w=vec(raw["rwkv_ln_w"]), ln_b=vec(raw["rwkv_ln_b"]),
        pool_w=raw["pool_w"].astype(BF16), pool_scale=raw["pool_scale"],
    )
    for name in ("ffn1_w_gate", "ffn1_w_up", "ffn1_w_down", "ffn2_w_gate", "ffn2_w_up", "ffn2_w_down",
                 "w_br_a", "w_br_b", "w_br_c", "w_out"):
        P[name] = raw[name].astype(BF16)
    for name in ("ffn1_norm", "mix_norm", "ffn2_norm"):
        P[name] = raw[name]
    P["pack_pa"] = pack_pa
    return P


def _unpack_pa(x, dims):
    ca, dd, da = dims["C_A"], dims["D_DECAY"], dims["D_AAA"]
    return jnp.concatenate([x[..., :3 * ca], x[..., 3 * ca:3 * ca + dd],
                            x[..., 3 * ca + LANE:3 * ca + LANE + da], x[..., 3 * ca + 2 * LANE:]], axis=-1)


def _layer(x, P, l, pos0, shift_prev, s0, pool_hist, dims, sample_ctx):
    b, t, d = x.shape
    m = b * t
    ca, cb, cc = dims["C_A"], dims["C_B"], dims["C_C"]
    h = _ffn(x.reshape(m, d), P["ffn1_norm"][l], P["ffn1_w_gate"][l], P["ffn1_w_up"][l], P["ffn1_w_down"][l])
    u = _rmsnorm(h, P["mix_norm"][l], BF16)
    pa = _mm(u, P["w_pa"][l], out_dtype=F32, name="proj_pa").reshape(b, t, -1)
    mid = _mm(u, P["w_mid"][l], out_dtype=F32, name="proj_mid").reshape(b, t, -1)
    gates = _mm(u, P["w_gl"][l], out_dtype=BF16, act="sigmoid", name="proj_gate")

    chunk = min(N_A, t)
    rt, at, bt, kt, v_a, bkt, gct, bonus, g = _rwkv_prep(pa, P["pack_pa"](shift_prev), P, l, dims, chunk)
    y_a, new_wkv = _rwkv_scan(rt, at, bt, kt, v_a, bkt, gct, s0, chunk)
    ya = _rwkv_post(y_a.reshape(m, ca), bonus.reshape(m, ca), g.reshape(m, ca), P["ln_w"][l], P["ln_b"][l])
    new_shift = _unpack_pa(pa[:, -1], dims)

    hist16 = jnp.pad(pool_hist, ((0, 0), (2 * SUBLANE - POOL_HIST, 0), (0, 0)))
    yb = _pool(mid, hist16, P["pool_w"][l], P["pool_scale"][l], pos0, dims)
    zb = mid[..., :cb]
    new_pool = jnp.concatenate([pool_hist, zb], axis=1)[:, -POOL_HIST:]

    q, k, v, k_bf, v_bf, qi, kiwi = _rope_all(mid, pos0, dims)
    ki = kiwi[..., :dims["D_IDX"]]
    wi = kiwi[..., dims["D_IDX"]:dims["D_IDX"] + dims["H_IDX"]]
    if sample_ctx is None:
        yc = _dsa_prompt(q, k_bf, v_bf, qi, ki, wi, dims)
    else:
        cache_k, cache_v, cache_kidx, page_table = sample_ctx
        yc = _dsa_sample(q, k, v, qi, ki, wi, cache_k, cache_v, cache_kidx, page_table, l, dims)

    merged = _merge(ya, yb.reshape(m, cb), yc.reshape(m, cc), P["w_br_a"][l], P["w_br_b"][l], P["w_br_c"][l], gates)
    h = _mm(merged, P["w_out"][l], out_dtype=F32, res=h, name="w_out")
    h = _ffn(h, P["ffn2_norm"][l], P["ffn2_w_gate"][l], P["ffn2_w_up"][l], P["ffn2_w_down"][l])
    return h.reshape(b, t, d), (k, v, ki, new_wkv, new_shift, new_pool)


def kernel(x_prompt, x_sample, cache_k, cache_v, cache_kidx, state_wkv, state_shift, state_pool, page_table,
           ffn1_norm, ffn1_w_gate, ffn1_w_up, ffn1_w_down, mix_norm, w_in, rwkv_mu, rwkv_w0, rwkv_w2,
           rwkv_a0, rwkv_a2, rwkv_g2, rwkv_k_k, rwkv_k_a, rwkv_r_k, rwkv_ln_w, rwkv_ln_b, w_br_a, pool_w,
           pool_scale, w_br_b, w_br_c, w_out, ffn2_norm, ffn2_w_gate, ffn2_w_up, ffn2_w_down, final_norm):
    raw = dict(ffn1_norm=ffn1_norm, ffn1_w_gate=ffn1_w_gate, ffn1_w_up=ffn1_w_up, ffn1_w_down=ffn1_w_down,
               mix_norm=mix_norm, w_in=w_in, rwkv_mu=rwkv_mu, rwkv_w0=rwkv_w0, rwkv_w2=rwkv_w2, rwkv_a0=rwkv_a0,
               rwkv_a2=rwkv_a2, rwkv_g2=rwkv_g2, rwkv_k_k=rwkv_k_k, rwkv_k_a=rwkv_k_a, rwkv_r_k=rwkv_r_k,
               rwkv_ln_w=rwkv_ln_w, rwkv_ln_b=rwkv_ln_b, w_br_a=w_br_a, pool_w=pool_w, pool_scale=pool_scale,
               w_br_b=w_br_b, w_br_c=w_br_c, w_out=w_out, ffn2_norm=ffn2_norm, ffn2_w_gate=ffn2_w_gate,
               ffn2_w_up=ffn2_w_up, ffn2_w_down=ffn2_w_down)
    depth, d = mix_norm.shape
    ca = rwkv_w0.shape[-1]
    cb = pool_scale.shape[-1]
    n_heads, dh = cache_k.shape[3], cache_k.shape[4]
    cc = n_heads * dh
    d_idx = cache_kidx.shape[-1]
    dd, da, dg = rwkv_w2.shape[1], rwkv_a2.shape[1], rwkv_g2.shape[1]
    a_cols = 3 * ca + dd + da + dg
    h_idx = (w_in.shape[-1] - a_cols - cb - 3 * cc - d_idx - 3 * d) // (d_idx + 1)
    dims = dict(D=d, C_A=ca, C_B=cb, C_C=cc, H_C=n_heads, DH_C=dh, D_IDX=d_idx, H_IDX=h_idx,
                QIW=h_idx * d_idx, D_DECAY=dd, D_AAA=da, D_GATE=dg)
    assert w_in.shape[-1] == a_cols + cb + 3 * cc + h_idx * d_idx + d_idx + h_idx + 3 * d
    P = _prepare_weights(raw, dims)

    bp = x_prompt.shape[0]
    past = page_table.shape[1] * PAGE
    yp, ys = x_prompt, x_sample
    st_p, st_s = [], []
    for l in range(depth):
        zero_shift = jnp.zeros((bp, a_cols), F32)
        zero_wkv = jnp.zeros((bp, ca // N_A, N_A, N_A), F32)
        zero_pool = jnp.zeros((bp, POOL_HIST, cb), F32)
        yp, sp = _layer(yp, P, l, 0, zero_shift, zero_wkv, zero_pool, dims, None)
        ys, ss = _layer(ys, P, l, past, state_shift[l], state_wkv[l], state_pool[l], dims,
                        (cache_k, cache_v, cache_kidx, page_table))
        st_p.append(sp)
        st_s.append(ss)
    fin = lambda y: _rmsnorm(y.reshape(-1, d), final_norm, F32).reshape(y.shape)
    stk = lambda sts, i: jnp.stack([s[i] for s in sts])
    return (fin(yp), fin(ys),
            stk(st_p, 0), stk(st_p, 1), stk(st_p, 2), stk(st_p, 3), stk(st_p, 4), stk(st_p, 5),
            stk(st_s, 0), stk(st_s, 1), stk(st_s, 2), stk(st_s, 3), stk(st_s, 4), stk(st_s, 5))
```

```python
import functools
import math

import jax
import jax.numpy as jnp
from jax import lax
from jax.experimental import pallas as pl
from jax.experimental.pallas import tpu as pltpu

F32 = jnp.float32
BF16 = jnp.bfloat16

LANE = 128
SUBLANE = 8
VMEM_LIMIT_BYTES = 56 * 1024 * 1024

N_A = 64
GN_EPS = 64e-5
POOL_WINDOWS = (2, 4, 8, 16)
POOL_HIST = max(POOL_WINDOWS) - 1
TOPK_MAX = 256
ROPE_THETA = 10000.0
RMS_EPS = 1e-6
PAGE = 128
NEG_BIG = -1e30


def _cparams(n_axes):
    return pltpu.CompilerParams(dimension_semantics=("arbitrary",) * n_axes,
                                vmem_limit_bytes=VMEM_LIMIT_BYTES)


def _tile(n, target, mult):
    best = None
    for t in range(mult, min(n, target) + 1, mult):
        if n % t == 0:
            best = t
    return best if best is not None else n


def _roundup(n, m):
    return -(-n // m) * m


def _dot(a, b):
    return jnp.dot(a.astype(BF16), b.astype(BF16), preferred_element_type=F32)


def _dot_nt(a, b):
    return lax.dot_general(a.astype(BF16), b.astype(BF16), (((1,), (1,)), ((), ())),
                           preferred_element_type=F32)


def _split3(x):
    h1 = x.astype(BF16)
    r1 = x - h1.astype(F32)
    h2 = r1.astype(BF16)
    h3 = (r1 - h2.astype(F32)).astype(BF16)
    return h1, h2, h3


def _dot_exact_rhs(x, m_bf16):
    h1, h2, h3 = _split3(x)
    return (jnp.dot(h1, m_bf16, preferred_element_type=F32)
            + jnp.dot(h2, m_bf16, preferred_element_type=F32)
            + jnp.dot(h3, m_bf16, preferred_element_type=F32))


def _dot_exact_lhs(m_bf16, x):
    h1, h2, h3 = _split3(x)
    return (jnp.dot(m_bf16, h1, preferred_element_type=F32)
            + jnp.dot(m_bf16, h2, preferred_element_type=F32)
            + jnp.dot(m_bf16, h3, preferred_element_type=F32))


def _rmsnorm_kernel(x_ref, g_ref, o_ref):
    x = x_ref[...]
    ms = jnp.mean(x * x, axis=-1, keepdims=True)
    o_ref[...] = (x * lax.rsqrt(ms + RMS_EPS) * g_ref[...]).astype(o_ref.dtype)


def _rmsnorm(x2d, g, out_dtype):
    m, d = x2d.shape
    tm = _tile(m, 256, SUBLANE)
    return pl.pallas_call(
        _rmsnorm_kernel,
        grid=(m // tm,),
        in_specs=[pl.BlockSpec((tm, d), lambda i: (i, 0)), pl.BlockSpec((1, d), lambda i: (0, 0))],
        out_specs=pl.BlockSpec((tm, d), lambda i: (i, 0)),
        out_shape=jax.ShapeDtypeStruct((m, d), out_dtype),
        compiler_params=_cparams(1),
        name="rmsnorm",
    )(x2d, g.reshape(1, d))


def _prenorm_kernel(x_ref, g_ref, xb_ref, ssq_ref):
    x = x_ref[...]
    xb_ref[...] = (x * g_ref[...]).astype(xb_ref.dtype)
    ssq_ref[...] = jnp.sum(x * x, axis=-1, keepdims=True)


def _prenorm(x2d, g):
    m, d = x2d.shape
    tm = _tile(m, 256, SUBLANE)
    return pl.pallas_call(
        _prenorm_kernel,
        grid=(m // tm,),
        in_specs=[pl.BlockSpec((tm, d), lambda i: (i, 0)), pl.BlockSpec((1, d), lambda i: (0, 0))],
        out_specs=[pl.BlockSpec((tm, d), lambda i: (i, 0)), pl.BlockSpec((tm, 1), lambda i: (i, 0))],
        out_shape=[jax.ShapeDtypeStruct((m, d), BF16), jax.ShapeDtypeStruct((m, 1), F32)],
        compiler_params=_cparams(1),
        name="prenorm",
    )(x2d, g.reshape(1, d))


def _row_scale(ssq, width):
    return lax.rsqrt(ssq * (1.0 / width) + RMS_EPS)


def _mm_kernel(*refs, nk, kdim, has_ssq, has_res, has_next, res_scale, act):
    it = iter(refs)
    x_ref, w_ref = next(it), next(it)
    ssq_ref = next(it) if has_ssq else None
    res_ref = next(it) if has_res else None
    gnext_ref = next(it) if has_next else None
    o_ref = next(it)
    ob_ref, ssqo_ref = (next(it), next(it)) if has_next else (None, None)

    def epilogue(acc):
        if has_ssq:
            acc = acc * _row_scale(ssq_ref[...], kdim)
        if act == "sigmoid":
            acc = jax.nn.sigmoid(acc)
        if has_res:
            acc = res_ref[...] + res_scale * acc
        o_ref[...] = acc.astype(o_ref.dtype)
        if has_next:
            ob_ref[...] = (acc * gnext_ref[...]).astype(ob_ref.dtype)
            part_ssq = jnp.sum(acc * acc, axis=-1, keepdims=True)
            j = pl.program_id(1)

            @pl.when(j == 0)
            def _():
                ssqo_ref[...] = part_ssq

            @pl.when(j > 0)
            def _():
                ssqo_ref[...] += part_ssq

    part = jnp.dot(x_ref[...], w_ref[0], preferred_element_type=F32)
    if nk == 1:
        epilogue(part)
    else:
        acc_ref = next(it)
        k = pl.program_id(2)

        @pl.when(k == 0)
        def _():
            acc_ref[...] = part

        @pl.when(k > 0)
        def _():
            acc_ref[...] += part

        @pl.when(k == nk - 1)
        def _():
            epilogue(acc_ref[...])


def _mm(x, w, l, *, out_dtype, ssq=None, res=None, res_scale=1.0, act=None, next_g=None, tm_target=1024,
        tn_target=1024, tk_target=4096, name="mm"):
    m, kdim = x.shape
    n = w.shape[2]
    tm = _tile(m, tm_target, SUBLANE)
    tn = _tile(n, tn_target, LANE)
    tk = _tile(kdim, tk_target, LANE)
    nk = kdim // tk
    tile = pl.BlockSpec((tm, tn), lambda i, j, k: (i, j))
    col = pl.BlockSpec((tm, 1), lambda i, j, k: (i, 0))
    in_specs = [pl.BlockSpec((tm, tk), lambda i, j, k: (i, k)),
                pl.BlockSpec((1, tk, tn), lambda i, j, k: (l, k, j))]
    args = [x, w]
    if ssq is not None:
        in_specs.append(col)
        args.append(ssq)
    if res is not None:
        in_specs.append(tile)
        args.append(res)
    out_specs, out_shape = tile, jax.ShapeDtypeStruct((m, n), out_dtype)
    if next_g is not None:
        assert res is not None
        in_specs.append(pl.BlockSpec((1, tn), lambda i, j, k: (0, j)))
        args.append(next_g.reshape(1, n))
        out_specs = [tile, tile, col]
        out_shape = [out_shape, jax.ShapeDtypeStruct((m, n), BF16), jax.ShapeDtypeStruct((m, 1), F32)]
    scratch = [pltpu.VMEM((tm, tn), F32)] if nk > 1 else []
    return pl.pallas_call(
        functools.partial(_mm_kernel, nk=nk, kdim=kdim, has_ssq=ssq is not None, has_res=res is not None,
                          has_next=next_g is not None, res_scale=res_scale, act=act),
        grid=(m // tm, n // tn, nk),
        in_specs=in_specs,
        out_specs=out_specs,
        out_shape=out_shape,
        scratch_shapes=scratch,
        compiler_params=_cparams(3),
        name=name,
    )(*args)


def _swiglu_up_kernel(x_ref, ssq_ref, wg_ref, wu_ref, o_ref, *, d):
    x = x_ref[...]
    scale = _row_scale(ssq_ref[...], d)
    g = jnp.dot(x, wg_ref[0].astype(BF16), preferred_element_type=F32) * scale
    u = jnp.dot(x, wu_ref[0].astype(BF16), preferred_element_type=F32) * scale
    o_ref[...] = (g * jax.nn.sigmoid(g) * u).astype(o_ref.dtype)


def _swiglu_up(x, ssq, wg, wu, l):
    m, d = x.shape
    f = wg.shape[2]
    tm = _tile(m, 1024, SUBLANE)
    tn = _tile(f, 512, LANE)
    return pl.pallas_call(
        functools.partial(_swiglu_up_kernel, d=d),
        grid=(m // tm, f // tn),
        in_specs=[pl.BlockSpec((tm, d), lambda i, j: (i, 0)),
                  pl.BlockSpec((tm, 1), lambda i, j: (i, 0)),
                  pl.BlockSpec((1, d, tn), lambda i, j: (l, 0, j)),
                  pl.BlockSpec((1, d, tn), lambda i, j: (l, 0, j))],
        out_specs=pl.BlockSpec((tm, tn), lambda i, j: (i, j)),
        out_shape=jax.ShapeDtypeStruct((m, f), BF16),
        compiler_params=_cparams(2),
        name="swiglu_up",
    )(x, ssq, wg, wu)


def _ffn(h, hb, ssq, wg, wu, wd, l, next_g):
    a = _swiglu_up(hb, ssq, wg, wu, l)
    tk = _tile(a.shape[1], 5632, LANE)
    return _mm(a, wd, l, out_dtype=F32, res=h, res_scale=0.5, next_g=next_g, tm_target=1024, tn_target=512,
               tk_target=tk, name="ffn_down")


def _merge_kernel(ya_ref, yb_ref, yc_ref, wa_ref, wb_ref, wc_ref, ga_ref, gb_ref, gc_ref, o_ref):
    a = jnp.dot(ya_ref[...], wa_ref[0], preferred_element_type=F32)
    b = jnp.dot(yb_ref[...], wb_ref[0], preferred_element_type=F32)
    c = jnp.dot(yc_ref[...], wc_ref[0], preferred_element_type=F32)
    out = ga_ref[...].astype(F32) * a + gb_ref[...].astype(F32) * b + gc_ref[...].astype(F32) * c
    o_ref[...] = out.astype(o_ref.dtype)


def _merge(ya, yb, yc, wa, wb, wc, l, gates):
    m = ya.shape[0]
    d = wa.shape[2]
    tm = _tile(m, 1024, SUBLANE)
    tn = _tile(d, 512, LANE)
    nb = d // tn
    wspec = lambda w: pl.BlockSpec((1, w.shape[1], tn), lambda i, j: (l, 0, j))
    return pl.pallas_call(
        _merge_kernel,
        grid=(m // tm, nb),
        in_specs=[pl.BlockSpec((tm, ya.shape[1]), lambda i, j: (i, 0)),
                  pl.BlockSpec((tm, yb.shape[1]), lambda i, j: (i, 0)),
                  pl.BlockSpec((tm, yc.shape[1]), lambda i, j: (i, 0)),
                  wspec(wa), wspec(wb), wspec(wc),
                  pl.BlockSpec((tm, tn), lambda i, j: (i, j)),
                  pl.BlockSpec((tm, tn), lambda i, j: (i, nb + j)),
                  pl.BlockSpec((tm, tn), lambda i, j: (i, 2 * nb + j))],
        out_specs=pl.BlockSpec((tm, tn), lambda i, j: (i, j)),
        out_shape=jax.ShapeDtypeStruct((m, d), BF16),
        compiler_params=_cparams(2),
        name="merge",
    )(ya, yb, yc, wa, wb, wc, gates, gates, gates)


def _rope_kernel(q_ref, k_ref, v_ref, qi_ref, kw_ref, c128_ref, s128_ref, c64_ref, s64_ref,
                 qo_ref, ko_ref, vo_ref, kb_ref, vb_ref, qio_ref, kwo_ref, *, n_heads, n_idx_groups, d_idx,
                 idx_scale):
    c128, s128 = c128_ref[...], s128_ref[...]
    c64, s64 = c64_ref[...], s64_ref[...]
    lane = lax.broadcasted_iota(jnp.int32, c64.shape, 1)
    first_half = (lane % d_idx) < (d_idx // 2)

    def rope128(x):
        return x * c128 + pltpu.roll(x, LANE // 2, axis=1) * s128

    def rope64(x):
        rot = jnp.where(first_half, pltpu.roll(x, LANE - d_idx // 2, axis=1), pltpu.roll(x, d_idx // 2, axis=1))
        return x * c64 + rot * s64

    for h in range(n_heads):
        sl = slice(h * LANE, (h + 1) * LANE)
        qo_ref[0, :, sl] = rope128(q_ref[0, :, sl]).astype(qo_ref.dtype)
        kr = rope128(k_ref[0, :, sl])
        ko_ref[0, :, h, :] = kr
        kb_ref[0, :, sl] = kr.astype(kb_ref.dtype)
        vo_ref[0, :, h, :] = v_ref[0, :, sl]
    vb_ref[0] = v_ref[0].astype(vb_ref.dtype)
    for g in range(n_idx_groups):
        sl = slice(g * LANE, (g + 1) * LANE)
        qio_ref[0, :, sl] = rope64(qi_ref[0, :, sl])
    kw = kw_ref[0]
    kwo_ref[0] = jnp.where(lane < d_idx, rope64(kw), kw * idx_scale)


def _rope_tables(pos, d):
    inv = ROPE_THETA ** (-jnp.arange(0, d, 2, dtype=F32) / d)
    ang = pos.astype(F32)[:, None] * inv[None, :]
    cos, sin = jnp.cos(ang), jnp.sin(ang)
    reps = LANE // d
    c = jnp.tile(jnp.concatenate([cos, cos], axis=-1), (1, reps))
    s = jnp.tile(jnp.concatenate([-sin, sin], axis=-1), (1, reps))
    return c, s


def _rope_all(mid, pos0, dims):
    b, t, _ = mid.shape
    cb, cc, qiw, d_idx, h_idx = dims["C_B"], dims["C_C"], dims["QIW"], dims["D_IDX"], dims["H_IDX"]
    assert dims["DH_C"] == LANE and LANE % d_idx == 0
    assert cb % cc == 0 and (cb + 3 * cc) % qiw == 0
    tm = _tile(t, 256, SUBLANE)
    pos = pos0 + jnp.arange(t)
    c128, s128 = _rope_tables(pos, LANE)
    c64, s64 = _rope_tables(pos, d_idx)
    ob = cb // cc
    row = lambda width, idx: pl.BlockSpec((1, tm, width), lambda i, j: (i, j, idx))
    tab = pl.BlockSpec((tm, LANE), lambda i, j: (j, 0))
    out = lambda width: pl.BlockSpec((1, tm, width), lambda i, j: (i, j, 0))
    split = pl.BlockSpec((1, tm, cc // LANE, LANE), lambda i, j: (i, j, 0, 0))
    return pl.pallas_call(
        functools.partial(_rope_kernel, n_heads=cc // LANE, n_idx_groups=qiw // LANE, d_idx=d_idx,
                          idx_scale=(h_idx * d_idx) ** -0.5),
        grid=(b, t // tm),
        in_specs=[row(cc, ob + 1), row(cc, ob + 2), row(cc, ob), row(qiw, (cb + 3 * cc) // qiw),
                  row(LANE, (cb + 3 * cc + qiw) // LANE), tab, tab, tab, tab],
        out_specs=[out(cc), split, split, out(cc), out(cc), out(qiw), out(LANE)],
        out_shape=[jax.ShapeDtypeStruct((b, t, cc), BF16), jax.ShapeDtypeStruct((b, t, cc // LANE, LANE), F32),
                   jax.ShapeDtypeStruct((b, t, cc // LANE, LANE), F32),
                   jax.ShapeDtypeStruct((b, t, cc), BF16), jax.ShapeDtypeStruct((b, t, cc), BF16),
                   jax.ShapeDtypeStruct((b, t, qiw), F32), jax.ShapeDtypeStruct((b, t, LANE), F32)],
        compiler_params=_cparams(2),
        name="rope",
    )(mid, mid, mid, mid, mid, c128, s128, c64, s64)


def _head_sum(x, jmat):
    parts = []
    for g in range(x.shape[1] // LANE):
        parts.append(_dot_exact_rhs(x[:, g * LANE:(g + 1) * LANE], jmat))
    return parts[0] if len(parts) == 1 else jnp.concatenate(parts, axis=1)


def _head_ones():
    r = lax.broadcasted_iota(jnp.int32, (LANE, LANE), 0)
    c = lax.broadcasted_iota(jnp.int32, (LANE, LANE), 1)
    return jnp.where((r // N_A) == (c // N_A), 1.0, 0.0).astype(BF16)


def _rwkv_prep_kernel(pa_ref, halo_ref, shift_ref, mu_ref, w0_ref, w2_ref, a0_ref, a2_ref, g2_ref,
                      kk_ref, ka_ref, rk_ref,
                      rt_ref, at_ref, bt_ref, kt_ref, v_ref, bkt_ref, gct_ref, bonus_ref, g_ref, *, ca, chunk, tm):
    j = pl.program_id(1)
    prev_last = jnp.where(j == 0, shift_ref[0], halo_ref[0, SUBLANE - 1:SUBLANE, :])
    row = lax.broadcasted_iota(jnp.int32, (tm, 1), 0)

    def shifted(lo, hi):
        x = pa_ref[0, :, lo:hi]
        prev = jnp.where(row == 0, prev_last[:, lo:hi], pltpu.roll(x, 1, axis=0))
        return x + mu_ref[:, lo:hi] * (prev - x)

    r = shifted(0, ca)
    k = shifted(ca, 2 * ca)
    v = shifted(2 * ca, 3 * ca)
    wl = shifted(3 * ca, 3 * ca + LANE)
    al = shifted(3 * ca + LANE, 3 * ca + 2 * LANE)
    gl = shifted(3 * ca + 2 * LANE, pa_ref.shape[2])

    z = -(w0_ref[...] + _dot(jnp.tanh(wl), w2_ref[...]))
    softplus = jnp.maximum(z, 0.0) + jnp.log(1.0 + jnp.exp(-jnp.abs(z)))
    log_decay = -jnp.exp(-softplus - 0.5)
    rr = lax.broadcasted_iota(jnp.int32, (tm, tm), 0)
    cc = lax.broadcasted_iota(jnp.int32, (tm, tm), 1)
    same_chunk = (rr // chunk) == (cc // chunk)
    tril = jnp.where(same_chunk & (cc <= rr), 1.0, 0.0).astype(BF16)
    ones_blk = jnp.where(same_chunk, 1.0, 0.0).astype(BF16)
    cum = _dot_exact_lhs(tril, log_decay)
    cum_c = _dot_exact_lhs(ones_blk, log_decay)
    fwd = jnp.exp(cum_c - cum)
    back = jnp.exp(cum - cum_c)
    back_prev = jnp.exp(cum - log_decay - cum_c)
    gam_c = jnp.exp(cum_c)

    a = jax.nn.sigmoid(a0_ref[...] + _dot(al, a2_ref[...]))
    jmat = _head_ones()
    kk = k * kk_ref[...]
    kk = kk / jnp.maximum(jnp.sqrt(_head_sum(kk * kk, jmat)), 1e-12)
    kh = k * (1.0 + (a - 1.0) * ka_ref[...])
    bonus_ref[0] = _head_sum(r * kh * rk_ref[...], jmat) * v
    g_ref[0] = _dot(jax.nn.sigmoid(gl), g2_ref[...])
    bh = kk * a * fwd
    kf = kh * fwd
    rt_ref[0] = (r * back).astype(rt_ref.dtype)
    at_ref[0] = (-kk * back_prev).astype(at_ref.dtype)
    bt_ref[0] = bh.astype(bt_ref.dtype)
    kt_ref[0] = kf.astype(kt_ref.dtype)
    v_ref[0] = v.astype(v_ref.dtype)
    pad = N_A - chunk
    for c in range(tm // chunk):
        rows = slice(c * chunk, (c + 1) * chunk)
        if pad:
            zeros = jnp.zeros((pad, ca), F32)
            stacked = jnp.concatenate([bh[rows], zeros, kf[rows], zeros], axis=0)
        else:
            stacked = jnp.concatenate([bh[rows], kf[rows]], axis=0)
        bkt_ref[0, c] = stacked.T.astype(bkt_ref.dtype)
        gct_ref[0, c] = jnp.broadcast_to(gam_c[c * chunk:c * chunk + 1], (LANE, ca)).T


def _rwkv_prep(pa, shift_p, P, l, dims, chunk):
    b, t, paw = pa.shape
    ca = dims["C_A"]
    tm = chunk * max(1, min(128, t) // chunk)
    assert t % tm == 0 and 2 * N_A == LANE
    hb = tm // SUBLANE
    nct = tm // chunk
    row = pl.BlockSpec((1, tm, ca), lambda i, j: (i, j, 0))
    cmaj = pl.BlockSpec((1, nct, ca, LANE), lambda i, j: (i, j, 0, 0))
    vec = lambda w: pl.BlockSpec((1, w), lambda i, j: (0, 0))
    mat = lambda r: pl.BlockSpec((r, ca), lambda i, j: (0, 0))
    tmaj = lambda dt: jax.ShapeDtypeStruct((b, t, ca), dt)
    return pl.pallas_call(
        functools.partial(_rwkv_prep_kernel, ca=ca, chunk=chunk, tm=tm),
        grid=(b, t // tm),
        in_specs=[pl.BlockSpec((1, tm, paw), lambda i, j: (i, j, 0)),
                  pl.BlockSpec((1, SUBLANE, paw), lambda i, j: (i, jnp.maximum(j * hb - 1, 0), 0)),
                  pl.BlockSpec((1, 1, paw), lambda i, j: (i, 0, 0)),
                  vec(paw), vec(ca), mat(LANE), vec(ca), mat(LANE), mat(P["g2"].shape[1]),
                  vec(ca), vec(ca), vec(ca)],
        out_specs=[row] * 5 + [cmaj, cmaj, row, row],
        out_shape=[tmaj(BF16)] * 5 + [jax.ShapeDtypeStruct((b, t // chunk, ca, LANE), BF16),
                                      jax.ShapeDtypeStruct((b, t // chunk, ca, LANE), F32), tmaj(F32), tmaj(F32)],
        compiler_params=_cparams(2),
        name="rwkv_prep",
    )(pa, pa, shift_p.reshape(b, 1, paw), P["mu"][l], P["w0"][l], P["w2"][l], P["a0"][l], P["a2"][l],
      P["g2"][l], P["k_k"][l], P["k_a"][l], P["r_k"][l])


def _rwkv_scan_kernel(at_ref, rt_ref, v_ref, bt_ref, kt_ref, bkt_ref, gct_ref, s0_ref, y_ref, s_ref, *,
                      n_pairs, chunk, unroll):
    c = pl.program_id(1)

    @pl.when(c == 0)
    def _():
        s_ref[...] = s0_ref[...]

    rr = lax.broadcasted_iota(jnp.int32, (chunk, chunk), 0)
    cc = lax.broadcasted_iota(jnp.int32, (chunk, chunk), 1)
    lower_strict = cc < rr
    lower_incl = cc <= rr
    eye = jnp.where(rr == cc, 1.0, 0.0)
    first = lax.broadcasted_iota(jnp.int32, (chunk, LANE), 1) < N_A
    br = lax.broadcasted_iota(jnp.int32, (LANE, LANE), 0)
    bc = lax.broadcasted_iota(jnp.int32, (LANE, LANE), 1)
    block_diag = (br // N_A) == (bc // N_A)
    n_double = max(int(math.log2(chunk)) - 1, 0)
    pad = N_A - chunk

    def load(p):
        sl = pl.ds(pl.multiple_of(p * LANE, LANE), LANE)
        return (sl,
                at_ref[0, :, sl],
                rt_ref[0, :, sl],
                v_ref[0, :, sl],
                bt_ref[0, :, sl],
                kt_ref[0, :, sl],
                bkt_ref[0, 0, sl, :],
                s_ref[0, sl, :] * gct_ref[0, 0, sl, :])

    def group(g, carry):
        loaded = [load(g * unroll + j) for j in range(unroll)]
        sls, a, r, v, b, k, bk_t, sb = (list(col) for col in zip(*loaded))
        pairs = range(unroll)
        heads = [(i, hh) for i in pairs for hh in range(2)]
        own = lambda hh: first if hh == 0 else jnp.logical_not(first)
        sb16 = [sb[i].astype(BF16) for i in pairs]
        x = [jnp.dot(a[i], sb16[i], preferred_element_type=F32) for i in pairs]
        y = [jnp.dot(r[i], sb16[i], preferred_element_type=F32) for i in pairs]
        a_h = [jnp.where(own(hh), a[i], jnp.zeros_like(a[i])) for i, hh in heads]
        r_h = [jnp.where(own(hh), r[i], jnp.zeros_like(r[i])) for i, hh in heads]
        n_ab = [jnp.where(lower_strict, _dot_nt(a_h[j], b[i]), 0.0) for j, (i, _) in enumerate(heads)]
        n_ak = [jnp.where(lower_strict, _dot_nt(a_h[j], k[i]), 0.0) for j, (i, _) in enumerate(heads)]
        m_rb = [jnp.where(lower_incl, _dot_nt(r_h[j], b[i]), 0.0) for j, (i, _) in enumerate(heads)]
        m_rk = [jnp.where(lower_incl, _dot_nt(r_h[j], k[i]), 0.0) for j, (i, _) in enumerate(heads)]
        inv = [eye + n for n in n_ab]
        npow = n_ab
        for _ in range(n_double):
            npow = [_dot(n, n) for n in npow]
            inv = [iv + _dot(iv, n) for iv, n in zip(inv, npow)]
        w = [x[i] + _dot(n_ak[j], v[i]) for j, (i, _) in enumerate(heads)]
        u = [_dot(inv[j], w[j]) for j in range(len(heads))]
        y_h = [_dot(m_rb[j], u[j]) + _dot(m_rk[j], v[i]) for j, (i, _) in enumerate(heads)]
        for i in pairs:
            u_p = jnp.where(first, u[2 * i], u[2 * i + 1])
            y_new = y[i] + jnp.where(first, y_h[2 * i], y_h[2 * i + 1])
            v32 = v[i].astype(F32)
            if pad:
                zeros = jnp.zeros((pad, LANE), F32)
                stacked = jnp.concatenate([u_p, zeros, v32, zeros], axis=0)
            else:
                stacked = jnp.concatenate([u_p, v32], axis=0)
            s_new = sb[i] + jnp.where(block_diag, _dot(bk_t[i], stacked), 0.0)
            y_ref[0, :, sls[i]] = y_new
            s_ref[0, sls[i], :] = s_new
        return carry

    lax.fori_loop(0, n_pairs // unroll, group, 0)


def _rwkv_scan(rt, at, bt, kt, v, bkt, gct, s0, chunk):
    b, t, ca = rt.shape
    h = ca // N_A
    nc = t // chunk
    eye2 = jnp.eye(2, dtype=F32)
    s0t = jnp.swapaxes(s0, -1, -2).reshape(b, h // 2, 2, N_A, 1, N_A)
    sb0 = (s0t * eye2[None, None, :, None, :, None]).reshape(b, ca, LANE)
    tmaj = pl.BlockSpec((1, chunk, ca), lambda i, j: (i, j, 0))
    cmaj = pl.BlockSpec((1, 1, ca, LANE), lambda i, j: (i, j, 0, 0))
    st = pl.BlockSpec((1, ca, LANE), lambda i, j: (i, 0, 0))
    y, sb = pl.pallas_call(
        functools.partial(_rwkv_scan_kernel, n_pairs=h // 2, chunk=chunk, unroll=_tile(h // 2, 16, 1)),
        grid=(b, nc),
        in_specs=[tmaj, tmaj, tmaj, tmaj, tmaj, cmaj, cmaj, st],
        out_specs=[tmaj, st],
        out_shape=[jax.ShapeDtypeStruct((b, t, ca), F32), jax.ShapeDtypeStruct((b, ca, LANE), F32)],
        compiler_params=_cparams(2),
        name="rwkv_scan",
    )(at, rt, v, bt, kt, bkt, gct, sb0)
    sb = sb.reshape(b, h // 2, 2, N_A, 2, N_A)
    s_t = jnp.stack([sb[:, :, 0, :, 0, :], sb[:, :, 1, :, 1, :]], axis=2).reshape(b, h, N_A, N_A)
    return y, jnp.swapaxes(s_t, -1, -2)


def _rwkv_post_kernel(y_ref, bonus_ref, g_ref, lnw_ref, lnb_ref, o_ref):
    jmat = _head_ones()
    y = y_ref[...]
    mean = _head_sum(y, jmat) * (1.0 / N_A)
    d = y - mean
    var = _head_sum(d * d, jmat) * (1.0 / N_A)
    out = (d * lax.rsqrt(var + GN_EPS) * lnw_ref[...] + lnb_ref[...] + bonus_ref[...]) * g_ref[...]
    o_ref[...] = out.astype(o_ref.dtype)


def _rwkv_post(y, bonus, g, ln_w, ln_b):
    m, ca = y.shape
    tm = _tile(m, 256, SUBLANE)
    row = pl.BlockSpec((tm, ca), lambda i: (i, 0))
    vec = pl.BlockSpec((1, ca), lambda i: (0, 0))
    return pl.pallas_call(
        _rwkv_post_kernel,
        grid=(m // tm,),
        in_specs=[row, row, row, vec, vec],
        out_specs=row,
        out_shape=jax.ShapeDtypeStruct((m, ca), BF16),
        compiler_params=_cparams(1),
        name="rwkv_post",
    )(y, bonus, g, ln_w, ln_b)


def _pool_kernel(z_ref, hist_ref, w_ref, scale_ref, o_ref, x_scr, *, t, tm, group, pos0):
    halo = 2 * SUBLANE
    x_scr[0:halo, :] = hist_ref[0]
    x_scr[halo:halo + t, :] = z_ref[0]
    for i in range(t // tm):
        r0 = i * tm
        pos = pos0 + r0 + lax.broadcasted_iota(jnp.int32, (tm, 1), 0)
        for gi, win in enumerate(POOL_WINDOWS):
            lo, hi = gi * group, (gi + 1) * group
            cur = x_scr[halo + r0:halo + r0 + tm, lo:hi]
            tot = cur
            for back in range(1, win):
                tot = tot + x_scr[halo + r0 - back:halo + r0 - back + tm, lo:hi]
            cnt = jnp.minimum(pos + 1, win).astype(F32)
            d = tot / cnt - cur
            y = _dot(d, w_ref[gi]) * scale_ref[:, lo:hi]
            o_ref[0, r0:r0 + tm, lo:hi] = y.astype(o_ref.dtype)


def _pool(mid, hist16, pool_w, pool_scale, pos0, dims):
    b, t, _ = mid.shape
    cb = dims["C_B"]
    group = pool_w.shape[1]
    assert len(POOL_WINDOWS) * group == cb
    tm = _tile(t, 256, SUBLANE)
    return pl.pallas_call(
        functools.partial(_pool_kernel, t=t, tm=tm, group=group, pos0=pos0),
        grid=(b,),
        in_specs=[pl.BlockSpec((1, t, cb), lambda i: (i, 0, 0)),
                  pl.BlockSpec((1, 2 * SUBLANE, cb), lambda i: (i, 0, 0)),
                  pl.BlockSpec(pool_w.shape, lambda i: (0, 0, 0)),
                  pl.BlockSpec((1, cb), lambda i: (0, 0))],
        out_specs=pl.BlockSpec((1, t, cb), lambda i: (i, 0, 0)),
        out_shape=jax.ShapeDtypeStruct((b, t, cb), BF16),
        scratch_shapes=[pltpu.VMEM((t + 2 * SUBLANE, cb), F32)],
        compiler_params=_cparams(1),
        name="pool",
    )(mid, hist16, pool_w, pool_scale.reshape(1, cb))


def _monotone_key(x):
    x = jnp.where(x == 0.0, 0.0, x)
    bits = lax.bitcast_convert_type(x, jnp.int32)
    return jnp.where(bits < 0, bits ^ jnp.int32(0x7FFFFFFF), bits)


def _kth_largest_key(key, k):
    def body(i, tau):
        cand = tau + jnp.left_shift(jnp.int32(1), jnp.int32(31) - i)
        cnt = jnp.sum(jnp.where(key >= cand, 1.0, 0.0), axis=-1, keepdims=True)
        return jnp.where(cnt >= k, cand, tau)

    tau0 = jnp.full((key.shape[0], 1), -2 ** 31, jnp.int32)
    return lax.fori_loop(0, 32, body, tau0)


def _prefix_count(ind):
    r = lax.broadcasted_iota(jnp.int32, (LANE, LANE), 0)
    c = lax.broadcasted_iota(jnp.int32, (LANE, LANE), 1)
    tri = jnp.where(r <= c, 1.0, 0.0).astype(BF16)
    run = jnp.zeros((ind.shape[0], 1), F32)
    outs = []
    for j in range(ind.shape[1] // LANE):
        pj = jnp.dot(ind[:, j * LANE:(j + 1) * LANE].astype(BF16), tri, preferred_element_type=F32)
        outs.append(pj + run)
        run = run + pj[:, LANE - 1:LANE]
    return outs[0] if len(outs) == 1 else jnp.concatenate(outs, axis=1)


def _topk_mask(scores, k):
    key = _monotone_key(scores)
    tau = _kth_largest_key(key, float(k))
    gt = key > tau
    eq = key == tau
    need = float(k) - jnp.sum(jnp.where(gt, 1.0, 0.0), axis=-1, keepdims=True)
    prefix = _prefix_count(jnp.where(eq, 1.0, 0.0))
    return gt | (eq & (prefix <= need))


def _dsa_prompt_kernel(q_ref, k_ref, v_ref, qi_ref, ki_ref, wi_ref, o_ref, key_s, sc_s, m_s, l_s, acc_s, *,
                       tq, t, kc, topk, h_idx, d_idx, n_heads, dh):
    q0 = pl.program_id(1) * tq
    n_kc = t // kc
    needed = (q0 + tq + kc - 1) // kc
    qpos = q0 + lax.broadcasted_iota(jnp.int32, (tq, 1), 0)
    qi = [qi_ref[0, :, h * d_idx:(h + 1) * d_idx].astype(BF16) for h in range(h_idx)]
    wi = [wi_ref[0, :, h:h + 1] for h in range(h_idx)]

    def score_chunk(c, carry):
        keys = ki_ref[0, pl.ds(pl.multiple_of(c * kc, kc), kc), :].astype(BF16)
        acc = jnp.zeros((tq, kc), F32)
        for h in range(h_idx):
            acc = acc + jnp.maximum(_dot_nt(qi[h], keys), 0.0) * wi[h]
        spos = c * kc + lax.broadcasted_iota(jnp.int32, (1, kc), 1)
        key_s[c] = _monotone_key(jnp.where(spos <= qpos, acc, -jnp.inf))
        return carry

    lax.fori_loop(0, needed, score_chunk, 0)

    def count(pred):
        def body(c, tot):
            ind = jnp.where(pred(key_s[c]), 1.0, 0.0)
            for j in range(kc // LANE):
                tot = tot + ind[:, j * LANE:(j + 1) * LANE]
            return tot
        lanes = lax.fori_loop(0, needed, body, jnp.zeros((tq, LANE), F32))
        return jnp.sum(lanes, axis=-1, keepdims=True)

    def search(i, tau):
        cand = tau + jnp.left_shift(jnp.int32(1), jnp.int32(31) - i)
        return jnp.where(count(lambda key: key >= cand) >= float(topk), cand, tau)

    tau = lax.fori_loop(0, 32, search, jnp.full((tq, 1), -2 ** 31, jnp.int32))
    need = float(topk) - count(lambda key: key > tau)

    def mask_chunk(c, run):
        key = key_s[c]
        eq = jnp.where(key == tau, 1.0, 0.0)
        prefix = _prefix_count(eq) + run
        spos = c * kc + lax.broadcasted_iota(jnp.int32, (1, kc), 1)
        sel = (key > tau) | ((key == tau) & (prefix <= need))
        sc_s[c] = jnp.where(sel & (spos <= qpos), 1.0, 0.0)
        return prefix[:, kc - 1:kc]

    lax.fori_loop(0, needed, mask_chunk, jnp.zeros((tq, 1), F32))

    m_s[...] = jnp.full(m_s.shape, NEG_BIG, F32)
    l_s[...] = jnp.zeros(l_s.shape, F32)
    acc_s[...] = jnp.zeros(acc_s.shape, F32)
    scale = dh ** -0.5

    def attn_chunk(c, carry):
        rows = pl.ds(pl.multiple_of(c * kc, kc), kc)
        ok = sc_s[c] > 0.5
        heads = range(n_heads)
        sls = [slice(h * dh, (h + 1) * dh) for h in heads]
        lg = [_dot_nt(q_ref[0, :, sls[h]], k_ref[0, rows, sls[h]]) for h in heads]
        lg = [jnp.where(ok, x * scale, NEG_BIG) for x in lg]
        m_old = [m_s[h] for h in heads]
        m_new = [jnp.maximum(m_old[h], jnp.max(lg[h], axis=-1, keepdims=True)) for h in heads]
        p = [jnp.where(ok, jnp.exp(lg[h] - m_new[h]), 0.0) for h in heads]
        pv = [_dot(p[h], v_ref[0, rows, sls[h]]) for h in heads]
        for h in heads:
            alpha = jnp.exp(m_old[h] - m_new[h])
            l_s[h] = alpha * l_s[h] + jnp.sum(p[h], axis=-1, keepdims=True)
            acc_s[:, sls[h]] = alpha * acc_s[:, sls[h]] + pv[h]
            m_s[h] = m_new[h]
        return carry

    lax.fori_loop(0, needed, attn_chunk, 0)
    for h in range(n_heads):
        sl = slice(h * dh, (h + 1) * dh)
        o_ref[0, :, sl] = (acc_s[:, sl] / l_s[h]).astype(o_ref.dtype)


def _dsa_prompt(q, k, v, qi, ki, wi, dims):
    b, t, cc = q.shape
    tq = _tile(t, 128, SUBLANE)
    kc = _tile(t, 512, tq)
    topk = min(TOPK_MAX, t // 4)
    assert kc >= topk
    h_idx, d_idx, n_heads = dims["H_IDX"], dims["D_IDX"], dims["H_C"]
    qb = lambda w: pl.BlockSpec((1, tq, w), lambda i, j: (i, j, 0))
    full = lambda w: pl.BlockSpec((1, t, w), lambda i, j: (i, 0, 0))
    return pl.pallas_call(
        functools.partial(_dsa_prompt_kernel, tq=tq, t=t, kc=kc, topk=topk, h_idx=h_idx, d_idx=d_idx,
                          n_heads=n_heads, dh=dims["DH_C"]),
        grid=(b, t // tq),
        in_specs=[qb(cc), full(cc), full(cc), qb(h_idx * d_idx), full(d_idx), qb(h_idx)],
        out_specs=qb(cc),
        out_shape=jax.ShapeDtypeStruct((b, t, cc), BF16),
        scratch_shapes=[pltpu.VMEM((t // kc, tq, kc), jnp.int32), pltpu.VMEM((t // kc, tq, kc), F32),
                        pltpu.VMEM((n_heads, tq, 1), F32),
                        pltpu.VMEM((n_heads, tq, 1), F32), pltpu.VMEM((tq, cc), F32)],
        compiler_params=_cparams(2),
        name="dsa_prompt",
    )(q, k, v, qi, ki, wi)


def _idx_rows_scores(qi2, wi2, keys, tq, h_idx):
    s = jnp.maximum(_dot_nt(qi2, keys), 0.0) * wi2
    acc = s[0:tq]
    for h in range(1, h_idx):
        acc = acc + s[h * tq:(h + 1) * tq]
    return acc


def _dsa_sample_scores_kernel(pt_ref, qi_ref, wi_ref, *refs, tq, h_idx):
    o_ref = refs[-1]
    for g, kc_ref in enumerate(refs[:-1]):
        o_ref[0, :, g * PAGE:(g + 1) * PAGE] = _idx_rows_scores(qi_ref[0], wi_ref[0], kc_ref[0, 0], tq, h_idx)


def _dsa_sample_scores(page_table, qi2, wi2, cache_kidx, l, tq, h_idx):
    b, n_pages = page_table.shape
    d_idx = cache_kidx.shape[-1]
    rows = qi2.shape[1]
    group = _tile(n_pages, 16, 1)
    page = lambda g: pl.BlockSpec((1, 1, PAGE, d_idx), lambda i, p, pt: (l, pt[i, p * group + g], 0, 0))
    grid_spec = pltpu.PrefetchScalarGridSpec(
        num_scalar_prefetch=1,
        grid=(b, n_pages // group),
        in_specs=[pl.BlockSpec((1, rows, d_idx), lambda i, p, pt: (i, 0, 0)),
                  pl.BlockSpec((1, rows, 1), lambda i, p, pt: (i, 0, 0))] + [page(g) for g in range(group)],
        out_specs=pl.BlockSpec((1, tq, group * PAGE), lambda i, p, pt: (i, 0, p)),
    )
    return pl.pallas_call(
        functools.partial(_dsa_sample_scores_kernel, tq=tq, h_idx=h_idx),
        grid_spec=grid_spec,
        out_shape=jax.ShapeDtypeStruct((b, tq, n_pages * PAGE), F32),
        compiler_params=_cparams(2),
        name="dsa_sample_scores",
    )(page_table, qi2, wi2, *([cache_kidx] * group))


def _dsa_sample_select_kernel(sc_ref, qi_ref, wi_ref, kn_ref, o_ref, on_ref, *, tq, h_idx, topk, past):
    new = _idx_rows_scores(qi_ref[0], wi_ref[0], kn_ref[0], tq, h_idx)
    qrow = lax.broadcasted_iota(jnp.int32, (tq, PAGE), 0)
    col = lax.broadcasted_iota(jnp.int32, (tq, PAGE), 1)
    new_ok = col <= qrow
    scores = jnp.concatenate([sc_ref[0], jnp.where(new_ok, new, -jnp.inf)], axis=1)
    sel = jnp.where(_topk_mask(scores, topk), 1.0, 0.0)
    o_ref[0] = sel[:, :past]
    on_ref[0] = jnp.where(new_ok, sel[:, past:], 0.0)


def _dsa_sample_select(scores, qi2, wi2, ki_new_pad, tq, h_idx):
    b, _, past = scores.shape
    topk = min(TOPK_MAX, (past + tq) // 4)
    rows, d_idx = qi2.shape[1], qi2.shape[2]
    return pl.pallas_call(
        functools.partial(_dsa_sample_select_kernel, tq=tq, h_idx=h_idx, topk=topk, past=past),
        grid=(b,),
        in_specs=[pl.BlockSpec((1, tq, past), lambda i: (i, 0, 0)),
                  pl.BlockSpec((1, rows, d_idx), lambda i: (i, 0, 0)),
                  pl.BlockSpec((1, rows, 1), lambda i: (i, 0, 0)),
                  pl.BlockSpec((1, PAGE, d_idx), lambda i: (i, 0, 0))],
        out_specs=[pl.BlockSpec((1, tq, past), lambda i: (i, 0, 0)), pl.BlockSpec((1, tq, PAGE), lambda i: (i, 0, 0))],
        out_shape=[jax.ShapeDtypeStruct((b, tq, past), F32), jax.ShapeDtypeStruct((b, tq, PAGE), F32)],
        compiler_params=_cparams(1),
        name="dsa_sample_select",
    )(scores, qi2, wi2, ki_new_pad)


def _dsa_sample_attn_kernel(pt_ref, q_ref, kn_ref, vn_ref, m_ref, mn_ref, *refs, n_steps, group, n_heads, dh, tq):
    kc_refs, vc_refs = refs[:group], refs[group:2 * group]
    o_ref, q2_s, exp_s, hm_s, m_s, l_s, acc_s = refs[2 * group:]
    p = pl.program_id(1)
    rows = n_heads * tq
    cols = PAGE * n_heads

    @pl.when(p == 0)
    def _():
        for h in range(n_heads):
            q2_s[h * tq:(h + 1) * tq, :] = q_ref[0, :, h * dh:(h + 1) * dh]
        pos = lax.broadcasted_iota(jnp.int32, (PAGE, cols), 0)
        col = lax.broadcasted_iota(jnp.int32, (PAGE, cols), 1)
        exp_s[...] = jnp.where(col // n_heads == pos, 1.0, 0.0).astype(exp_s.dtype)
        rr = lax.broadcasted_iota(jnp.int32, (rows, cols), 0)
        cc = lax.broadcasted_iota(jnp.int32, (rows, cols), 1)
        hm_s[...] = jnp.where((cc % n_heads) == (rr // tq), 1.0, 0.0)
        m_s[...] = jnp.full(m_s.shape, NEG_BIG, F32)
        l_s[...] = jnp.zeros(l_s.shape, F32)
        acc_s[...] = jnp.zeros(acc_s.shape, F32)

    def update(kmat, vmat, sel):
        lg = _dot_nt(q2_s[...], kmat) * (dh ** -0.5)
        sel8 = jnp.dot(sel.astype(BF16), exp_s[...], preferred_element_type=F32)
        msk = jnp.concatenate([sel8] * n_heads, axis=0) * hm_s[...] > 0.5
        lgm = jnp.where(msk, lg, NEG_BIG)
        m_new = jnp.maximum(m_s[...], jnp.max(lgm, axis=-1, keepdims=True))
        alpha = jnp.exp(m_s[...] - m_new)
        pm = jnp.where(msk, jnp.exp(lgm - m_new), 0.0)
        l_s[...] = alpha * l_s[...] + jnp.sum(pm, axis=-1, keepdims=True)
        acc_s[...] = alpha * acc_s[...] + _dot(pm, vmat)
        m_s[...] = m_new

    @pl.when(p < n_steps)
    def _():
        for g in range(group):
            update(kc_refs[g][0, 0].reshape(cols, dh), vc_refs[g][0, 0].reshape(cols, dh),
                   m_ref[0, :, g * PAGE:(g + 1) * PAGE])

    @pl.when(p == n_steps)
    def _():
        update(kn_ref[0].reshape(cols, dh), vn_ref[0].reshape(cols, dh), mn_ref[0])
        res = acc_s[...] / l_s[...]
        for h in range(n_heads):
            o_ref[0, :, h * dh:(h + 1) * dh] = res[h * tq:(h + 1) * tq, :].astype(o_ref.dtype)


def _dsa_sample_attn(page_table, q, cache_k, cache_v, k_new_pad, v_new_pad, mask, mask_new, l, dims):
    b, n_pages = page_table.shape
    tq = q.shape[1]
    n_heads, dh = dims["H_C"], dims["DH_C"]
    cc = n_heads * dh
    group = _tile(n_pages, 8, 1)
    n_steps = n_pages // group
    page = lambda g: pl.BlockSpec(
        (1, 1, PAGE, n_heads, dh),
        lambda i, p, pt: (l, pt[i, jnp.minimum(p * group + g, n_pages - 1)], 0, 0, 0))
    new = pl.BlockSpec((1, PAGE, n_heads, dh), lambda i, p, pt: (i, 0, 0, 0))
    grid_spec = pltpu.PrefetchScalarGridSpec(
        num_scalar_prefetch=1,
        grid=(b, n_steps + 1),
        in_specs=[pl.BlockSpec((1, tq, cc), lambda i, p, pt: (i, 0, 0)), new, new,
                  pl.BlockSpec((1, tq, group * PAGE), lambda i, p, pt: (i, 0, jnp.minimum(p, n_steps - 1))),
                  pl.BlockSpec((1, tq, PAGE), lambda i, p, pt: (i, 0, 0))]
                 + [page(g) for g in range(group)] * 2,
        out_specs=pl.BlockSpec((1, tq, cc), lambda i, p, pt: (i, 0, 0)),
        scratch_shapes=[pltpu.VMEM((n_heads * tq, dh), BF16), pltpu.VMEM((PAGE, PAGE * n_heads), BF16),
                        pltpu.VMEM((n_heads * tq, PAGE * n_heads), F32), pltpu.VMEM((n_heads * tq, 1), F32),
                        pltpu.VMEM((n_heads * tq, 1), F32), pltpu.VMEM((n_heads * tq, dh), F32)],
    )
    return pl.pallas_call(
        functools.partial(_dsa_sample_attn_kernel, n_steps=n_steps, group=group, n_heads=n_heads, dh=dh, tq=tq),
        grid_spec=grid_spec,
        out_shape=jax.ShapeDtypeStruct((b, tq, cc), BF16),
        compiler_params=_cparams(2),
        name="dsa_sample_attn",
    )(page_table, q, k_new_pad, v_new_pad, mask, mask_new, *([cache_k] * group), *([cache_v] * group))


def _dsa_sample(q, k_f32, v_f32, qi, ki, wi, cache_k, cache_v, cache_kidx, page_table, l, dims):
    b, tq, cc = q.shape
    h_idx, d_idx, n_heads, dh = dims["H_IDX"], dims["D_IDX"], dims["H_C"], dims["DH_C"]
    qi2 = jnp.transpose(qi.reshape(b, tq, h_idx, d_idx), (0, 2, 1, 3)).reshape(b, h_idx * tq, d_idx)
    wi2 = jnp.transpose(wi, (0, 2, 1)).reshape(b, h_idx * tq, 1)
    pad_rows = lambda x: jnp.pad(x, ((0, 0), (0, PAGE - tq)) + ((0, 0),) * (x.ndim - 2))
    scores = _dsa_sample_scores(page_table, qi2, wi2, cache_kidx, l, tq, h_idx)
    mask, mask_new = _dsa_sample_select(scores, qi2, wi2, pad_rows(ki), tq, h_idx)
    return _dsa_sample_attn(page_table, q, cache_k, cache_v, pad_rows(k_f32), pad_rows(v_f32), mask, mask_new, l,
                            dims)


def _prepare_weights(raw, dims):
    ca, cb, cc, qiw, d_idx, h_idx, d = (dims[k] for k in ("C_A", "C_B", "C_C", "QIW", "D_IDX", "H_IDX", "D"))
    dd, da, dg = dims["D_DECAY"], dims["D_AAA"], dims["D_GATE"]
    assert dd <= LANE and da <= LANE and dg % LANE == 0
    a_cols = 3 * ca + dd + da + dg
    o = [0, a_cols, a_cols + cb, a_cols + cb + cc, a_cols + cb + 2 * cc, a_cols + cb + 3 * cc]
    o += [o[-1] + qiw, o[-1] + qiw + d_idx, o[-1] + qiw + d_idx + h_idx]
    w_in = raw["w_in"]

    def pack_pa(x):
        zeros = lambda n: jnp.zeros(x.shape[:-1] + (n,), x.dtype)
        return jnp.concatenate([x[..., :3 * ca], x[..., 3 * ca:3 * ca + dd], zeros(LANE - dd),
                                x[..., 3 * ca + dd:3 * ca + dd + da], zeros(LANE - da),
                                x[..., 3 * ca + dd + da:a_cols]], axis=-1)

    kiw = LANE - d_idx - h_idx
    assert kiw >= 0
    w_pa = pack_pa(w_in[..., :a_cols]).astype(BF16)
    w_mid = jnp.concatenate([w_in[..., o[1]:o[2]], w_in[..., o[4]:o[5]], w_in[..., o[2]:o[4]],
                             w_in[..., o[5]:o[8]], jnp.zeros(w_in.shape[:-1] + (kiw,), w_in.dtype)],
                            axis=-1).astype(BF16)
    w_gl = w_in[..., o[8]:].astype(BF16)
    pad_rows = lambda x: jnp.pad(x, ((0, 0), (0, LANE - x.shape[1]), (0, 0)))
    depth = w_in.shape[0]
    vec = lambda x: x.reshape(depth, 1, -1)
    P = dict(
        w_pa=w_pa, w_mid=w_mid, w_gl=w_gl,
        mu=vec(pack_pa(raw["rwkv_mu"])), w0=vec(raw["rwkv_w0"]), w2=pad_rows(raw["rwkv_w2"]).astype(BF16),
        a0=vec(raw["rwkv_a0"]), a2=pad_rows(raw["rwkv_a2"]).astype(BF16), g2=raw["rwkv_g2"].astype(BF16),
        k_k=vec(raw["rwkv_k_k"]), k_a=vec(raw["rwkv_k_a"]), r_k=vec(raw["rwkv_r_k"]),
        ln_w=vec(raw["rwkv_ln_w"]), ln_b=vec(raw["rwkv_ln_b"]),
        pool_w=raw["pool_w"].astype(BF16), pool_scale=raw["pool_scale"],
    )
    for name in ("ffn1_w_down", "ffn2_w_down", "w_br_a", "w_br_b", "w_br_c", "w_out"):
        P[name] = raw[name].astype(BF16)
    for name in ("ffn1_w_gate", "ffn1_w_up", "ffn2_w_gate", "ffn2_w_up", "ffn1_norm", "mix_norm", "ffn2_norm"):
        P[name] = raw[name]
    P["pack_pa"] = pack_pa
    return P


def _unpack_pa(x, dims):
    ca, dd, da = dims["C_A"], dims["D_DECAY"], dims["D_AAA"]
    return jnp.concatenate([x[..., :3 * ca], x[..., 3 * ca:3 * ca + dd],
                            x[..., 3 * ca + LANE:3 * ca + LANE + da], x[..., 3 * ca + 2 * LANE:]], axis=-1)


def _layer(carry, b, t, P, l, next_g, pos0, shift_prev, s0, pool_hist, dims, sample_ctx):
    h, hb, ssq = carry
    m, d = h.shape
    ca, cb, cc = dims["C_A"], dims["C_B"], dims["C_C"]
    h, u, ssq = _ffn(h, hb, ssq, P["ffn1_w_gate"], P["ffn1_w_up"], P["ffn1_w_down"], l, P["mix_norm"][l])
    pa = _mm(u, P["w_pa"], l, out_dtype=F32, ssq=ssq, name="proj_pa").reshape(b, t, -1)
    mid = _mm(u, P["w_mid"], l, out_dtype=F32, ssq=ssq, name="proj_mid").reshape(b, t, -1)
    gates = _mm(u, P["w_gl"], l, out_dtype=BF16, ssq=ssq, act="sigmoid", name="proj_gate")

    chunk = min(N_A, t)
    rt, at, bt, kt, v_a, bkt, gct, bonus, g = _rwkv_prep(pa, P["pack_pa"](shift_prev), P, l, dims, chunk)
    y_a, new_wkv = _rwkv_scan(rt, at, bt, kt, v_a, bkt, gct, s0, chunk)
    ya = _rwkv_post(y_a.reshape(m, ca), bonus.reshape(m, ca), g.reshape(m, ca), P["ln_w"][l], P["ln_b"][l])
    new_shift = _unpack_pa(pa[:, -1], dims)

    hist16 = jnp.pad(pool_hist, ((0, 0), (2 * SUBLANE - POOL_HIST, 0), (0, 0)))
    yb = _pool(mid, hist16, P["pool_w"][l], P["pool_scale"][l], pos0, dims)
    zb = mid[..., :cb]
    new_pool = jnp.concatenate([pool_hist, zb], axis=1)[:, -POOL_HIST:]

    q, k, v, k_bf, v_bf, qi, kiwi = _rope_all(mid, pos0, dims)
    ki = kiwi[..., :dims["D_IDX"]]
    wi = kiwi[..., dims["D_IDX"]:dims["D_IDX"] + dims["H_IDX"]]
    if sample_ctx is None:
        yc = _dsa_prompt(q, k_bf, v_bf, qi, ki, wi, dims)
    else:
        cache_k, cache_v, cache_kidx, page_table = sample_ctx
        yc = _dsa_sample(q, k, v, qi, ki, wi, cache_k, cache_v, cache_kidx, page_table, l, dims)

    merged = _merge(ya, yb.reshape(m, cb), yc.reshape(m, cc), P["w_br_a"], P["w_br_b"], P["w_br_c"], l, gates)
    h, hb, ssq = _mm(merged, P["w_out"], l, out_dtype=F32, res=h, next_g=P["ffn2_norm"][l], tn_target=512,
                     name="w_out")
    carry = _ffn(h, hb, ssq, P["ffn2_w_gate"], P["ffn2_w_up"], P["ffn2_w_down"], l, next_g)
    return carry, (k, v, ki, new_wkv, new_shift, new_pool)


def kernel(x_prompt, x_sample, cache_k, cache_v, cache_kidx, state_wkv, state_shift, state_pool, page_table,
           ffn1_norm, ffn1_w_gate, ffn1_w_up, ffn1_w_down, mix_norm, w_in, rwkv_mu, rwkv_w0, rwkv_w2,
           rwkv_a0, rwkv_a2, rwkv_g2, rwkv_k_k, rwkv_k_a, rwkv_r_k, rwkv_ln_w, rwkv_ln_b, w_br_a, pool_w,
           pool_scale, w_br_b, w_br_c, w_out, ffn2_norm, ffn2_w_gate, ffn2_w_up, ffn2_w_down, final_norm):
    raw = dict(ffn1_norm=ffn1_norm, ffn1_w_gate=ffn1_w_gate, ffn1_w_up=ffn1_w_up, ffn1_w_down=ffn1_w_down,
               mix_norm=mix_norm, w_in=w_in, rwkv_mu=rwkv_mu, rwkv_w0=rwkv_w0, rwkv_w2=rwkv_w2, rwkv_a0=rwkv_a0,
               rwkv_a2=rwkv_a2, rwkv_g2=rwkv_g2, rwkv_k_k=rwkv_k_k, rwkv_k_a=rwkv_k_a, rwkv_r_k=rwkv_r_k,
               rwkv_ln_w=rwkv_ln_w, rwkv_ln_b=rwkv_ln_b, w_br_a=w_br_a, pool_w=pool_w, pool_scale=pool_scale,
               w_br_b=w_br_b, w_br_c=w_br_c, w_out=w_out, ffn2_norm=ffn2_norm, ffn2_w_gate=ffn2_w_gate,
               ffn2_w_up=ffn2_w_up, ffn2_w_down=ffn2_w_down)
    depth, d = mix_norm.shape
    ca = rwkv_w0.shape[-1]
    cb = pool_scale.shape[-1]
    n_heads, dh = cache_k.shape[3], cache_k.shape[4]
    cc = n_heads * dh
    d_idx = cache_kidx.shape[-1]
    dd, da, dg = rwkv_w2.shape[1], rwkv_a2.shape[1], rwkv_g2.shape[1]
    a_cols = 3 * ca + dd + da + dg
    h_idx = (w_in.shape[-1] - a_cols - cb - 3 * cc - d_idx - 3 * d) // (d_idx + 1)
    dims = dict(D=d, C_A=ca, C_B=cb, C_C=cc, H_C=n_heads, DH_C=dh, D_IDX=d_idx, H_IDX=h_idx,
                QIW=h_idx * d_idx, D_DECAY=dd, D_AAA=da, D_GATE=dg)
    assert w_in.shape[-1] == a_cols + cb + 3 * cc + h_idx * d_idx + d_idx + h_idx + 3 * d
    P = _prepare_weights(raw, dims)

    bp = x_prompt.shape[0]
    past = page_table.shape[1] * PAGE
    bs, ts = x_sample.shape[:2]
    tp = x_prompt.shape[1]

    def start(x):
        x2d = x.reshape(-1, d)
        return (x2d,) + tuple(_prenorm(x2d, ffn1_norm[0]))

    cp, cs = start(x_prompt), start(x_sample)
    st_p, st_s = [], []
    for l in range(depth):
        next_g = ffn1_norm[l + 1] if l + 1 < depth else final_norm
        zero_shift = jnp.zeros((bp, a_cols), F32)
        zero_wkv = jnp.zeros((bp, ca // N_A, N_A, N_A), F32)
        zero_pool = jnp.zeros((bp, POOL_HIST, cb), F32)
        cp, sp = _layer(cp, bp, tp, P, l, next_g, 0, zero_shift, zero_wkv, zero_pool, dims, None)
        cs, ss = _layer(cs, bs, ts, P, l, next_g, past, state_shift[l], state_wkv[l], state_pool[l], dims,
                        (cache_k, cache_v, cache_kidx, page_table))
        st_p.append(sp)
        st_s.append(ss)
    fin = lambda c, shape: _rmsnorm(c[0], final_norm, F32).reshape(shape)
    stk = lambda sts, i: jnp.stack([s[i] for s in sts])
    return (fin(cp, x_prompt.shape), fin(cs, x_sample.shape),
            stk(st_p, 0), stk(st_p, 1), stk(st_p, 2), stk(st_p, 3), stk(st_p, 4), stk(st_p, 5),
            stk(st_s, 0), stk(st_s, 1), stk(st_s, 2), stk(st_s, 3), stk(st_s, 4), stk(st_s, 5))
```

```python
import functools
import math

import jax
import jax.numpy as jnp
from jax import lax
from jax.experimental import pallas as pl
from jax.experimental.pallas import tpu as pltpu

F32 = jnp.float32
BF16 = jnp.bfloat16

LANE = 128
SUBLANE = 8
VMEM_LIMIT_BYTES = 56 * 1024 * 1024
ROW_TILE = 1400

N_A = 64
GN_EPS = 64e-5
POOL_WINDOWS = (2, 4, 8, 16)
POOL_HIST = max(POOL_WINDOWS) - 1
TOPK_MAX = 256
ROPE_THETA = 10000.0
RMS_EPS = 1e-6
PAGE = 128
NEG_BIG = -1e30


def _cparams(n_axes):
    return pltpu.CompilerParams(dimension_semantics=("arbitrary",) * n_axes,
                                vmem_limit_bytes=VMEM_LIMIT_BYTES)


def _tile(n, target, mult):
    best = None
    for t in range(mult, min(n, target) + 1, mult):
        if n % t == 0:
            best = t
    return best if best is not None else n


def _roundup(n, m):
    return -(-n // m) * m


def _dot(a, b):
    return jnp.dot(a.astype(BF16), b.astype(BF16), preferred_element_type=F32)


def _dot_nt(a, b):
    return lax.dot_general(a.astype(BF16), b.astype(BF16), (((1,), (1,)), ((), ())),
                           preferred_element_type=F32)


def _split3(x):
    h1 = x.astype(BF16)
    r1 = x - h1.astype(F32)
    h2 = r1.astype(BF16)
    h3 = (r1 - h2.astype(F32)).astype(BF16)
    return h1, h2, h3


def _dot_exact_rhs(x, m_bf16):
    h1, h2, h3 = _split3(x)
    return (jnp.dot(h1, m_bf16, preferred_element_type=F32)
            + jnp.dot(h2, m_bf16, preferred_element_type=F32)
            + jnp.dot(h3, m_bf16, preferred_element_type=F32))


def _dot_exact_lhs(m_bf16, x):
    h1, h2, h3 = _split3(x)
    return (jnp.dot(m_bf16, h1, preferred_element_type=F32)
            + jnp.dot(m_bf16, h2, preferred_element_type=F32)
            + jnp.dot(m_bf16, h3, preferred_element_type=F32))


def _rmsnorm_kernel(x_ref, g_ref, o_ref):
    x = x_ref[...]
    ms = jnp.mean(x * x, axis=-1, keepdims=True)
    o_ref[...] = (x * lax.rsqrt(ms + RMS_EPS) * g_ref[...]).astype(o_ref.dtype)


def _rmsnorm(x2d, row0, rows, g, out_dtype):
    d = x2d.shape[1]
    tm = _tile(rows, 256, SUBLANE)
    assert row0 % tm == 0
    return pl.pallas_call(
        _rmsnorm_kernel,
        grid=(rows // tm,),
        in_specs=[pl.BlockSpec((tm, d), lambda i: (row0 // tm + i, 0)), pl.BlockSpec((1, d), lambda i: (0, 0))],
        out_specs=pl.BlockSpec((tm, d), lambda i: (i, 0)),
        out_shape=jax.ShapeDtypeStruct((rows, d), out_dtype),
        compiler_params=_cparams(1),
        name="rmsnorm",
    )(x2d, g.reshape(1, d))


def _prenorm_kernel(x_ref, g_ref, xb_ref, ssq_ref):
    x = x_ref[...]
    xb_ref[...] = (x * g_ref[...]).astype(xb_ref.dtype)
    ssq_ref[...] = jnp.sum(x * x, axis=-1, keepdims=True)


def _prenorm(x2d, g):
    m, d = x2d.shape
    tm = _tile(m, 256, SUBLANE)
    return pl.pallas_call(
        _prenorm_kernel,
        grid=(m // tm,),
        in_specs=[pl.BlockSpec((tm, d), lambda i: (i, 0)), pl.BlockSpec((1, d), lambda i: (0, 0))],
        out_specs=[pl.BlockSpec((tm, d), lambda i: (i, 0)), pl.BlockSpec((tm, 1), lambda i: (i, 0))],
        out_shape=[jax.ShapeDtypeStruct((m, d), BF16), jax.ShapeDtypeStruct((m, 1), F32)],
        compiler_params=_cparams(1),
        name="prenorm",
    )(x2d, g.reshape(1, d))


def _row_scale(ssq, width):
    return lax.rsqrt(ssq * (1.0 / width) + RMS_EPS)


def _mm_kernel(*refs, nk, kdim, has_ssq, has_res, has_next, res_scale, act):
    it = iter(refs)
    x_ref, w_ref = next(it), next(it)
    ssq_ref = next(it) if has_ssq else None
    res_ref = next(it) if has_res else None
    gnext_ref = next(it) if has_next else None
    o_ref = next(it)
    ob_ref, ssqo_ref = (next(it), next(it)) if has_next else (None, None)

    def epilogue(acc):
        if has_ssq:
            acc = acc * _row_scale(ssq_ref[...], kdim)
        if act == "sigmoid":
            acc = jax.nn.sigmoid(acc)
        if has_res:
            acc = res_ref[...] + res_scale * acc
        o_ref[...] = acc.astype(o_ref.dtype)
        if has_next:
            ob_ref[...] = (acc * gnext_ref[...]).astype(ob_ref.dtype)
            part_ssq = jnp.sum(acc * acc, axis=-1, keepdims=True)
            j = pl.program_id(1)

            @pl.when(j == 0)
            def _():
                ssqo_ref[...] = part_ssq

            @pl.when(j > 0)
            def _():
                ssqo_ref[...] += part_ssq

    part = jnp.dot(x_ref[...], w_ref[0], preferred_element_type=F32)
    if nk == 1:
        epilogue(part)
    else:
        acc_ref = next(it)
        k = pl.program_id(2)

        @pl.when(k == 0)
        def _():
            acc_ref[...] = part

        @pl.when(k > 0)
        def _():
            acc_ref[...] += part

        @pl.when(k == nk - 1)
        def _():
            epilogue(acc_ref[...])


def _mm(x, w, l, *, out_dtype, ssq=None, res=None, res_scale=1.0, act=None, next_g=None, tm_target=ROW_TILE,
        tn_target=1024, tk_target=4096, name="mm"):
    m, kdim = x.shape
    n = w.shape[2]
    tm = _tile(m, tm_target, 2 * SUBLANE)
    tn = _tile(n, tn_target, LANE)
    tk = _tile(kdim, tk_target, LANE)
    nk = kdim // tk
    tile = pl.BlockSpec((tm, tn), lambda i, j, k: (i, j))
    col = pl.BlockSpec((tm, 1), lambda i, j, k: (i, 0))
    in_specs = [pl.BlockSpec((tm, tk), lambda i, j, k: (i, k)),
                pl.BlockSpec((1, tk, tn), lambda i, j, k: (l, k, j))]
    args = [x, w]
    if ssq is not None:
        in_specs.append(col)
        args.append(ssq)
    if res is not None:
        in_specs.append(tile)
        args.append(res)
    out_specs, out_shape = tile, jax.ShapeDtypeStruct((m, n), out_dtype)
    if next_g is not None:
        assert res is not None
        in_specs.append(pl.BlockSpec((1, tn), lambda i, j, k: (0, j)))
        args.append(next_g.reshape(1, n))
        out_specs = [tile, tile, col]
        out_shape = [out_shape, jax.ShapeDtypeStruct((m, n), BF16), jax.ShapeDtypeStruct((m, 1), F32)]
    scratch = [pltpu.VMEM((tm, tn), F32)] if nk > 1 else []
    return pl.pallas_call(
        functools.partial(_mm_kernel, nk=nk, kdim=kdim, has_ssq=ssq is not None, has_res=res is not None,
                          has_next=next_g is not None, res_scale=res_scale, act=act),
        grid=(m // tm, n // tn, nk),
        in_specs=in_specs,
        out_specs=out_specs,
        out_shape=out_shape,
        scratch_shapes=scratch,
        compiler_params=_cparams(3),
        name=name,
    )(*args)


def _swiglu_up_kernel(x_ref, ssq_ref, wg_ref, wu_ref, o_ref, *, d):
    x = x_ref[...]
    scale = _row_scale(ssq_ref[...], d)
    g = jnp.dot(x, wg_ref[0].astype(BF16), preferred_element_type=F32) * scale
    u = jnp.dot(x, wu_ref[0].astype(BF16), preferred_element_type=F32) * scale
    o_ref[...] = (g * jax.nn.sigmoid(g) * u).astype(o_ref.dtype)


def _swiglu_up(x, ssq, wg, wu, l):
    m, d = x.shape
    f = wg.shape[2]
    tm = _tile(m, ROW_TILE, 2 * SUBLANE)
    tn = _tile(f, 512, LANE)
    return pl.pallas_call(
        functools.partial(_swiglu_up_kernel, d=d),
        grid=(m // tm, f // tn),
        in_specs=[pl.BlockSpec((tm, d), lambda i, j: (i, 0)),
                  pl.BlockSpec((tm, 1), lambda i, j: (i, 0)),
                  pl.BlockSpec((1, d, tn), lambda i, j: (l, 0, j)),
                  pl.BlockSpec((1, d, tn), lambda i, j: (l, 0, j))],
        out_specs=pl.BlockSpec((tm, tn), lambda i, j: (i, j)),
        out_shape=jax.ShapeDtypeStruct((m, f), BF16),
        compiler_params=_cparams(2),
        name="swiglu_up",
    )(x, ssq, wg, wu)


def _ffn_down_kernel(a_ref, w_ref, res_ref, gnext_ref, o_ref, ob_ref, ssq_ref, acc_ref, *, nk, res_scale):
    k = pl.program_id(1)
    j = pl.program_id(2)
    part = jnp.dot(a_ref[...], w_ref[0], preferred_element_type=F32)

    if nk > 1:
        @pl.when(k == 0)
        def _():
            acc_ref[j] = part

        @pl.when((k > 0) & (k < nk - 1))
        def _():
            acc_ref[j] += part

    @pl.when(k == nk - 1)
    def _():
        acc = acc_ref[j] + part if nk > 1 else part
        h = res_ref[...] + res_scale * acc
        o_ref[...] = h
        ob_ref[...] = (h * gnext_ref[...]).astype(ob_ref.dtype)
        part_ssq = jnp.sum(h * h, axis=-1, keepdims=True)

        @pl.when(j == 0)
        def _():
            ssq_ref[...] = part_ssq

        @pl.when(j > 0)
        def _():
            ssq_ref[...] += part_ssq


def _ffn_down(a, w, l, res, res_scale, next_g):
    m, kdim = a.shape
    n = w.shape[2]
    tm = _tile(m, 1024, 2 * SUBLANE)
    tn = _tile(n, 512 if tm <= 768 else 256, LANE)
    tk = _tile(kdim, 5632, LANE)
    nk, nn = kdim // tk, n // tn
    last = lambda k, j: jnp.where(k == nk - 1, j, 0)
    tile = pl.BlockSpec((tm, tn), lambda i, k, j: (i, last(k, j)))
    col = pl.BlockSpec((tm, 1), lambda i, k, j: (i, 0))
    return pl.pallas_call(
        functools.partial(_ffn_down_kernel, nk=nk, res_scale=res_scale),
        grid=(m // tm, nk, nn),
        in_specs=[pl.BlockSpec((tm, tk), lambda i, k, j: (i, k)),
                  pl.BlockSpec((1, tk, tn), lambda i, k, j: (l, k, j)),
                  tile,
                  pl.BlockSpec((1, tn), lambda i, k, j: (0, last(k, j)))],
        out_specs=[tile, tile, col],
        out_shape=[jax.ShapeDtypeStruct((m, n), F32), jax.ShapeDtypeStruct((m, n), BF16),
                   jax.ShapeDtypeStruct((m, 1), F32)],
        scratch_shapes=[pltpu.VMEM((nn, tm, tn), F32)],
        compiler_params=_cparams(3),
        name="ffn_down",
    )(a, w, res, next_g.reshape(1, n))


def _ffn(h, hb, ssq, wg, wu, wd, l, next_g):
    a = _swiglu_up(hb, ssq, wg, wu, l)
    return _ffn_down(a, wd, l, h, 0.5, next_g)


def _merge_kernel(ya_ref, yb_ref, yc_ref, wa_ref, wb_ref, wc_ref, ga_ref, gb_ref, gc_ref, o_ref):
    a = jnp.dot(ya_ref[...], wa_ref[0], preferred_element_type=F32)
    b = jnp.dot(yb_ref[...], wb_ref[0], preferred_element_type=F32)
    c = jnp.dot(yc_ref[...], wc_ref[0], preferred_element_type=F32)
    out = ga_ref[...].astype(F32) * a + gb_ref[...].astype(F32) * b + gc_ref[...].astype(F32) * c
    o_ref[...] = out.astype(o_ref.dtype)


def _merge(ya, yb, yc, wa, wb, wc, l, gates):
    m = ya.shape[0]
    d = wa.shape[2]
    tm = _tile(m, ROW_TILE, 2 * SUBLANE)
    tn = _tile(d, 512, LANE)
    nb = d // tn
    wspec = lambda w: pl.BlockSpec((1, w.shape[1], tn), lambda i, j: (l, 0, j))
    return pl.pallas_call(
        _merge_kernel,
        grid=(m // tm, nb),
        in_specs=[pl.BlockSpec((tm, ya.shape[1]), lambda i, j: (i, 0)),
                  pl.BlockSpec((tm, yb.shape[1]), lambda i, j: (i, 0)),
                  pl.BlockSpec((tm, yc.shape[1]), lambda i, j: (i, 0)),
                  wspec(wa), wspec(wb), wspec(wc),
                  pl.BlockSpec((tm, tn), lambda i, j: (i, j)),
                  pl.BlockSpec((tm, tn), lambda i, j: (i, nb + j)),
                  pl.BlockSpec((tm, tn), lambda i, j: (i, 2 * nb + j))],
        out_specs=pl.BlockSpec((tm, tn), lambda i, j: (i, j)),
        out_shape=jax.ShapeDtypeStruct((m, d), BF16),
        compiler_params=_cparams(2),
        name="merge",
    )(ya, yb, yc, wa, wb, wc, gates, gates, gates)


def _rope_kernel(q_ref, k_ref, v_ref, qi_ref, kw_ref, c128_ref, s128_ref, c64_ref, s64_ref,
                 qo_ref, ko_ref, vo_ref, kb_ref, vb_ref, qio_ref, kwo_ref, *, n_heads, n_idx_groups, d_idx,
                 idx_scale):
    c128, s128 = c128_ref[...], s128_ref[...]
    c64, s64 = c64_ref[...], s64_ref[...]
    lane = lax.broadcasted_iota(jnp.int32, c64.shape, 1)
    first_half = (lane % d_idx) < (d_idx // 2)

    def rope128(x):
        return x * c128 + pltpu.roll(x, LANE // 2, axis=1) * s128

    def rope64(x):
        rot = jnp.where(first_half, pltpu.roll(x, LANE - d_idx // 2, axis=1), pltpu.roll(x, d_idx // 2, axis=1))
        return x * c64 + rot * s64

    for h in range(n_heads):
        sl = slice(h * LANE, (h + 1) * LANE)
        qo_ref[0, :, sl] = rope128(q_ref[0, :, sl]).astype(qo_ref.dtype)
        kr = rope128(k_ref[0, :, sl])
        ko_ref[0, :, h, :] = kr
        kb_ref[0, :, sl] = kr.astype(kb_ref.dtype)
        vo_ref[0, :, h, :] = v_ref[0, :, sl]
    vb_ref[0] = v_ref[0].astype(vb_ref.dtype)
    for g in range(n_idx_groups):
        sl = slice(g * LANE, (g + 1) * LANE)
        qio_ref[0, :, sl] = rope64(qi_ref[0, :, sl])
    kw = kw_ref[0]
    kwo_ref[0] = jnp.where(lane < d_idx, rope64(kw), kw * idx_scale)


def _rope_tables(pos, d):
    inv = ROPE_THETA ** (-jnp.arange(0, d, 2, dtype=F32) / d)
    ang = pos.astype(F32)[:, None] * inv[None, :]
    cos, sin = jnp.cos(ang), jnp.sin(ang)
    reps = LANE // d
    c = jnp.tile(jnp.concatenate([cos, cos], axis=-1), (1, reps))
    s = jnp.tile(jnp.concatenate([-sin, sin], axis=-1), (1, reps))
    return c, s


def _rope_all(mid_all, row0, b, t, pos0, dims):
    mid = mid_all.reshape((1,) + mid_all.shape)
    cb, cc, qiw, d_idx, h_idx = dims["C_B"], dims["C_C"], dims["QIW"], dims["D_IDX"], dims["H_IDX"]
    assert dims["DH_C"] == LANE and LANE % d_idx == 0
    assert cb % cc == 0 and (cb + 3 * cc) % qiw == 0
    tm = _tile(t, 256, SUBLANE)
    pos = pos0 + jnp.arange(t)
    c128, s128 = _rope_tables(pos, LANE)
    c64, s64 = _rope_tables(pos, d_idx)
    ob = cb // cc
    assert row0 % tm == 0
    row = lambda width, idx: pl.BlockSpec((1, tm, width), lambda i, j: (0, row0 // tm + i * (t // tm) + j, idx))
    tab = pl.BlockSpec((tm, LANE), lambda i, j: (j, 0))
    out = lambda width: pl.BlockSpec((1, tm, width), lambda i, j: (i, j, 0))
    split = pl.BlockSpec((1, tm, cc // LANE, LANE), lambda i, j: (i, j, 0, 0))
    return pl.pallas_call(
        functools.partial(_rope_kernel, n_heads=cc // LANE, n_idx_groups=qiw // LANE, d_idx=d_idx,
                          idx_scale=(h_idx * d_idx) ** -0.5),
        grid=(b, t // tm),
        in_specs=[row(cc, ob + 1), row(cc, ob + 2), row(cc, ob), row(qiw, (cb + 3 * cc) // qiw),
                  row(LANE, (cb + 3 * cc + qiw) // LANE), tab, tab, tab, tab],
        out_specs=[out(cc), split, split, out(cc), out(cc), out(qiw), out(LANE)],
        out_shape=[jax.ShapeDtypeStruct((b, t, cc), BF16), jax.ShapeDtypeStruct((b, t, cc // LANE, LANE), F32),
                   jax.ShapeDtypeStruct((b, t, cc // LANE, LANE), F32),
                   jax.ShapeDtypeStruct((b, t, cc), BF16), jax.ShapeDtypeStruct((b, t, cc), BF16),
                   jax.ShapeDtypeStruct((b, t, qiw), F32), jax.ShapeDtypeStruct((b, t, LANE), F32)],
        compiler_params=_cparams(2),
        name="rope",
    )(mid, mid, mid, mid, mid, c128, s128, c64, s64)


def _head_sum(x, jmat):
    parts = []
    for g in range(x.shape[1] // LANE):
        parts.append(_dot_exact_rhs(x[:, g * LANE:(g + 1) * LANE], jmat))
    return parts[0] if len(parts) == 1 else jnp.concatenate(parts, axis=1)


def _head_ones():
    r = lax.broadcasted_iota(jnp.int32, (LANE, LANE), 0)
    c = lax.broadcasted_iota(jnp.int32, (LANE, LANE), 1)
    return jnp.where((r // N_A) == (c // N_A), 1.0, 0.0).astype(BF16)


def _rwkv_prep_kernel(pa_ref, halo_ref, shift_ref, mu_ref, w0_ref, w2_ref, a0_ref, a2_ref, g2_ref,
                      kk_ref, ka_ref, rk_ref,
                      rt_ref, at_ref, bt_ref, kt_ref, v_ref, bkt_ref, gct_ref, bonus_ref, g_ref, *, ca, chunk, tm):
    j = pl.program_id(1)
    prev_last = jnp.where(j == 0, shift_ref[0], halo_ref[0, SUBLANE - 1:SUBLANE, :])
    row = lax.broadcasted_iota(jnp.int32, (tm, 1), 0)

    def shifted(lo, hi):
        x = pa_ref[0, :, lo:hi]
        prev = jnp.where(row == 0, prev_last[:, lo:hi], pltpu.roll(x, 1, axis=0))
        return x + mu_ref[:, lo:hi] * (prev - x)

    r = shifted(0, ca)
    k = shifted(ca, 2 * ca)
    v = shifted(2 * ca, 3 * ca)
    wl = shifted(3 * ca, 3 * ca + LANE)
    al = shifted(3 * ca + LANE, 3 * ca + 2 * LANE)
    gl = shifted(3 * ca + 2 * LANE, pa_ref.shape[2])

    z = -(w0_ref[...] + _dot(jnp.tanh(wl), w2_ref[...]))
    softplus = jnp.maximum(z, 0.0) + jnp.log(1.0 + jnp.exp(-jnp.abs(z)))
    log_decay = -jnp.exp(-softplus - 0.5)
    rr = lax.broadcasted_iota(jnp.int32, (tm, tm), 0)
    cc = lax.broadcasted_iota(jnp.int32, (tm, tm), 1)
    same_chunk = (rr // chunk) == (cc // chunk)
    tril = jnp.where(same_chunk & (cc <= rr), 1.0, 0.0).astype(BF16)
    ones_blk = jnp.where(same_chunk, 1.0, 0.0).astype(BF16)
    cum = _dot_exact_lhs(tril, log_decay)
    cum_c = _dot_exact_lhs(ones_blk, log_decay)
    fwd = jnp.exp(cum_c - cum)
    back = jnp.exp(cum - cum_c)
    back_prev = jnp.exp(cum - log_decay - cum_c)
    gam_c = jnp.exp(cum_c)

    a = jax.nn.sigmoid(a0_ref[...] + _dot(al, a2_ref[...]))
    jmat = _head_ones()
    kk = k * kk_ref[...]
    kk = kk / jnp.maximum(jnp.sqrt(_head_sum(kk * kk, jmat)), 1e-12)
    kh = k * (1.0 + (a - 1.0) * ka_ref[...])
    bonus_ref[0] = _head_sum(r * kh * rk_ref[...], jmat) * v
    g_ref[0] = _dot(jax.nn.sigmoid(gl), g2_ref[...])
    bh = kk * a * fwd
    kf = kh * fwd
    rt_ref[0] = (r * back).astype(rt_ref.dtype)
    at_ref[0] = (-kk * back_prev).astype(at_ref.dtype)
    bt_ref[0] = bh.astype(bt_ref.dtype)
    kt_ref[0] = kf.astype(kt_ref.dtype)
    v_ref[0] = v.astype(v_ref.dtype)
    pad = N_A - chunk
    for c in range(tm // chunk):
        rows = slice(c * chunk, (c + 1) * chunk)
        if pad:
            zeros = jnp.zeros((pad, ca), F32)
            stacked = jnp.concatenate([bh[rows], zeros, kf[rows], zeros], axis=0)
        else:
            stacked = jnp.concatenate([bh[rows], kf[rows]], axis=0)
        bkt_ref[0, c] = stacked.T.astype(bkt_ref.dtype)
        gct_ref[0, c] = jnp.broadcast_to(gam_c[c * chunk:c * chunk + 1], (LANE, ca)).T


def _rwkv_prep(pa_all, row0, b, t, shift_p, P, l, dims, chunk):
    paw = pa_all.shape[1]
    pa = pa_all.reshape((1,) + pa_all.shape)
    ca = dims["C_A"]
    tm = chunk * max(1, min(128, t) // chunk)
    assert t % tm == 0 and row0 % tm == 0 and 2 * N_A == LANE
    hb = tm // SUBLANE
    nct = tm // chunk
    row = pl.BlockSpec((1, tm, ca), lambda i, j: (i, j, 0))
    cmaj = pl.BlockSpec((1, nct, ca, LANE), lambda i, j: (i, j, 0, 0))
    vec = lambda w: pl.BlockSpec((1, w), lambda i, j: (0, 0))
    mat = lambda r: pl.BlockSpec((r, ca), lambda i, j: (0, 0))
    tmaj = lambda dt: jax.ShapeDtypeStruct((b, t, ca), dt)
    return pl.pallas_call(
        functools.partial(_rwkv_prep_kernel, ca=ca, chunk=chunk, tm=tm),
        grid=(b, t // tm),
        in_specs=[pl.BlockSpec((1, tm, paw), lambda i, j: (0, row0 // tm + i * (t // tm) + j, 0)),
                  pl.BlockSpec((1, SUBLANE, paw),
                               lambda i, j: (0, jnp.maximum((row0 + i * t) // SUBLANE + j * hb - 1, 0), 0)),
                  pl.BlockSpec((1, 1, paw), lambda i, j: (i, 0, 0)),
                  vec(paw), vec(ca), mat(LANE), vec(ca), mat(LANE), mat(P["g2"].shape[1]),
                  vec(ca), vec(ca), vec(ca)],
        out_specs=[row] * 5 + [cmaj, cmaj, row, row],
        out_shape=[tmaj(BF16)] * 5 + [jax.ShapeDtypeStruct((b, t // chunk, ca, LANE), BF16),
                                      jax.ShapeDtypeStruct((b, t // chunk, ca, LANE), F32), tmaj(F32), tmaj(F32)],
        compiler_params=_cparams(2),
        name="rwkv_prep",
    )(pa, pa, shift_p.reshape(b, 1, paw), P["mu"][l], P["w0"][l], P["w2"][l], P["a0"][l], P["a2"][l],
      P["g2"][l], P["k_k"][l], P["k_a"][l], P["r_k"][l])


def _rwkv_scan_kernel(at_ref, rt_ref, v_ref, bt_ref, kt_ref, bkt_ref, gct_ref, s0_ref, y_ref, s_ref, *,
                      n_pairs, chunk, unroll):
    c = pl.program_id(1)

    @pl.when(c == 0)
    def _():
        s_ref[...] = s0_ref[...]

    rr = lax.broadcasted_iota(jnp.int32, (chunk, chunk), 0)
    cc = lax.broadcasted_iota(jnp.int32, (chunk, chunk), 1)
    lower_strict = cc < rr
    lower_incl = cc <= rr
    eye = jnp.where(rr == cc, 1.0, 0.0)
    first = lax.broadcasted_iota(jnp.int32, (chunk, LANE), 1) < N_A
    br = lax.broadcasted_iota(jnp.int32, (LANE, LANE), 0)
    bc = lax.broadcasted_iota(jnp.int32, (LANE, LANE), 1)
    block_diag = (br // N_A) == (bc // N_A)
    n_double = max(int(math.log2(chunk)) - 1, 0)
    pad = N_A - chunk

    def load(p):
        sl = pl.ds(pl.multiple_of(p * LANE, LANE), LANE)
        return (sl,
                at_ref[0, :, sl],
                rt_ref[0, :, sl],
                v_ref[0, :, sl],
                bt_ref[0, :, sl],
                kt_ref[0, :, sl],
                bkt_ref[0, 0, sl, :],
                s_ref[0, sl, :] * gct_ref[0, 0, sl, :])

    def group(g, carry):
        loaded = [load(g * unroll + j) for j in range(unroll)]
        sls, a, r, v, b, k, bk_t, sb = (list(col) for col in zip(*loaded))
        pairs = range(unroll)
        heads = [(i, hh) for i in pairs for hh in range(2)]
        own = lambda hh: first if hh == 0 else jnp.logical_not(first)
        sb16 = [sb[i].astype(BF16) for i in pairs]
        x = [jnp.dot(a[i], sb16[i], preferred_element_type=F32) for i in pairs]
        y = [jnp.dot(r[i], sb16[i], preferred_element_type=F32) for i in pairs]
        a_h = [jnp.where(own(hh), a[i], jnp.zeros_like(a[i])) for i, hh in heads]
        r_h = [jnp.where(own(hh), r[i], jnp.zeros_like(r[i])) for i, hh in heads]
        n_ab = [jnp.where(lower_strict, _dot_nt(a_h[j], b[i]), 0.0) for j, (i, _) in enumerate(heads)]
        n_ak = [jnp.where(lower_strict, _dot_nt(a_h[j], k[i]), 0.0) for j, (i, _) in enumerate(heads)]
        m_rb = [jnp.where(lower_incl, _dot_nt(r_h[j], b[i]), 0.0) for j, (i, _) in enumerate(heads)]
        m_rk = [jnp.where(lower_incl, _dot_nt(r_h[j], k[i]), 0.0) for j, (i, _) in enumerate(heads)]
        inv = [eye + n for n in n_ab]
        npow = n_ab
        for _ in range(n_double):
            npow = [_dot(n, n) for n in npow]
            inv = [iv + _dot(iv, n) for iv, n in zip(inv, npow)]
        w = [x[i] + _dot(n_ak[j], v[i]) for j, (i, _) in enumerate(heads)]
        u = [_dot(inv[j], w[j]) for j in range(len(heads))]
        y_h = [_dot(m_rb[j], u[j]) + _dot(m_rk[j], v[i]) for j, (i, _) in enumerate(heads)]
        for i in pairs:
            u_p = jnp.where(first, u[2 * i], u[2 * i + 1])
            y_new = y[i] + jnp.where(first, y_h[2 * i], y_h[2 * i + 1])
            v32 = v[i].astype(F32)
            if pad:
                zeros = jnp.zeros((pad, LANE), F32)
                stacked = jnp.concatenate([u_p, zeros, v32, zeros], axis=0)
            else:
                stacked = jnp.concatenate([u_p, v32], axis=0)
            s_new = sb[i] + jnp.where(block_diag, _dot(bk_t[i], stacked), 0.0)
            y_ref[0, :, sls[i]] = y_new
            s_ref[0, sls[i], :] = s_new
        return carry

    lax.fori_loop(0, n_pairs // unroll, group, 0)


def _rwkv_scan(rt, at, bt, kt, v, bkt, gct, s0, chunk):
    b, t, ca = rt.shape
    h = ca // N_A
    nc = t // chunk
    eye2 = jnp.eye(2, dtype=F32)
    s0t = jnp.swapaxes(s0, -1, -2).reshape(b, h // 2, 2, N_A, 1, N_A)
    sb0 = (s0t * eye2[None, None, :, None, :, None]).reshape(b, ca, LANE)
    tmaj = pl.BlockSpec((1, chunk, ca), lambda i, j: (i, j, 0))
    cmaj = pl.BlockSpec((1, 1, ca, LANE), lambda i, j: (i, j, 0, 0))
    st = pl.BlockSpec((1, ca, LANE), lambda i, j: (i, 0, 0))
    y, sb = pl.pallas_call(
        functools.partial(_rwkv_scan_kernel, n_pairs=h // 2, chunk=chunk, unroll=_tile(h // 2, 16, 1)),
        grid=(b, nc),
        in_specs=[tmaj, tmaj, tmaj, tmaj, tmaj, cmaj, cmaj, st],
        out_specs=[tmaj, st],
        out_shape=[jax.ShapeDtypeStruct((b, t, ca), F32), jax.ShapeDtypeStruct((b, ca, LANE), F32)],
        compiler_params=_cparams(2),
        name="rwkv_scan",
    )(at, rt, v, bt, kt, bkt, gct, sb0)
    sb = sb.reshape(b, h // 2, 2, N_A, 2, N_A)
    s_t = jnp.stack([sb[:, :, 0, :, 0, :], sb[:, :, 1, :, 1, :]], axis=2).reshape(b, h, N_A, N_A)
    return y, jnp.swapaxes(s_t, -1, -2)


def _rwkv_post_kernel(y_ref, bonus_ref, g_ref, lnw_ref, lnb_ref, o_ref):
    jmat = _head_ones()
    y = y_ref[...]
    mean = _head_sum(y, jmat) * (1.0 / N_A)
    d = y - mean
    var = _head_sum(d * d, jmat) * (1.0 / N_A)
    out = (d * lax.rsqrt(var + GN_EPS) * lnw_ref[...] + lnb_ref[...] + bonus_ref[...]) * g_ref[...]
    o_ref[...] = out.astype(o_ref.dtype)


def _rwkv_post(y, bonus, g, ln_w, ln_b):
    m, ca = y.shape
    tm = _tile(m, 256, SUBLANE)
    row = pl.BlockSpec((tm, ca), lambda i: (i, 0))
    vec = pl.BlockSpec((1, ca), lambda i: (0, 0))
    return pl.pallas_call(
        _rwkv_post_kernel,
        grid=(m // tm,),
        in_specs=[row, row, row, vec, vec],
        out_specs=row,
        out_shape=jax.ShapeDtypeStruct((m, ca), BF16),
        compiler_params=_cparams(1),
        name="rwkv_post",
    )(y, bonus, g, ln_w, ln_b)


def _pool_kernel(z_ref, hist_ref, w_ref, scale_ref, o_ref, x_scr, *, t, tm, group, pos0):
    halo = 2 * SUBLANE
    x_scr[0:halo, :] = hist_ref[0]
    x_scr[halo:halo + t, :] = z_ref[0]
    for i in range(t // tm):
        r0 = i * tm
        pos = pos0 + r0 + lax.broadcasted_iota(jnp.int32, (tm, 1), 0)
        for gi, win in enumerate(POOL_WINDOWS):
            lo, hi = gi * group, (gi + 1) * group
            cur = x_scr[halo + r0:halo + r0 + tm, lo:hi]
            tot = cur
            for back in range(1, win):
                tot = tot + x_scr[halo + r0 - back:halo + r0 - back + tm, lo:hi]
            cnt = jnp.minimum(pos + 1, win).astype(F32)
            d = tot / cnt - cur
            y = _dot(d, w_ref[gi]) * scale_ref[:, lo:hi]
            o_ref[0, r0:r0 + tm, lo:hi] = y.astype(o_ref.dtype)


def _pool(mid_all, row0, b, t, hist16, pool_w, pool_scale, pos0, dims):
    assert row0 % t == 0
    mid = mid_all.reshape((1,) + mid_all.shape)
    cb = dims["C_B"]
    group = pool_w.shape[1]
    assert len(POOL_WINDOWS) * group == cb
    tm = _tile(t, 256, SUBLANE)
    return pl.pallas_call(
        functools.partial(_pool_kernel, t=t, tm=tm, group=group, pos0=pos0),
        grid=(b,),
        in_specs=[pl.BlockSpec((1, t, cb), lambda i: (0, row0 // t + i, 0)),
                  pl.BlockSpec((1, 2 * SUBLANE, cb), lambda i: (i, 0, 0)),
                  pl.BlockSpec(pool_w.shape, lambda i: (0, 0, 0)),
                  pl.BlockSpec((1, cb), lambda i: (0, 0))],
        out_specs=pl.BlockSpec((1, t, cb), lambda i: (i, 0, 0)),
        out_shape=jax.ShapeDtypeStruct((b, t, cb), BF16),
        scratch_shapes=[pltpu.VMEM((t + 2 * SUBLANE, cb), F32)],
        compiler_params=_cparams(1),
        name="pool",
    )(mid, hist16, pool_w, pool_scale.reshape(1, cb))


def _monotone_key(x):
    x = jnp.where(x == 0.0, 0.0, x)
    bits = lax.bitcast_convert_type(x, jnp.int32)
    return jnp.where(bits < 0, bits ^ jnp.int32(0x7FFFFFFF), bits)


def _kth_largest_key(key, k):
    def body(i, tau):
        cand = tau + jnp.left_shift(jnp.int32(1), jnp.int32(31) - i)
        cnt = jnp.sum(jnp.where(key >= cand, 1.0, 0.0), axis=-1, keepdims=True)
        return jnp.where(cnt >= k, cand, tau)

    tau0 = jnp.full((key.shape[0], 1), -2 ** 31, jnp.int32)
    return lax.fori_loop(0, 32, body, tau0)


def _prefix_count(ind):
    r = lax.broadcasted_iota(jnp.int32, (LANE, LANE), 0)
    c = lax.broadcasted_iota(jnp.int32, (LANE, LANE), 1)
    tri = jnp.where(r <= c, 1.0, 0.0).astype(BF16)
    run = jnp.zeros((ind.shape[0], 1), F32)
    outs = []
    for j in range(ind.shape[1] // LANE):
        pj = jnp.dot(ind[:, j * LANE:(j + 1) * LANE].astype(BF16), tri, preferred_element_type=F32)
        outs.append(pj + run)
        run = run + pj[:, LANE - 1:LANE]
    return outs[0] if len(outs) == 1 else jnp.concatenate(outs, axis=1)


def _topk_mask(scores, k):
    key = _monotone_key(scores)
    tau = _kth_largest_key(key, float(k))
    gt = key > tau
    eq = key == tau
    need = float(k) - jnp.sum(jnp.where(gt, 1.0, 0.0), axis=-1, keepdims=True)
    prefix = _prefix_count(jnp.where(eq, 1.0, 0.0))
    return gt | (eq & (prefix <= need))


def _dsa_prompt_kernel(q_ref, k_ref, v_ref, qi_ref, ki_ref, wi_ref, o_ref, key_s, sc_s, m_s, l_s, acc_s, *,
                       tq, t, kc, topk, h_idx, d_idx, n_heads, dh):
    q0 = pl.program_id(1) * tq
    n_kc = t // kc
    needed = (q0 + tq + kc - 1) // kc
    qpos = q0 + lax.broadcasted_iota(jnp.int32, (tq, 1), 0)
    qi = [qi_ref[0, :, h * d_idx:(h + 1) * d_idx].astype(BF16) for h in range(h_idx)]
    wi = [wi_ref[0, :, h:h + 1] for h in range(h_idx)]

    def score_chunk(c, carry):
        keys = ki_ref[0, pl.ds(pl.multiple_of(c * kc, kc), kc), :].astype(BF16)
        acc = jnp.zeros((tq, kc), F32)
        for h in range(h_idx):
            acc = acc + jnp.maximum(_dot_nt(qi[h], keys), 0.0) * wi[h]
        spos = c * kc + lax.broadcasted_iota(jnp.int32, (1, kc), 1)
        key_s[c] = _monotone_key(jnp.where(spos <= qpos, acc, -jnp.inf))
        return carry

    lax.fori_loop(0, needed, score_chunk, 0)

    def count(pred):
        def body(c, tot):
            ind = jnp.where(pred(key_s[c]), 1.0, 0.0)
            for j in range(kc // LANE):
                tot = tot + ind[:, j * LANE:(j + 1) * LANE]
            return tot
        lanes = lax.fori_loop(0, needed, body, jnp.zeros((tq, LANE), F32))
        return jnp.sum(lanes, axis=-1, keepdims=True)

    def search(i, tau):
        cand = tau + jnp.left_shift(jnp.int32(1), jnp.int32(31) - i)
        return jnp.where(count(lambda key: key >= cand) >= float(topk), cand, tau)

    tau = lax.fori_loop(0, 32, search, jnp.full((tq, 1), -2 ** 31, jnp.int32))
    need = float(topk) - count(lambda key: key > tau)

    def mask_chunk(c, run):
        key = key_s[c]
        eq = jnp.where(key == tau, 1.0, 0.0)
        prefix = _prefix_count(eq) + run
        spos = c * kc + lax.broadcasted_iota(jnp.int32, (1, kc), 1)
        sel = (key > tau) | ((key == tau) & (prefix <= need))
        sc_s[c] = jnp.where(sel & (spos <= qpos), 1.0, 0.0)
        return prefix[:, kc - 1:kc]

    lax.fori_loop(0, needed, mask_chunk, jnp.zeros((tq, 1), F32))

    m_s[...] = jnp.full(m_s.shape, NEG_BIG, F32)
    l_s[...] = jnp.zeros(l_s.shape, F32)
    acc_s[...] = jnp.zeros(acc_s.shape, F32)
    scale = dh ** -0.5

    def attn_chunk(c, carry):
        rows = pl.ds(pl.multiple_of(c * kc, kc), kc)
        ok = sc_s[c] > 0.5
        heads = range(n_heads)
        sls = [slice(h * dh, (h + 1) * dh) for h in heads]
        lg = [_dot_nt(q_ref[0, :, sls[h]], k_ref[0, rows, sls[h]]) for h in heads]
        lg = [jnp.where(ok, x * scale, NEG_BIG) for x in lg]
        m_old = [m_s[h] for h in heads]
        m_new = [jnp.maximum(m_old[h], jnp.max(lg[h], axis=-1, keepdims=True)) for h in heads]
        p = [jnp.where(ok, jnp.exp(lg[h] - m_new[h]), 0.0) for h in heads]
        pv = [_dot(p[h], v_ref[0, rows, sls[h]]) for h in heads]
        for h in heads:
            alpha = jnp.exp(m_old[h] - m_new[h])
            l_s[h] = alpha * l_s[h] + jnp.sum(p[h], axis=-1, keepdims=True)
            acc_s[:, sls[h]] = alpha * acc_s[:, sls[h]] + pv[h]
            m_s[h] = m_new[h]
        return carry

    lax.fori_loop(0, needed, attn_chunk, 0)
    for h in range(n_heads):
        sl = slice(h * dh, (h + 1) * dh)
        o_ref[0, :, sl] = (acc_s[:, sl] / l_s[h]).astype(o_ref.dtype)


def _dsa_prompt(q, k, v, qi, ki, wi, dims):
    b, t, cc = q.shape
    tq = _tile(t, 128, SUBLANE)
    kc = _tile(t, 512, tq)
    topk = min(TOPK_MAX, t // 4)
    assert kc >= topk
    h_idx, d_idx, n_heads = dims["H_IDX"], dims["D_IDX"], dims["H_C"]
    qb = lambda w: pl.BlockSpec((1, tq, w), lambda i, j: (i, j, 0))
    full = lambda w: pl.BlockSpec((1, t, w), lambda i, j: (i, 0, 0))
    return pl.pallas_call(
        functools.partial(_dsa_prompt_kernel, tq=tq, t=t, kc=kc, topk=topk, h_idx=h_idx, d_idx=d_idx,
                          n_heads=n_heads, dh=dims["DH_C"]),
        grid=(b, t // tq),
        in_specs=[qb(cc), full(cc), full(cc), qb(h_idx * d_idx), full(d_idx), qb(h_idx)],
        out_specs=qb(cc),
        out_shape=jax.ShapeDtypeStruct((b, t, cc), BF16),
        scratch_shapes=[pltpu.VMEM((t // kc, tq, kc), jnp.int32), pltpu.VMEM((t // kc, tq, kc), F32),
                        pltpu.VMEM((n_heads, tq, 1), F32),
                        pltpu.VMEM((n_heads, tq, 1), F32), pltpu.VMEM((tq, cc), F32)],
        compiler_params=_cparams(2),
        name="dsa_prompt",
    )(q, k, v, qi, ki, wi)


def _idx_rows_scores(qi2, wi2, keys, tq, h_idx):
    s = jnp.maximum(_dot_nt(qi2, keys), 0.0) * wi2
    acc = s[0:tq]
    for h in range(1, h_idx):
        acc = acc + s[h * tq:(h + 1) * tq]
    return acc


def _dsa_sample_scores_kernel(pt_ref, qi_ref, wi_ref, *refs, tq, h_idx):
    o_ref = refs[-1]
    for g, kc_ref in enumerate(refs[:-1]):
        o_ref[0, :, g * PAGE:(g + 1) * PAGE] = _idx_rows_scores(qi_ref[0], wi_ref[0], kc_ref[0, 0], tq, h_idx)


def _dsa_sample_scores(page_table, qi2, wi2, cache_kidx, l, tq, h_idx):
    b, n_pages = page_table.shape
    d_idx = cache_kidx.shape[-1]
    rows = qi2.shape[1]
    group = _tile(n_pages, 16, 1)
    page = lambda g: pl.BlockSpec((1, 1, PAGE, d_idx), lambda i, p, pt: (l, pt[i, p * group + g], 0, 0))
    grid_spec = pltpu.PrefetchScalarGridSpec(
        num_scalar_prefetch=1,
        grid=(b, n_pages // group),
        in_specs=[pl.BlockSpec((1, rows, d_idx), lambda i, p, pt: (i, 0, 0)),
                  pl.BlockSpec((1, rows, 1), lambda i, p, pt: (i, 0, 0))] + [page(g) for g in range(group)],
        out_specs=pl.BlockSpec((1, tq, group * PAGE), lambda i, p, pt: (i, 0, p)),
    )
    return pl.pallas_call(
        functools.partial(_dsa_sample_scores_kernel, tq=tq, h_idx=h_idx),
        grid_spec=grid_spec,
        out_shape=jax.ShapeDtypeStruct((b, tq, n_pages * PAGE), F32),
        compiler_params=_cparams(2),
        name="dsa_sample_scores",
    )(page_table, qi2, wi2, *([cache_kidx] * group))


def _dsa_sample_select_kernel(sc_ref, qi_ref, wi_ref, kn_ref, o_ref, on_ref, *, tq, h_idx, topk, past):
    new = _idx_rows_scores(qi_ref[0], wi_ref[0], kn_ref[0], tq, h_idx)
    qrow = lax.broadcasted_iota(jnp.int32, (tq, PAGE), 0)
    col = lax.broadcasted_iota(jnp.int32, (tq, PAGE), 1)
    new_ok = col <= qrow
    scores = jnp.concatenate([sc_ref[0], jnp.where(new_ok, new, -jnp.inf)], axis=1)
    sel = jnp.where(_topk_mask(scores, topk), 1.0, 0.0)
    o_ref[0] = sel[:, :past]
    on_ref[0] = jnp.where(new_ok, sel[:, past:], 0.0)


def _dsa_sample_select(scores, qi2, wi2, ki_new_pad, tq, h_idx):
    b, _, past = scores.shape
    topk = min(TOPK_MAX, (past + tq) // 4)
    rows, d_idx = qi2.shape[1], qi2.shape[2]
    return pl.pallas_call(
        functools.partial(_dsa_sample_select_kernel, tq=tq, h_idx=h_idx, topk=topk, past=past),
        grid=(b,),
        in_specs=[pl.BlockSpec((1, tq, past), lambda i: (i, 0, 0)),
                  pl.BlockSpec((1, rows, d_idx), lambda i: (i, 0, 0)),
                  pl.BlockSpec((1, rows, 1), lambda i: (i, 0, 0)),
                  pl.BlockSpec((1, PAGE, d_idx), lambda i: (i, 0, 0))],
        out_specs=[pl.BlockSpec((1, tq, past), lambda i: (i, 0, 0)), pl.BlockSpec((1, tq, PAGE), lambda i: (i, 0, 0))],
        out_shape=[jax.ShapeDtypeStruct((b, tq, past), F32), jax.ShapeDtypeStruct((b, tq, PAGE), F32)],
        compiler_params=_cparams(1),
        name="dsa_sample_select",
    )(scores, qi2, wi2, ki_new_pad)


def _dsa_sample_attn_kernel(pt_ref, q_ref, kn_ref, vn_ref, m_ref, mn_ref, *refs, n_steps, group, n_heads, dh, tq):
    kc_refs, vc_refs = refs[:group], refs[group:2 * group]
    o_ref, q2_s, exp_s, hm_s, m_s, l_s, acc_s = refs[2 * group:]
    p = pl.program_id(1)
    rows = n_heads * tq
    cols = PAGE * n_heads

    @pl.when(p == 0)
    def _():
        for h in range(n_heads):
            q2_s[h * tq:(h + 1) * tq, :] = q_ref[0, :, h * dh:(h + 1) * dh]
        pos = lax.broadcasted_iota(jnp.int32, (PAGE, cols), 0)
        col = lax.broadcasted_iota(jnp.int32, (PAGE, cols), 1)
        exp_s[...] = jnp.where(col // n_heads == pos, 1.0, 0.0).astype(exp_s.dtype)
        rr = lax.broadcasted_iota(jnp.int32, (rows, cols), 0)
        cc = lax.broadcasted_iota(jnp.int32, (rows, cols), 1)
        hm_s[...] = jnp.where((cc % n_heads) == (rr // tq), 1.0, 0.0)
        m_s[...] = jnp.full(m_s.shape, NEG_BIG, F32)
        l_s[...] = jnp.zeros(l_s.shape, F32)
        acc_s[...] = jnp.zeros(acc_s.shape, F32)

    def update(kmat, vmat, sel):
        lg = _dot_nt(q2_s[...], kmat) * (dh ** -0.5)
        sel8 = jnp.dot(sel.astype(BF16), exp_s[...], preferred_element_type=F32)
        msk = jnp.concatenate([sel8] * n_heads, axis=0) * hm_s[...] > 0.5
        lgm = jnp.where(msk, lg, NEG_BIG)
        m_new = jnp.maximum(m_s[...], jnp.max(lgm, axis=-1, keepdims=True))
        alpha = jnp.exp(m_s[...] - m_new)
        pm = jnp.where(msk, jnp.exp(lgm - m_new), 0.0)
        l_s[...] = alpha * l_s[...] + jnp.sum(pm, axis=-1, keepdims=True)
        acc_s[...] = alpha * acc_s[...] + _dot(pm, vmat)
        m_s[...] = m_new

    @pl.when(p < n_steps)
    def _():
        for g in range(group):
            update(kc_refs[g][0, 0].reshape(cols, dh), vc_refs[g][0, 0].reshape(cols, dh),
                   m_ref[0, :, g * PAGE:(g + 1) * PAGE])

    @pl.when(p == n_steps)
    def _():
        update(kn_ref[0].reshape(cols, dh), vn_ref[0].reshape(cols, dh), mn_ref[0])
        res = acc_s[...] / l_s[...]
        for h in range(n_heads):
            o_ref[0, :, h * dh:(h + 1) * dh] = res[h * tq:(h + 1) * tq, :].astype(o_ref.dtype)


def _dsa_sample_attn(page_table, q, cache_k, cache_v, k_new_pad, v_new_pad, mask, mask_new, l, dims):
    b, n_pages = page_table.shape
    tq = q.shape[1]
    n_heads, dh = dims["H_C"], dims["DH_C"]
    cc = n_heads * dh
    group = _tile(n_pages, 8, 1)
    n_steps = n_pages // group
    page = lambda g: pl.BlockSpec(
        (1, 1, PAGE, n_heads, dh),
        lambda i, p, pt: (l, pt[i, jnp.minimum(p * group + g, n_pages - 1)], 0, 0, 0))
    new = pl.BlockSpec((1, PAGE, n_heads, dh), lambda i, p, pt: (i, 0, 0, 0))
    grid_spec = pltpu.PrefetchScalarGridSpec(
        num_scalar_prefetch=1,
        grid=(b, n_steps + 1),
        in_specs=[pl.BlockSpec((1, tq, cc), lambda i, p, pt: (i, 0, 0)), new, new,
                  pl.BlockSpec((1, tq, group * PAGE), lambda i, p, pt: (i, 0, jnp.minimum(p, n_steps - 1))),
                  pl.BlockSpec((1, tq, PAGE), lambda i, p, pt: (i, 0, 0))]
                 + [page(g) for g in range(group)] * 2,
        out_specs=pl.BlockSpec((1, tq, cc), lambda i, p, pt: (i, 0, 0)),
        scratch_shapes=[pltpu.VMEM((n_heads * tq, dh), BF16), pltpu.VMEM((PAGE, PAGE * n_heads), BF16),
                        pltpu.VMEM((n_heads * tq, PAGE * n_heads), F32), pltpu.VMEM((n_heads * tq, 1), F32),
                        pltpu.VMEM((n_heads * tq, 1), F32), pltpu.VMEM((n_heads * tq, dh), F32)],
    )
    return pl.pallas_call(
        functools.partial(_dsa_sample_attn_kernel, n_steps=n_steps, group=group, n_heads=n_heads, dh=dh, tq=tq),
        grid_spec=grid_spec,
        out_shape=jax.ShapeDtypeStruct((b, tq, cc), BF16),
        compiler_params=_cparams(2),
        name="dsa_sample_attn",
    )(page_table, q, k_new_pad, v_new_pad, mask, mask_new, *([cache_k] * group), *([cache_v] * group))


def _dsa_sample(q, k_f32, v_f32, qi, ki, wi, cache_k, cache_v, cache_kidx, page_table, l, dims):
    b, tq, cc = q.shape
    h_idx, d_idx, n_heads, dh = dims["H_IDX"], dims["D_IDX"], dims["H_C"], dims["DH_C"]
    qi2 = jnp.transpose(qi.reshape(b, tq, h_idx, d_idx), (0, 2, 1, 3)).reshape(b, h_idx * tq, d_idx)
    wi2 = jnp.transpose(wi, (0, 2, 1)).reshape(b, h_idx * tq, 1)
    pad_rows = lambda x: jnp.pad(x, ((0, 0), (0, PAGE - tq)) + ((0, 0),) * (x.ndim - 2))
    scores = _dsa_sample_scores(page_table, qi2, wi2, cache_kidx, l, tq, h_idx)
    mask, mask_new = _dsa_sample_select(scores, qi2, wi2, pad_rows(ki), tq, h_idx)
    return _dsa_sample_attn(page_table, q, cache_k, cache_v, pad_rows(k_f32), pad_rows(v_f32), mask, mask_new, l,
                            dims)


def _prepare_weights(raw, dims):
    ca, cb, cc, qiw, d_idx, h_idx, d = (dims[k] for k in ("C_A", "C_B", "C_C", "QIW", "D_IDX", "H_IDX", "D"))
    dd, da, dg = dims["D_DECAY"], dims["D_AAA"], dims["D_GATE"]
    assert dd <= LANE and da <= LANE and dg % LANE == 0
    a_cols = 3 * ca + dd + da + dg
    o = [0, a_cols, a_cols + cb, a_cols + cb + cc, a_cols + cb + 2 * cc, a_cols + cb + 3 * cc]
    o += [o[-1] + qiw, o[-1] + qiw + d_idx, o[-1] + qiw + d_idx + h_idx]
    w_in = raw["w_in"]

    def pack_pa(x):
        zeros = lambda n: jnp.zeros(x.shape[:-1] + (n,), x.dtype)
        return jnp.concatenate([x[..., :3 * ca], x[..., 3 * ca:3 * ca + dd], zeros(LANE - dd),
                                x[..., 3 * ca + dd:3 * ca + dd + da], zeros(LANE - da),
                                x[..., 3 * ca + dd + da:a_cols]], axis=-1)

    kiw = LANE - d_idx - h_idx
    assert kiw >= 0
    w_pa = pack_pa(w_in[..., :a_cols]).astype(BF16)
    w_mid = jnp.concatenate([w_in[..., o[1]:o[2]], w_in[..., o[4]:o[5]], w_in[..., o[2]:o[4]],
                             w_in[..., o[5]:o[8]], jnp.zeros(w_in.shape[:-1] + (kiw,), w_in.dtype)],
                            axis=-1).astype(BF16)
    w_gl = w_in[..., o[8]:].astype(BF16)
    pad_rows = lambda x: jnp.pad(x, ((0, 0), (0, LANE - x.shape[1]), (0, 0)))
    depth = w_in.shape[0]
    vec = lambda x: x.reshape(depth, 1, -1)
    P = dict(
        w_pa=w_pa, w_mid=w_mid, w_gl=w_gl,
        mu=vec(pack_pa(raw["rwkv_mu"])), w0=vec(raw["rwkv_w0"]), w2=pad_rows(raw["rwkv_w2"]).astype(BF16),
        a0=vec(raw["rwkv_a0"]), a2=pad_rows(raw["rwkv_a2"]).astype(BF16), g2=raw["rwkv_g2"].astype(BF16),
        k_k=vec(raw["rwkv_k_k"]), k_a=vec(raw["rwkv_k_a"]), r_k=vec(raw["rwkv_r_k"]),
        ln_w=vec(raw["rwkv_ln_w"]), ln_b=vec(raw["rwkv_ln_b"]),
        pool_w=raw["pool_w"].astype(BF16), pool_scale=raw["pool_scale"],
    )
    for name in ("ffn1_w_down", "ffn2_w_down", "w_br_a", "w_br_b", "w_br_c", "w_out"):
        P[name] = raw[name].astype(BF16)
    for name in ("ffn1_w_gate", "ffn1_w_up", "ffn2_w_gate", "ffn2_w_up", "ffn1_norm", "mix_norm", "ffn2_norm"):
        P[name] = raw[name]
    P["pack_pa"] = pack_pa
    return P


def _unpack_pa(x, dims):
    ca, dd, da = dims["C_A"], dims["D_DECAY"], dims["D_AAA"]
    return jnp.concatenate([x[..., :3 * ca], x[..., 3 * ca:3 * ca + dd],
                            x[..., 3 * ca + LANE:3 * ca + LANE + da], x[..., 3 * ca + 2 * LANE:]], axis=-1)


def _branches(pa, mid, grp, P, l, dims):
    row0, b, t, pos0 = grp["row0"], grp["b"], grp["t"], grp["pos0"]
    m = b * t
    ca, cb, cc = dims["C_A"], dims["C_B"], dims["C_C"]
    rows = lambda x: x[row0:row0 + m].reshape(b, t, x.shape[1])

    chunk = min(N_A, t)
    rt, at, bt, kt, v_a, bkt, gct, bonus, g = _rwkv_prep(pa, row0, b, t, P["pack_pa"](grp["shift"]), P, l, dims,
                                                          chunk)
    y_a, new_wkv = _rwkv_scan(rt, at, bt, kt, v_a, bkt, gct, grp["wkv"], chunk)
    ya = _rwkv_post(y_a.reshape(m, ca), bonus.reshape(m, ca), g.reshape(m, ca), P["ln_w"][l], P["ln_b"][l])
    new_shift = _unpack_pa(rows(pa)[:, -1], dims)

    hist16 = jnp.pad(grp["pool"], ((0, 0), (2 * SUBLANE - POOL_HIST, 0), (0, 0)))
    yb = _pool(mid, row0, b, t, hist16, P["pool_w"][l], P["pool_scale"][l], pos0, dims)
    new_pool = jnp.concatenate([grp["pool"], rows(mid)[:, -POOL_HIST:, :cb]], axis=1)[:, -POOL_HIST:]

    q, k, v, k_bf, v_bf, qi, kiwi = _rope_all(mid, row0, b, t, pos0, dims)
    ki = kiwi[..., :dims["D_IDX"]]
    wi = kiwi[..., dims["D_IDX"]:dims["D_IDX"] + dims["H_IDX"]]
    if grp["cache"] is None:
        yc = _dsa_prompt(q, k_bf, v_bf, qi, ki, wi, dims)
    else:
        cache_k, cache_v, cache_kidx, page_table = grp["cache"]
        yc = _dsa_sample(q, k, v, qi, ki, wi, cache_k, cache_v, cache_kidx, page_table, l, dims)
    return (ya, yb.reshape(m, cb), yc.reshape(m, cc)), (k, v, ki, new_wkv, new_shift, new_pool)


def _layer(carry, groups, P, l, next_g, dims):
    h, hb, ssq = carry
    h, u, ssq = _ffn(h, hb, ssq, P["ffn1_w_gate"], P["ffn1_w_up"], P["ffn1_w_down"], l, P["mix_norm"][l])
    pa = _mm(u, P["w_pa"], l, out_dtype=F32, ssq=ssq, name="proj_pa")
    mid = _mm(u, P["w_mid"], l, out_dtype=F32, ssq=ssq, tm_target=1024, name="proj_mid")
    gates = _mm(u, P["w_gl"], l, out_dtype=BF16, ssq=ssq, act="sigmoid", tn_target=512, name="proj_gate")
    outs = [_branches(pa, mid, grp, P, l, dims) for grp in groups]
    ya, yb, yc = (jnp.concatenate([o[0][i] for o in outs], axis=0) for i in range(3))
    merged = _merge(ya, yb, yc, P["w_br_a"], P["w_br_b"], P["w_br_c"], l, gates)
    h, hb, ssq = _mm(merged, P["w_out"], l, out_dtype=F32, res=h, next_g=P["ffn2_norm"][l], tn_target=512,
                     name="w_out")
    carry = _ffn(h, hb, ssq, P["ffn2_w_gate"], P["ffn2_w_up"], P["ffn2_w_down"], l, next_g)
    return carry, [o[1] for o in outs]


def kernel(x_prompt, x_sample, cache_k, cache_v, cache_kidx, state_wkv, state_shift, state_pool, page_table,
           ffn1_norm, ffn1_w_gate, ffn1_w_up, ffn1_w_down, mix_norm, w_in, rwkv_mu, rwkv_w0, rwkv_w2,
           rwkv_a0, rwkv_a2, rwkv_g2, rwkv_k_k, rwkv_k_a, rwkv_r_k, rwkv_ln_w, rwkv_ln_b, w_br_a, pool_w,
           pool_scale, w_br_b, w_br_c, w_out, ffn2_norm, ffn2_w_gate, ffn2_w_up, ffn2_w_down, final_norm):
    raw = dict(ffn1_norm=ffn1_norm, ffn1_w_gate=ffn1_w_gate, ffn1_w_up=ffn1_w_up, ffn1_w_down=ffn1_w_down,
               mix_norm=mix_norm, w_in=w_in, rwkv_mu=rwkv_mu, rwkv_w0=rwkv_w0, rwkv_w2=rwkv_w2, rwkv_a0=rwkv_a0,
               rwkv_a2=rwkv_a2, rwkv_g2=rwkv_g2, rwkv_k_k=rwkv_k_k, rwkv_k_a=rwkv_k_a, rwkv_r_k=rwkv_r_k,
               rwkv_ln_w=rwkv_ln_w, rwkv_ln_b=rwkv_ln_b, w_br_a=w_br_a, pool_w=pool_w, pool_scale=pool_scale,
               w_br_b=w_br_b, w_br_c=w_br_c, w_out=w_out, ffn2_norm=ffn2_norm, ffn2_w_gate=ffn2_w_gate,
               ffn2_w_up=ffn2_w_up, ffn2_w_down=ffn2_w_down)
    depth, d = mix_norm.shape
    ca = rwkv_w0.shape[-1]
    cb = pool_scale.shape[-1]
    n_heads, dh = cache_k.shape[3], cache_k.shape[4]
    cc = n_heads * dh
    d_idx = cache_kidx.shape[-1]
    dd, da, dg = rwkv_w2.shape[1], rwkv_a2.shape[1], rwkv_g2.shape[1]
    a_cols = 3 * ca + dd + da + dg
    h_idx = (w_in.shape[-1] - a_cols - cb - 3 * cc - d_idx - 3 * d) // (d_idx + 1)
    dims = dict(D=d, C_A=ca, C_B=cb, C_C=cc, H_C=n_heads, DH_C=dh, D_IDX=d_idx, H_IDX=h_idx,
                QIW=h_idx * d_idx, D_DECAY=dd, D_AAA=da, D_GATE=dg)
    assert w_in.shape[-1] == a_cols + cb + 3 * cc + h_idx * d_idx + d_idx + h_idx + 3 * d
    P = _prepare_weights(raw, dims)

    bp = x_prompt.shape[0]
    past = page_table.shape[1] * PAGE
    bs, ts = x_sample.shape[:2]
    tp = x_prompt.shape[1]
    mp, ms = bp * tp, bs * ts
    x_all = jnp.concatenate([x_prompt.reshape(mp, d), x_sample.reshape(ms, d)], axis=0)
    carry = (x_all,) + tuple(_prenorm(x_all, ffn1_norm[0]))
    st_p, st_s = [], []
    for l in range(depth):
        next_g = ffn1_norm[l + 1] if l + 1 < depth else final_norm
        groups = [
            dict(row0=0, b=bp, t=tp, pos0=0, cache=None, shift=jnp.zeros((bp, a_cols), F32),
                 wkv=jnp.zeros((bp, ca // N_A, N_A, N_A), F32), pool=jnp.zeros((bp, POOL_HIST, cb), F32)),
            dict(row0=mp, b=bs, t=ts, pos0=past, cache=(cache_k, cache_v, cache_kidx, page_table),
                 shift=state_shift[l], wkv=state_wkv[l], pool=state_pool[l]),
        ]
        carry, (sp, ss) = _layer(carry, groups, P, l, next_g, dims)
        st_p.append(sp)
        st_s.append(ss)
    y_p = _rmsnorm(carry[0], 0, mp, final_norm, F32).reshape(x_prompt.shape)
    y_s = _rmsnorm(carry[0], mp, ms, final_norm, F32).reshape(x_sample.shape)
    stk = lambda sts, i: jnp.stack([s[i] for s in sts])
    return (y_p, y_s,
            stk(st_p, 0), stk(st_p, 1), stk(st_p, 2), stk(st_p, 3), stk(st_p, 4), stk(st_p, 5),
            stk(st_s, 0), stk(st_s, 1), stk(st_s, 2), stk(st_s, 3), stk(st_s, 4), stk(st_s, 5))
```

```python
import functools
import math

import jax
import jax.numpy as jnp
from jax import lax
from jax.experimental import pallas as pl
from jax.experimental.pallas import tpu as pltpu

F32 = jnp.float32
BF16 = jnp.bfloat16

LANE = 128
SUBLANE = 8
VMEM_LIMIT_BYTES = 56 * 1024 * 1024
ROW_TILE = 1400

N_A = 64
GN_EPS = 64e-5
POOL_WINDOWS = (2, 4, 8, 16)
POOL_HIST = max(POOL_WINDOWS) - 1
TOPK_MAX = 256
ROPE_THETA = 10000.0
RMS_EPS = 1e-6
PAGE = 128
NEG_BIG = -1e30


def _cparams(n_axes):
    return pltpu.CompilerParams(dimension_semantics=("arbitrary",) * n_axes,
                                vmem_limit_bytes=VMEM_LIMIT_BYTES)


def _tile(n, target, mult):
    best = None
    for t in range(mult, min(n, target) + 1, mult):
        if n % t == 0:
            best = t
    return best if best is not None else n


def _roundup(n, m):
    return -(-n // m) * m


def _dot(a, b):
    return jnp.dot(a.astype(BF16), b.astype(BF16), preferred_element_type=F32)


def _dot_nt(a, b):
    return lax.dot_general(a.astype(BF16), b.astype(BF16), (((1,), (1,)), ((), ())),
                           preferred_element_type=F32)


def _split3(x):
    h1 = x.astype(BF16)
    r1 = x - h1.astype(F32)
    h2 = r1.astype(BF16)
    h3 = (r1 - h2.astype(F32)).astype(BF16)
    return h1, h2, h3


def _dot_exact_rhs(x, m_bf16):
    h1, h2, h3 = _split3(x)
    return (jnp.dot(h1, m_bf16, preferred_element_type=F32)
            + jnp.dot(h2, m_bf16, preferred_element_type=F32)
            + jnp.dot(h3, m_bf16, preferred_element_type=F32))


def _dot_exact_lhs(m_bf16, x):
    h1, h2, h3 = _split3(x)
    return (jnp.dot(m_bf16, h1, preferred_element_type=F32)
            + jnp.dot(m_bf16, h2, preferred_element_type=F32)
            + jnp.dot(m_bf16, h3, preferred_element_type=F32))


def _rmsnorm_kernel(x_ref, g_ref, o_ref):
    x = x_ref[...]
    ms = jnp.mean(x * x, axis=-1, keepdims=True)
    o_ref[...] = (x * lax.rsqrt(ms + RMS_EPS) * g_ref[...]).astype(o_ref.dtype)


def _rmsnorm(x2d, row0, rows, g, out_dtype):
    d = x2d.shape[1]
    tm = _tile(rows, 256, SUBLANE)
    assert row0 % tm == 0
    return pl.pallas_call(
        _rmsnorm_kernel,
        grid=(rows // tm,),
        in_specs=[pl.BlockSpec((tm, d), lambda i: (row0 // tm + i, 0)), pl.BlockSpec((1, d), lambda i: (0, 0))],
        out_specs=pl.BlockSpec((tm, d), lambda i: (i, 0)),
        out_shape=jax.ShapeDtypeStruct((rows, d), out_dtype),
        compiler_params=_cparams(1),
        name="rmsnorm",
    )(x2d, g.reshape(1, d))


def _prenorm_kernel(x_ref, g_ref, xb_ref, ssq_ref):
    x = x_ref[...]
    xb_ref[...] = (x * g_ref[...]).astype(xb_ref.dtype)
    ssq_ref[...] = jnp.sum(x * x, axis=-1, keepdims=True)


def _prenorm(x2d, g):
    m, d = x2d.shape
    tm = _tile(m, 256, SUBLANE)
    return pl.pallas_call(
        _prenorm_kernel,
        grid=(m // tm,),
        in_specs=[pl.BlockSpec((tm, d), lambda i: (i, 0)), pl.BlockSpec((1, d), lambda i: (0, 0))],
        out_specs=[pl.BlockSpec((tm, d), lambda i: (i, 0)), pl.BlockSpec((tm, 1), lambda i: (i, 0))],
        out_shape=[jax.ShapeDtypeStruct((m, d), BF16), jax.ShapeDtypeStruct((m, 1), F32)],
        compiler_params=_cparams(1),
        name="prenorm",
    )(x2d, g.reshape(1, d))


def _row_scale(ssq, width):
    return lax.rsqrt(ssq * (1.0 / width) + RMS_EPS)


def _mm_kernel(*refs, nk, kdim, w_t, has_ssq, has_res, has_next, res_scale, act):
    it = iter(refs)
    x_ref, w_ref = next(it), next(it)
    ssq_ref = next(it) if has_ssq else None
    res_ref = next(it) if has_res else None
    gnext_ref = next(it) if has_next else None
    o_ref = next(it)
    ob_ref, ssqo_ref = (next(it), next(it)) if has_next else (None, None)

    def epilogue(acc):
        if has_ssq:
            acc = acc * _row_scale(ssq_ref[...], kdim)
        if act == "sigmoid":
            acc = jax.nn.sigmoid(acc)
        if has_res:
            acc = res_ref[...] + res_scale * acc
        o_ref[...] = acc.astype(o_ref.dtype)
        if has_next:
            ob_ref[...] = (acc * gnext_ref[...]).astype(ob_ref.dtype)
            part_ssq = jnp.sum(acc * acc, axis=-1, keepdims=True)
            j = pl.program_id(1)

            @pl.when(j == 0)
            def _():
                ssqo_ref[...] = part_ssq

            @pl.when(j > 0)
            def _():
                ssqo_ref[...] += part_ssq

    if w_t:
        part = lax.dot_general(x_ref[...], w_ref[0].astype(BF16), (((1,), (1,)), ((), ())),
                               preferred_element_type=F32)
    else:
        part = jnp.dot(x_ref[...], w_ref[0], preferred_element_type=F32)
    if nk == 1:
        epilogue(part)
    else:
        acc_ref = next(it)
        k = pl.program_id(2)

        @pl.when(k == 0)
        def _():
            acc_ref[...] = part

        @pl.when(k > 0)
        def _():
            acc_ref[...] += part

        @pl.when(k == nk - 1)
        def _():
            epilogue(acc_ref[...])


def _mm(x, w, l, *, out_dtype, ssq=None, res=None, res_scale=1.0, act=None, next_g=None, tm_target=ROW_TILE,
        tn_target=1024, tk_target=4096, w_t=False, w_rows=None, name="mm"):
    m, kdim = x.shape
    n = w_rows[1] if w_rows is not None else (w.shape[1] if w_t else w.shape[2])
    tm = _tile(m, tm_target, 2 * SUBLANE)
    tn = _tile(n, tn_target, LANE)
    tk = _tile(kdim, tk_target, LANE)
    nk = kdim // tk
    tile = pl.BlockSpec((tm, tn), lambda i, j, k: (i, j))
    col = pl.BlockSpec((tm, 1), lambda i, j, k: (i, 0))
    if w_rows is not None:
        assert w_t and w_rows[0] % (2 * SUBLANE) == 0
        wspec = pl.BlockSpec((pl.Element(1), pl.Element(tn), pl.Element(tk)),
                             lambda i, j, k: (l, pl.multiple_of(w_rows[0] + j * tn, 2 * SUBLANE),
                                              pl.multiple_of(k * tk, LANE)))
    elif w_t:
        wspec = pl.BlockSpec((1, tn, tk), lambda i, j, k: (l, j, k))
    else:
        wspec = pl.BlockSpec((1, tk, tn), lambda i, j, k: (l, k, j))
    in_specs = [pl.BlockSpec((tm, tk), lambda i, j, k: (i, k)), wspec]
    args = [x, w]
    if ssq is not None:
        in_specs.append(col)
        args.append(ssq)
    if res is not None:
        in_specs.append(tile)
        args.append(res)
    out_specs, out_shape = tile, jax.ShapeDtypeStruct((m, n), out_dtype)
    if next_g is not None:
        assert res is not None
        in_specs.append(pl.BlockSpec((1, tn), lambda i, j, k: (0, j)))
        args.append(next_g.reshape(1, n))
        out_specs = [tile, tile, col]
        out_shape = [out_shape, jax.ShapeDtypeStruct((m, n), BF16), jax.ShapeDtypeStruct((m, 1), F32)]
    scratch = [pltpu.VMEM((tm, tn), F32)] if nk > 1 else []
    return pl.pallas_call(
        functools.partial(_mm_kernel, nk=nk, kdim=kdim, w_t=w_t, has_ssq=ssq is not None, has_res=res is not None,
                          has_next=next_g is not None, res_scale=res_scale, act=act),
        grid=(m // tm, n // tn, nk),
        in_specs=in_specs,
        out_specs=out_specs,
        out_shape=out_shape,
        scratch_shapes=scratch,
        compiler_params=_cparams(3),
        name=name,
    )(*args)


def _swiglu_up_kernel(x_ref, ssq_ref, wg_ref, wu_ref, o_ref, *, d):
    x = x_ref[...]
    scale = _row_scale(ssq_ref[...], d)
    g = jnp.dot(x, wg_ref[0].astype(BF16), preferred_element_type=F32) * scale
    u = jnp.dot(x, wu_ref[0].astype(BF16), preferred_element_type=F32) * scale
    o_ref[...] = (g * jax.nn.sigmoid(g) * u).astype(o_ref.dtype)


def _swiglu_up(x, ssq, wg, wu, l):
    m, d = x.shape
    f = wg.shape[2]
    tm = _tile(m, ROW_TILE, 2 * SUBLANE)
    tn = _tile(f, 512, LANE)
    return pl.pallas_call(
        functools.partial(_swiglu_up_kernel, d=d),
        grid=(m // tm, f // tn),
        in_specs=[pl.BlockSpec((tm, d), lambda i, j: (i, 0)),
                  pl.BlockSpec((tm, 1), lambda i, j: (i, 0)),
                  pl.BlockSpec((1, d, tn), lambda i, j: (l, 0, j)),
                  pl.BlockSpec((1, d, tn), lambda i, j: (l, 0, j))],
        out_specs=pl.BlockSpec((tm, tn), lambda i, j: (i, j)),
        out_shape=jax.ShapeDtypeStruct((m, f), BF16),
        compiler_params=_cparams(2),
        name="swiglu_up",
    )(x, ssq, wg, wu)


def _ffn_down_kernel(a_ref, w_ref, res_ref, gnext_ref, o_ref, ob_ref, ssq_ref, acc_ref, *, nk, res_scale):
    k = pl.program_id(1)
    j = pl.program_id(2)
    part = jnp.dot(a_ref[...], w_ref[0], preferred_element_type=F32)

    if nk > 1:
        @pl.when(k == 0)
        def _():
            acc_ref[j] = part

        @pl.when((k > 0) & (k < nk - 1))
        def _():
            acc_ref[j] += part

    @pl.when(k == nk - 1)
    def _():
        acc = acc_ref[j] + part if nk > 1 else part
        h = res_ref[...] + res_scale * acc
        o_ref[...] = h
        ob_ref[...] = (h * gnext_ref[...]).astype(ob_ref.dtype)
        part_ssq = jnp.sum(h * h, axis=-1, keepdims=True)

        @pl.when(j == 0)
        def _():
            ssq_ref[...] = part_ssq

        @pl.when(j > 0)
        def _():
            ssq_ref[...] += part_ssq


def _ffn_down(a, w, l, res, res_scale, next_g):
    m, kdim = a.shape
    n = w.shape[2]
    tm = _tile(m, 1024, 2 * SUBLANE)
    tn = _tile(n, 512 if tm <= 768 else 256, LANE)
    tk = _tile(kdim, 5632, LANE)
    nk, nn = kdim // tk, n // tn
    last = lambda k, j: jnp.where(k == nk - 1, j, 0)
    tile = pl.BlockSpec((tm, tn), lambda i, k, j: (i, last(k, j)))
    col = pl.BlockSpec((tm, 1), lambda i, k, j: (i, 0))
    return pl.pallas_call(
        functools.partial(_ffn_down_kernel, nk=nk, res_scale=res_scale),
        grid=(m // tm, nk, nn),
        in_specs=[pl.BlockSpec((tm, tk), lambda i, k, j: (i, k)),
                  pl.BlockSpec((1, tk, tn), lambda i, k, j: (l, k, j)),
                  tile,
                  pl.BlockSpec((1, tn), lambda i, k, j: (0, last(k, j)))],
        out_specs=[tile, tile, col],
        out_shape=[jax.ShapeDtypeStruct((m, n), F32), jax.ShapeDtypeStruct((m, n), BF16),
                   jax.ShapeDtypeStruct((m, 1), F32)],
        scratch_shapes=[pltpu.VMEM((nn, tm, tn), F32)],
        compiler_params=_cparams(3),
        name="ffn_down",
    )(a, w, res, next_g.reshape(1, n))


def _ffn(h, hb, ssq, wg, wu, wd, l, next_g):
    a = _swiglu_up(hb, ssq, wg, wu, l)
    return _ffn_down(a, wd, l, h, 0.5, next_g)


def _merge_kernel(ya_ref, yb_ref, yc_ref, wa_ref, wb_ref, wc_ref, ga_ref, gb_ref, gc_ref, o_ref):
    a = jnp.dot(ya_ref[...], wa_ref[0], preferred_element_type=F32)
    b = jnp.dot(yb_ref[...], wb_ref[0], preferred_element_type=F32)
    c = jnp.dot(yc_ref[...], wc_ref[0], preferred_element_type=F32)
    out = ga_ref[...].astype(F32) * a + gb_ref[...].astype(F32) * b + gc_ref[...].astype(F32) * c
    o_ref[...] = out.astype(o_ref.dtype)


def _merge(ya, yb, yc, wa, wb, wc, l, gates):
    m = ya.shape[0]
    d = wa.shape[2]
    tm = _tile(m, ROW_TILE, 2 * SUBLANE)
    tn = _tile(d, 512, LANE)
    nb = d // tn
    wspec = lambda w: pl.BlockSpec((1, w.shape[1], tn), lambda i, j: (l, 0, j))
    return pl.pallas_call(
        _merge_kernel,
        grid=(m // tm, nb),
        in_specs=[pl.BlockSpec((tm, ya.shape[1]), lambda i, j: (i, 0)),
                  pl.BlockSpec((tm, yb.shape[1]), lambda i, j: (i, 0)),
                  pl.BlockSpec((tm, yc.shape[1]), lambda i, j: (i, 0)),
                  wspec(wa), wspec(wb), wspec(wc),
                  pl.BlockSpec((tm, tn), lambda i, j: (i, j)),
                  pl.BlockSpec((tm, tn), lambda i, j: (i, nb + j)),
                  pl.BlockSpec((tm, tn), lambda i, j: (i, 2 * nb + j))],
        out_specs=pl.BlockSpec((tm, tn), lambda i, j: (i, j)),
        out_shape=jax.ShapeDtypeStruct((m, d), BF16),
        compiler_params=_cparams(2),
        name="merge",
    )(ya, yb, yc, wa, wb, wc, gates, gates, gates)


def _rope_kernel(q_ref, k_ref, v_ref, qi_ref, kw_ref, c128_ref, s128_ref, c64_ref, s64_ref,
                 qo_ref, ko_ref, vo_ref, kb_ref, vb_ref, qio_ref, kwo_ref, *, n_heads, n_idx_groups, d_idx,
                 idx_scale):
    c128, s128 = c128_ref[...], s128_ref[...]
    c64, s64 = c64_ref[...], s64_ref[...]
    lane = lax.broadcasted_iota(jnp.int32, c64.shape, 1)
    first_half = (lane % d_idx) < (d_idx // 2)

    def rope128(x):
        return x * c128 + pltpu.roll(x, LANE // 2, axis=1) * s128

    def rope64(x):
        rot = jnp.where(first_half, pltpu.roll(x, LANE - d_idx // 2, axis=1), pltpu.roll(x, d_idx // 2, axis=1))
        return x * c64 + rot * s64

    for h in range(n_heads):
        sl = slice(h * LANE, (h + 1) * LANE)
        qo_ref[0, :, sl] = rope128(q_ref[0, :, sl]).astype(qo_ref.dtype)
        kr = rope128(k_ref[0, :, sl])
        ko_ref[0, :, h, :] = kr
        kb_ref[0, :, sl] = kr.astype(kb_ref.dtype)
        vo_ref[0, :, h, :] = v_ref[0, :, sl]
    vb_ref[0] = v_ref[0].astype(vb_ref.dtype)
    for g in range(n_idx_groups):
        sl = slice(g * LANE, (g + 1) * LANE)
        qio_ref[0, :, sl] = rope64(qi_ref[0, :, sl])
    kw = kw_ref[0]
    kwo_ref[0] = jnp.where(lane < d_idx, rope64(kw), kw * idx_scale)


def _rope_tables(pos, d):
    inv = ROPE_THETA ** (-jnp.arange(0, d, 2, dtype=F32) / d)
    ang = pos.astype(F32)[:, None] * inv[None, :]
    cos, sin = jnp.cos(ang), jnp.sin(ang)
    reps = LANE // d
    c = jnp.tile(jnp.concatenate([cos, cos], axis=-1), (1, reps))
    s = jnp.tile(jnp.concatenate([-sin, sin], axis=-1), (1, reps))
    return c, s


def _rope_all(mid_all, row0, b, t, pos0, dims):
    mid = mid_all.reshape((1,) + mid_all.shape)
    cb, cc, qiw, d_idx, h_idx = dims["C_B"], dims["C_C"], dims["QIW"], dims["D_IDX"], dims["H_IDX"]
    assert dims["DH_C"] == LANE and LANE % d_idx == 0
    assert cb % cc == 0 and (cb + 3 * cc) % qiw == 0
    tm = _tile(t, 256, SUBLANE)
    pos = pos0 + jnp.arange(t)
    c128, s128 = _rope_tables(pos, LANE)
    c64, s64 = _rope_tables(pos, d_idx)
    ob = cb // cc
    assert row0 % tm == 0
    row = lambda width, idx: pl.BlockSpec((1, tm, width), lambda i, j: (0, row0 // tm + i * (t // tm) + j, idx))
    tab = pl.BlockSpec((tm, LANE), lambda i, j: (j, 0))
    out = lambda width: pl.BlockSpec((1, tm, width), lambda i, j: (i, j, 0))
    split = pl.BlockSpec((1, tm, cc // LANE, LANE), lambda i, j: (i, j, 0, 0))
    return pl.pallas_call(
        functools.partial(_rope_kernel, n_heads=cc // LANE, n_idx_groups=qiw // LANE, d_idx=d_idx,
                          idx_scale=(h_idx * d_idx) ** -0.5),
        grid=(b, t // tm),
        in_specs=[row(cc, ob), row(cc, ob + 1), row(cc, ob + 2), row(qiw, (cb + 3 * cc) // qiw),
                  row(LANE, (cb + 3 * cc + qiw) // LANE), tab, tab, tab, tab],
        out_specs=[out(cc), split, split, out(cc), out(cc), out(qiw), out(LANE)],
        out_shape=[jax.ShapeDtypeStruct((b, t, cc), BF16), jax.ShapeDtypeStruct((b, t, cc // LANE, LANE), F32),
                   jax.ShapeDtypeStruct((b, t, cc // LANE, LANE), F32),
                   jax.ShapeDtypeStruct((b, t, cc), BF16), jax.ShapeDtypeStruct((b, t, cc), BF16),
                   jax.ShapeDtypeStruct((b, t, qiw), F32), jax.ShapeDtypeStruct((b, t, LANE), F32)],
        compiler_params=_cparams(2),
        name="rope",
    )(mid, mid, mid, mid, mid, c128, s128, c64, s64)


def _head_sum(x, jmat):
    parts = []
    for g in range(x.shape[1] // LANE):
        parts.append(_dot_exact_rhs(x[:, g * LANE:(g + 1) * LANE], jmat))
    return parts[0] if len(parts) == 1 else jnp.concatenate(parts, axis=1)


def _head_ones():
    r = lax.broadcasted_iota(jnp.int32, (LANE, LANE), 0)
    c = lax.broadcasted_iota(jnp.int32, (LANE, LANE), 1)
    return jnp.where((r // N_A) == (c // N_A), 1.0, 0.0).astype(BF16)


def _rwkv_prep_kernel(pa_ref, halo_ref, shift_ref, mu_ref, w0_ref, w2_ref, a0_ref, a2_ref, g2_ref,
                      kk_ref, ka_ref, rk_ref,
                      rt_ref, at_ref, bt_ref, kt_ref, v_ref, bkt_ref, gct_ref, bonus_ref, g_ref, *, ca, chunk, tm):
    j = pl.program_id(1)
    prev_last = jnp.where(j == 0, shift_ref[0], halo_ref[0, SUBLANE - 1:SUBLANE, :])
    row = lax.broadcasted_iota(jnp.int32, (tm, 1), 0)

    def shifted(lo, hi):
        x = pa_ref[0, :, lo:hi]
        prev = jnp.where(row == 0, prev_last[:, lo:hi], pltpu.roll(x, 1, axis=0))
        return x + mu_ref[:, lo:hi] * (prev - x)

    r = shifted(0, ca)
    k = shifted(ca, 2 * ca)
    v = shifted(2 * ca, 3 * ca)
    wl = shifted(3 * ca, 3 * ca + LANE)
    al = shifted(3 * ca + LANE, 3 * ca + 2 * LANE)
    gl = shifted(3 * ca + 2 * LANE, pa_ref.shape[2])

    z = -(w0_ref[...] + _dot(jnp.tanh(wl), w2_ref[...]))
    softplus = jnp.maximum(z, 0.0) + jnp.log(1.0 + jnp.exp(-jnp.abs(z)))
    log_decay = -jnp.exp(-softplus - 0.5)
    rr = lax.broadcasted_iota(jnp.int32, (tm, tm), 0)
    cc = lax.broadcasted_iota(jnp.int32, (tm, tm), 1)
    same_chunk = (rr // chunk) == (cc // chunk)
    tril = jnp.where(same_chunk & (cc <= rr), 1.0, 0.0).astype(BF16)
    ones_blk = jnp.where(same_chunk, 1.0, 0.0).astype(BF16)
    cum = _dot_exact_lhs(tril, log_decay)
    cum_c = _dot_exact_lhs(ones_blk, log_decay)
    fwd = jnp.exp(cum_c - cum)
    back = jnp.exp(cum - cum_c)
    back_prev = jnp.exp(cum - log_decay - cum_c)
    gam_c = jnp.exp(cum_c)

    a = jax.nn.sigmoid(a0_ref[...] + _dot(al, a2_ref[...]))
    jmat = _head_ones()
    kk = k * kk_ref[...]
    kk = kk / jnp.maximum(jnp.sqrt(_head_sum(kk * kk, jmat)), 1e-12)
    kh = k * (1.0 + (a - 1.0) * ka_ref[...])
    bonus_ref[0] = _head_sum(r * kh * rk_ref[...], jmat) * v
    g_ref[0] = _dot(jax.nn.sigmoid(gl), g2_ref[...])
    bh = kk * a * fwd
    kf = kh * fwd
    rt_ref[0] = (r * back).astype(rt_ref.dtype)
    at_ref[0] = (-kk * back_prev).astype(at_ref.dtype)
    bt_ref[0] = bh.astype(bt_ref.dtype)
    kt_ref[0] = kf.astype(kt_ref.dtype)
    v_ref[0] = v.astype(v_ref.dtype)
    pad = N_A - chunk
    for c in range(tm // chunk):
        rows = slice(c * chunk, (c + 1) * chunk)
        if pad:
            zeros = jnp.zeros((pad, ca), F32)
            stacked = jnp.concatenate([bh[rows], zeros, kf[rows], zeros], axis=0)
        else:
            stacked = jnp.concatenate([bh[rows], kf[rows]], axis=0)
        bkt_ref[0, c] = stacked.T.astype(bkt_ref.dtype)
        gct_ref[0, c] = jnp.broadcast_to(gam_c[c * chunk:c * chunk + 1], (LANE, ca)).T


def _rwkv_prep(pa_all, row0, b, t, shift_p, P, l, dims, chunk):
    paw = pa_all.shape[1]
    pa = pa_all.reshape((1,) + pa_all.shape)
    ca = dims["C_A"]
    tm = chunk * max(1, min(128, t) // chunk)
    assert t % tm == 0 and row0 % tm == 0 and 2 * N_A == LANE
    hb = tm // SUBLANE
    nct = tm // chunk
    row = pl.BlockSpec((1, tm, ca), lambda i, j: (i, j, 0))
    cmaj = pl.BlockSpec((1, nct, ca, LANE), lambda i, j: (i, j, 0, 0))
    vec = lambda w: pl.BlockSpec((1, w), lambda i, j: (0, 0))
    mat = lambda r: pl.BlockSpec((r, ca), lambda i, j: (0, 0))
    tmaj = lambda dt: jax.ShapeDtypeStruct((b, t, ca), dt)
    return pl.pallas_call(
        functools.partial(_rwkv_prep_kernel, ca=ca, chunk=chunk, tm=tm),
        grid=(b, t // tm),
        in_specs=[pl.BlockSpec((1, tm, paw), lambda i, j: (0, row0 // tm + i * (t // tm) + j, 0)),
                  pl.BlockSpec((1, SUBLANE, paw),
                               lambda i, j: (0, jnp.maximum((row0 + i * t) // SUBLANE + j * hb - 1, 0), 0)),
                  pl.BlockSpec((1, 1, paw), lambda i, j: (i, 0, 0)),
                  vec(paw), vec(ca), mat(LANE), vec(ca), mat(LANE), mat(P["g2"].shape[1]),
                  vec(ca), vec(ca), vec(ca)],
        out_specs=[row] * 5 + [cmaj, cmaj, row, row],
        out_shape=[tmaj(BF16)] * 5 + [jax.ShapeDtypeStruct((b, t // chunk, ca, LANE), BF16),
                                      jax.ShapeDtypeStruct((b, t // chunk, ca, LANE), F32), tmaj(F32), tmaj(F32)],
        compiler_params=_cparams(2),
        name="rwkv_prep",
    )(pa, pa, shift_p.reshape(b, 1, paw), P["mu"][l], P["w0"][l], P["w2"][l], P["a0"][l], P["a2"][l],
      P["g2"][l], P["k_k"][l], P["k_a"][l], P["r_k"][l])


def _rwkv_scan_kernel(at_ref, rt_ref, v_ref, bt_ref, kt_ref, bkt_ref, gct_ref, s0_ref, y_ref, s_ref, *,
                      n_pairs, chunk, unroll):
    c = pl.program_id(1)

    @pl.when(c == 0)
    def _():
        s_ref[...] = s0_ref[...]

    rr = lax.broadcasted_iota(jnp.int32, (chunk, chunk), 0)
    cc = lax.broadcasted_iota(jnp.int32, (chunk, chunk), 1)
    lower_strict = cc < rr
    lower_incl = cc <= rr
    eye = jnp.where(rr == cc, 1.0, 0.0)
    first = lax.broadcasted_iota(jnp.int32, (chunk, LANE), 1) < N_A
    br = lax.broadcasted_iota(jnp.int32, (LANE, LANE), 0)
    bc = lax.broadcasted_iota(jnp.int32, (LANE, LANE), 1)
    block_diag = (br // N_A) == (bc // N_A)
    n_double = max(int(math.log2(chunk)) - 1, 0)
    pad = N_A - chunk

    def load(p):
        sl = pl.ds(pl.multiple_of(p * LANE, LANE), LANE)
        return (sl,
                at_ref[0, :, sl],
                rt_ref[0, :, sl],
                v_ref[0, :, sl],
                bt_ref[0, :, sl],
                kt_ref[0, :, sl],
                bkt_ref[0, 0, sl, :],
                s_ref[0, sl, :] * gct_ref[0, 0, sl, :])

    def group(g, carry):
        loaded = [load(g * unroll + j) for j in range(unroll)]
        sls, a, r, v, b, k, bk_t, sb = (list(col) for col in zip(*loaded))
        pairs = range(unroll)
        heads = [(i, hh) for i in pairs for hh in range(2)]
        own = lambda hh: first if hh == 0 else jnp.logical_not(first)
        sb16 = [sb[i].astype(BF16) for i in pairs]
        x = [jnp.dot(a[i], sb16[i], preferred_element_type=F32) for i in pairs]
        y = [jnp.dot(r[i], sb16[i], preferred_element_type=F32) for i in pairs]
        a_h = [jnp.where(own(hh), a[i], jnp.zeros_like(a[i])) for i, hh in heads]
        r_h = [jnp.where(own(hh), r[i], jnp.zeros_like(r[i])) for i, hh in heads]
        n_ab = [jnp.where(lower_strict, _dot_nt(a_h[j], b[i]), 0.0) for j, (i, _) in enumerate(heads)]
        n_ak = [jnp.where(lower_strict, _dot_nt(a_h[j], k[i]), 0.0) for j, (i, _) in enumerate(heads)]
        m_rb = [jnp.where(lower_incl, _dot_nt(r_h[j], b[i]), 0.0) for j, (i, _) in enumerate(heads)]
        m_rk = [jnp.where(lower_incl, _dot_nt(r_h[j], k[i]), 0.0) for j, (i, _) in enumerate(heads)]
        inv = [eye + n for n in n_ab]
        npow = n_ab
        for _ in range(n_double):
            npow = [_dot(n, n) for n in npow]
            inv = [iv + _dot(iv, n) for iv, n in zip(inv, npow)]
        w = [x[i] + _dot(n_ak[j], v[i]) for j, (i, _) in enumerate(heads)]
        u = [_dot(inv[j], w[j]) for j in range(len(heads))]
        y_h = [_dot(m_rb[j], u[j]) + _dot(m_rk[j], v[i]) for j, (i, _) in enumerate(heads)]
        for i in pairs:
            u_p = jnp.where(first, u[2 * i], u[2 * i + 1])
            y_new = y[i] + jnp.where(first, y_h[2 * i], y_h[2 * i + 1])
            v32 = v[i].astype(F32)
            if pad:
                zeros = jnp.zeros((pad, LANE), F32)
                stacked = jnp.concatenate([u_p, zeros, v32, zeros], axis=0)
            else:
                stacked = jnp.concatenate([u_p, v32], axis=0)
            s_new = sb[i] + jnp.where(block_diag, _dot(bk_t[i], stacked), 0.0)
            y_ref[0, :, sls[i]] = y_new
            s_ref[0, sls[i], :] = s_new
        return carry

    lax.fori_loop(0, n_pairs // unroll, group, 0)


def _rwkv_scan(rt, at, bt, kt, v, bkt, gct, s0, chunk):
    b, t, ca = rt.shape
    h = ca // N_A
    nc = t // chunk
    eye2 = jnp.eye(2, dtype=F32)
    s0t = jnp.swapaxes(s0, -1, -2).reshape(b, h // 2, 2, N_A, 1, N_A)
    sb0 = (s0t * eye2[None, None, :, None, :, None]).reshape(b, ca, LANE)
    tmaj = pl.BlockSpec((1, chunk, ca), lambda i, j: (i, j, 0))
    cmaj = pl.BlockSpec((1, 1, ca, LANE), lambda i, j: (i, j, 0, 0))
    st = pl.BlockSpec((1, ca, LANE), lambda i, j: (i, 0, 0))
    y, sb = pl.pallas_call(
        functools.partial(_rwkv_scan_kernel, n_pairs=h // 2, chunk=chunk, unroll=_tile(h // 2, 16, 1)),
        grid=(b, nc),
        in_specs=[tmaj, tmaj, tmaj, tmaj, tmaj, cmaj, cmaj, st],
        out_specs=[tmaj, st],
        out_shape=[jax.ShapeDtypeStruct((b, t, ca), F32), jax.ShapeDtypeStruct((b, ca, LANE), F32)],
        compiler_params=_cparams(2),
        name="rwkv_scan",
    )(at, rt, v, bt, kt, bkt, gct, sb0)
    sb = sb.reshape(b, h // 2, 2, N_A, 2, N_A)
    s_t = jnp.stack([sb[:, :, 0, :, 0, :], sb[:, :, 1, :, 1, :]], axis=2).reshape(b, h, N_A, N_A)
    return y, jnp.swapaxes(s_t, -1, -2)


def _rwkv_post_kernel(y_ref, bonus_ref, g_ref, lnw_ref, lnb_ref, o_ref):
    jmat = _head_ones()
    y = y_ref[...]
    mean = _head_sum(y, jmat) * (1.0 / N_A)
    d = y - mean
    var = _head_sum(d * d, jmat) * (1.0 / N_A)
    out = (d * lax.rsqrt(var + GN_EPS) * lnw_ref[...] + lnb_ref[...] + bonus_ref[...]) * g_ref[...]
    o_ref[...] = out.astype(o_ref.dtype)


def _rwkv_post(y, bonus, g, ln_w, ln_b):
    m, ca = y.shape
    tm = _tile(m, 256, SUBLANE)
    row = pl.BlockSpec((tm, ca), lambda i: (i, 0))
    vec = pl.BlockSpec((1, ca), lambda i: (0, 0))
    return pl.pallas_call(
        _rwkv_post_kernel,
        grid=(m // tm,),
        in_specs=[row, row, row, vec, vec],
        out_specs=row,
        out_shape=jax.ShapeDtypeStruct((m, ca), BF16),
        compiler_params=_cparams(1),
        name="rwkv_post",
    )(y, bonus, g, ln_w, ln_b)


def _pool_kernel(z_ref, hist_ref, w_ref, scale_ref, o_ref, x_scr, *, t, tm, group, pos0):
    halo = 2 * SUBLANE
    x_scr[0:halo, :] = hist_ref[0]
    x_scr[halo:halo + t, :] = z_ref[0]
    for i in range(t // tm):
        r0 = i * tm
        pos = pos0 + r0 + lax.broadcasted_iota(jnp.int32, (tm, 1), 0)
        for gi, win in enumerate(POOL_WINDOWS):
            lo, hi = gi * group, (gi + 1) * group
            cur = x_scr[halo + r0:halo + r0 + tm, lo:hi]
            tot = cur
            for back in range(1, win):
                tot = tot + x_scr[halo + r0 - back:halo + r0 - back + tm, lo:hi]
            cnt = jnp.minimum(pos + 1, win).astype(F32)
            d = tot / cnt - cur
            y = _dot(d, w_ref[gi]) * scale_ref[:, lo:hi]
            o_ref[0, r0:r0 + tm, lo:hi] = y.astype(o_ref.dtype)


def _pool(mid_all, row0, b, t, hist16, pool_w, pool_scale, pos0, dims):
    assert row0 % t == 0
    mid = mid_all.reshape((1,) + mid_all.shape)
    cb = dims["C_B"]
    group = pool_w.shape[1]
    assert len(POOL_WINDOWS) * group == cb
    tm = _tile(t, 256, SUBLANE)
    return pl.pallas_call(
        functools.partial(_pool_kernel, t=t, tm=tm, group=group, pos0=pos0),
        grid=(b,),
        in_specs=[pl.BlockSpec((1, t, cb), lambda i: (0, row0 // t + i, 0)),
                  pl.BlockSpec((1, 2 * SUBLANE, cb), lambda i: (i, 0, 0)),
                  pl.BlockSpec(pool_w.shape, lambda i: (0, 0, 0)),
                  pl.BlockSpec((1, cb), lambda i: (0, 0))],
        out_specs=pl.BlockSpec((1, t, cb), lambda i: (i, 0, 0)),
        out_shape=jax.ShapeDtypeStruct((b, t, cb), BF16),
        scratch_shapes=[pltpu.VMEM((t + 2 * SUBLANE, cb), F32)],
        compiler_params=_cparams(1),
        name="pool",
    )(mid, hist16, pool_w, pool_scale.reshape(1, cb))


def _monotone_key(x):
    x = jnp.where(x == 0.0, 0.0, x)
    bits = lax.bitcast_convert_type(x, jnp.int32)
    return jnp.where(bits < 0, bits ^ jnp.int32(0x7FFFFFFF), bits)


def _kth_largest_key(key, k):
    def body(i, tau):
        cand = tau + jnp.left_shift(jnp.int32(1), jnp.int32(31) - i)
        cnt = jnp.sum(jnp.where(key >= cand, 1.0, 0.0), axis=-1, keepdims=True)
        return jnp.where(cnt >= k, cand, tau)

    tau0 = jnp.full((key.shape[0], 1), -2 ** 31, jnp.int32)
    return lax.fori_loop(0, 32, body, tau0)


def _prefix_count(ind):
    r = lax.broadcasted_iota(jnp.int32, (LANE, LANE), 0)
    c = lax.broadcasted_iota(jnp.int32, (LANE, LANE), 1)
    tri = jnp.where(r <= c, 1.0, 0.0).astype(BF16)
    run = jnp.zeros((ind.shape[0], 1), F32)
    outs = []
    for j in range(ind.shape[1] // LANE):
        pj = jnp.dot(ind[:, j * LANE:(j + 1) * LANE].astype(BF16), tri, preferred_element_type=F32)
        outs.append(pj + run)
        run = run + pj[:, LANE - 1:LANE]
    return outs[0] if len(outs) == 1 else jnp.concatenate(outs, axis=1)


def _topk_mask(scores, k):
    key = _monotone_key(scores)
    tau = _kth_largest_key(key, float(k))
    gt = key > tau
    eq = key == tau
    need = float(k) - jnp.sum(jnp.where(gt, 1.0, 0.0), axis=-1, keepdims=True)
    prefix = _prefix_count(jnp.where(eq, 1.0, 0.0))
    return gt | (eq & (prefix <= need))


def _dsa_prompt_kernel(q_ref, k_ref, v_ref, qi_ref, ki_ref, wi_ref, o_ref, key_s, sc_s, m_s, l_s, acc_s, *,
                       tq, t, kc, topk, h_idx, d_idx, n_heads, dh):
    q0 = pl.program_id(1) * tq
    n_kc = t // kc
    needed = (q0 + tq + kc - 1) // kc
    qpos = q0 + lax.broadcasted_iota(jnp.int32, (tq, 1), 0)
    qi = [qi_ref[0, :, h * d_idx:(h + 1) * d_idx].astype(BF16) for h in range(h_idx)]
    wi = [wi_ref[0, :, h:h + 1] for h in range(h_idx)]

    def score_chunk(c, carry):
        keys = ki_ref[0, pl.ds(pl.multiple_of(c * kc, kc), kc), :].astype(BF16)
        acc = jnp.zeros((tq, kc), F32)
        for h in range(h_idx):
            acc = acc + jnp.maximum(_dot_nt(qi[h], keys), 0.0) * wi[h]
        spos = c * kc + lax.broadcasted_iota(jnp.int32, (1, kc), 1)
        key_s[c] = _monotone_key(jnp.where(spos <= qpos, acc, -jnp.inf))
        return carry

    lax.fori_loop(0, needed, score_chunk, 0)

    def count(pred):
        def body(c, tot):
            ind = jnp.where(pred(key_s[c]), 1.0, 0.0)
            for j in range(kc // LANE):
                tot = tot + ind[:, j * LANE:(j + 1) * LANE]
            return tot
        lanes = lax.fori_loop(0, needed, body, jnp.zeros((tq, LANE), F32))
        return jnp.sum(lanes, axis=-1, keepdims=True)

    def search(i, tau):
        cand = tau + jnp.left_shift(jnp.int32(1), jnp.int32(31) - i)
        return jnp.where(count(lambda key: key >= cand) >= float(topk), cand, tau)

    tau = lax.fori_loop(0, 32, search, jnp.full((tq, 1), -2 ** 31, jnp.int32))
    need = float(topk) - count(lambda key: key > tau)

    def mask_chunk(c, run):
        key = key_s[c]
        eq = jnp.where(key == tau, 1.0, 0.0)
        prefix = _prefix_count(eq) + run
        spos = c * kc + lax.broadcasted_iota(jnp.int32, (1, kc), 1)
        sel = (key > tau) | ((key == tau) & (prefix <= need))
        sc_s[c] = jnp.where(sel & (spos <= qpos), 1.0, 0.0)
        return prefix[:, kc - 1:kc]

    lax.fori_loop(0, needed, mask_chunk, jnp.zeros((tq, 1), F32))

    m_s[...] = jnp.full(m_s.shape, NEG_BIG, F32)
    l_s[...] = jnp.zeros(l_s.shape, F32)
    acc_s[...] = jnp.zeros(acc_s.shape, F32)
    scale = dh ** -0.5

    def attn_chunk(c, carry):
        rows = pl.ds(pl.multiple_of(c * kc, kc), kc)
        ok = sc_s[c] > 0.5
        heads = range(n_heads)
        sls = [slice(h * dh, (h + 1) * dh) for h in heads]
        lg = [_dot_nt(q_ref[0, :, sls[h]], k_ref[0, rows, sls[h]]) for h in heads]
        lg = [jnp.where(ok, x * scale, NEG_BIG) for x in lg]
        m_old = [m_s[h] for h in heads]
        m_new = [jnp.maximum(m_old[h], jnp.max(lg[h], axis=-1, keepdims=True)) for h in heads]
        p = [jnp.where(ok, jnp.exp(lg[h] - m_new[h]), 0.0) for h in heads]
        pv = [_dot(p[h], v_ref[0, rows, sls[h]]) for h in heads]
        for h in heads:
            alpha = jnp.exp(m_old[h] - m_new[h])
            l_s[h] = alpha * l_s[h] + jnp.sum(p[h], axis=-1, keepdims=True)
            acc_s[:, sls[h]] = alpha * acc_s[:, sls[h]] + pv[h]
            m_s[h] = m_new[h]
        return carry

    lax.fori_loop(0, needed, attn_chunk, 0)
    for h in range(n_heads):
        sl = slice(h * dh, (h + 1) * dh)
        o_ref[0, :, sl] = (acc_s[:, sl] / l_s[h]).astype(o_ref.dtype)


def _dsa_prompt(q, k, v, qi, ki, wi, dims):
    b, t, cc = q.shape
    tq = _tile(t, 128, SUBLANE)
    kc = _tile(t, 512, tq)
    topk = min(TOPK_MAX, t // 4)
    assert kc >= topk
    h_idx, d_idx, n_heads = dims["H_IDX"], dims["D_IDX"], dims["H_C"]
    qb = lambda w: pl.BlockSpec((1, tq, w), lambda i, j: (i, j, 0))
    full = lambda w: pl.BlockSpec((1, t, w), lambda i, j: (i, 0, 0))
    return pl.pallas_call(
        functools.partial(_dsa_prompt_kernel, tq=tq, t=t, kc=kc, topk=topk, h_idx=h_idx, d_idx=d_idx,
                          n_heads=n_heads, dh=dims["DH_C"]),
        grid=(b, t // tq),
        in_specs=[qb(cc), full(cc), full(cc), qb(h_idx * d_idx), full(d_idx), qb(h_idx)],
        out_specs=qb(cc),
        out_shape=jax.ShapeDtypeStruct((b, t, cc), BF16),
        scratch_shapes=[pltpu.VMEM((t // kc, tq, kc), jnp.int32), pltpu.VMEM((t // kc, tq, kc), F32),
                        pltpu.VMEM((n_heads, tq, 1), F32),
                        pltpu.VMEM((n_heads, tq, 1), F32), pltpu.VMEM((tq, cc), F32)],
        compiler_params=_cparams(2),
        name="dsa_prompt",
    )(q, k, v, qi, ki, wi)


def _idx_rows_scores(qi2, wi2, keys, tq, h_idx):
    s = jnp.maximum(_dot_nt(qi2, keys), 0.0) * wi2
    acc = s[0:tq]
    for h in range(1, h_idx):
        acc = acc + s[h * tq:(h + 1) * tq]
    return acc


def _dsa_sample_scores_kernel(pt_ref, qi_ref, wi_ref, *refs, tq, h_idx):
    o_ref = refs[-1]
    for g, kc_ref in enumerate(refs[:-1]):
        o_ref[0, :, g * PAGE:(g + 1) * PAGE] = _idx_rows_scores(qi_ref[0], wi_ref[0], kc_ref[0, 0], tq, h_idx)


def _dsa_sample_scores(page_table, qi2, wi2, cache_kidx, l, tq, h_idx):
    b, n_pages = page_table.shape
    d_idx = cache_kidx.shape[-1]
    rows = qi2.shape[1]
    group = _tile(n_pages, 16, 1)
    page = lambda g: pl.BlockSpec((1, 1, PAGE, d_idx), lambda i, p, pt: (l, pt[i, p * group + g], 0, 0))
    grid_spec = pltpu.PrefetchScalarGridSpec(
        num_scalar_prefetch=1,
        grid=(b, n_pages // group),
        in_specs=[pl.BlockSpec((1, rows, d_idx), lambda i, p, pt: (i, 0, 0)),
                  pl.BlockSpec((1, rows, 1), lambda i, p, pt: (i, 0, 0))] + [page(g) for g in range(group)],
        out_specs=pl.BlockSpec((1, tq, group * PAGE), lambda i, p, pt: (i, 0, p)),
    )
    return pl.pallas_call(
        functools.partial(_dsa_sample_scores_kernel, tq=tq, h_idx=h_idx),
        grid_spec=grid_spec,
        out_shape=jax.ShapeDtypeStruct((b, tq, n_pages * PAGE), F32),
        compiler_params=_cparams(2),
        name="dsa_sample_scores",
    )(page_table, qi2, wi2, *([cache_kidx] * group))


def _dsa_sample_select_kernel(sc_ref, qi_ref, wi_ref, kn_ref, o_ref, on_ref, *, tq, h_idx, topk, past):
    new = _idx_rows_scores(qi_ref[0], wi_ref[0], kn_ref[0], tq, h_idx)
    qrow = lax.broadcasted_iota(jnp.int32, (tq, PAGE), 0)
    col = lax.broadcasted_iota(jnp.int32, (tq, PAGE), 1)
    new_ok = col <= qrow
    scores = jnp.concatenate([sc_ref[0], jnp.where(new_ok, new, -jnp.inf)], axis=1)
    sel = jnp.where(_topk_mask(scores, topk), 1.0, 0.0)
    o_ref[0] = sel[:, :past]
    on_ref[0] = jnp.where(new_ok, sel[:, past:], 0.0)


def _dsa_sample_select(scores, qi2, wi2, ki_new_pad, tq, h_idx):
    b, _, past = scores.shape
    topk = min(TOPK_MAX, (past + tq) // 4)
    rows, d_idx = qi2.shape[1], qi2.shape[2]
    return pl.pallas_call(
        functools.partial(_dsa_sample_select_kernel, tq=tq, h_idx=h_idx, topk=topk, past=past),
        grid=(b,),
        in_specs=[pl.BlockSpec((1, tq, past), lambda i: (i, 0, 0)),
                  pl.BlockSpec((1, rows, d_idx), lambda i: (i, 0, 0)),
                  pl.BlockSpec((1, rows, 1), lambda i: (i, 0, 0)),
                  pl.BlockSpec((1, PAGE, d_idx), lambda i: (i, 0, 0))],
        out_specs=[pl.BlockSpec((1, tq, past), lambda i: (i, 0, 0)), pl.BlockSpec((1, tq, PAGE), lambda i: (i, 0, 0))],
        out_shape=[jax.ShapeDtypeStruct((b, tq, past), F32), jax.ShapeDtypeStruct((b, tq, PAGE), F32)],
        compiler_params=_cparams(1),
        name="dsa_sample_select",
    )(scores, qi2, wi2, ki_new_pad)


def _dsa_sample_attn_kernel(pt_ref, q_ref, kn_ref, vn_ref, m_ref, mn_ref, *refs, n_steps, group, n_heads, dh, tq):
    kc_refs, vc_refs = refs[:group], refs[group:2 * group]
    o_ref, q2_s, exp_s, hm_s, m_s, l_s, acc_s = refs[2 * group:]
    p = pl.program_id(1)
    rows = n_heads * tq
    cols = PAGE * n_heads

    @pl.when(p == 0)
    def _():
        for h in range(n_heads):
            q2_s[h * tq:(h + 1) * tq, :] = q_ref[0, :, h * dh:(h + 1) * dh]
        pos = lax.broadcasted_iota(jnp.int32, (PAGE, cols), 0)
        col = lax.broadcasted_iota(jnp.int32, (PAGE, cols), 1)
        exp_s[...] = jnp.where(col // n_heads == pos, 1.0, 0.0).astype(exp_s.dtype)
        rr = lax.broadcasted_iota(jnp.int32, (rows, cols), 0)
        cc = lax.broadcasted_iota(jnp.int32, (rows, cols), 1)
        hm_s[...] = jnp.where((cc % n_heads) == (rr // tq), 1.0, 0.0)
        m_s[...] = jnp.full(m_s.shape, NEG_BIG, F32)
        l_s[...] = jnp.zeros(l_s.shape, F32)
        acc_s[...] = jnp.zeros(acc_s.shape, F32)

    def update(kmat, vmat, sel):
        lg = _dot_nt(q2_s[...], kmat) * (dh ** -0.5)
        sel8 = jnp.dot(sel.astype(BF16), exp_s[...], preferred_element_type=F32)
        msk = jnp.concatenate([sel8] * n_heads, axis=0) * hm_s[...] > 0.5
        lgm = jnp.where(msk, lg, NEG_BIG)
        m_new = jnp.maximum(m_s[...], jnp.max(lgm, axis=-1, keepdims=True))
        alpha = jnp.exp(m_s[...] - m_new)
        pm = jnp.where(msk, jnp.exp(lgm - m_new), 0.0)
        l_s[...] = alpha * l_s[...] + jnp.sum(pm, axis=-1, keepdims=True)
        acc_s[...] = alpha * acc_s[...] + _dot(pm, vmat)
        m_s[...] = m_new

    @pl.when(p < n_steps)
    def _():
        for g in range(group):
            update(kc_refs[g][0, 0].reshape(cols, dh), vc_refs[g][0, 0].reshape(cols, dh),
                   m_ref[0, :, g * PAGE:(g + 1) * PAGE])

    @pl.when(p == n_steps)
    def _():
        update(kn_ref[0].reshape(cols, dh), vn_ref[0].reshape(cols, dh), mn_ref[0])
        res = acc_s[...] / l_s[...]
        for h in range(n_heads):
            o_ref[0, :, h * dh:(h + 1) * dh] = res[h * tq:(h + 1) * tq, :].astype(o_ref.dtype)


def _dsa_sample_attn(page_table, q, cache_k, cache_v, k_new_pad, v_new_pad, mask, mask_new, l, dims):
    b, n_pages = page_table.shape
    tq = q.shape[1]
    n_heads, dh = dims["H_C"], dims["DH_C"]
    cc = n_heads * dh
    group = _tile(n_pages, 8, 1)
    n_steps = n_pages // group
    page = lambda g: pl.BlockSpec(
        (1, 1, PAGE, n_heads, dh),
        lambda i, p, pt: (l, pt[i, jnp.minimum(p * group + g, n_pages - 1)], 0, 0, 0))
    new = pl.BlockSpec((1, PAGE, n_heads, dh), lambda i, p, pt: (i, 0, 0, 0))
    grid_spec = pltpu.PrefetchScalarGridSpec(
        num_scalar_prefetch=1,
        grid=(b, n_steps + 1),
        in_specs=[pl.BlockSpec((1, tq, cc), lambda i, p, pt: (i, 0, 0)), new, new,
                  pl.BlockSpec((1, tq, group * PAGE), lambda i, p, pt: (i, 0, jnp.minimum(p, n_steps - 1))),
                  pl.BlockSpec((1, tq, PAGE), lambda i, p, pt: (i, 0, 0))]
                 + [page(g) for g in range(group)] * 2,
        out_specs=pl.BlockSpec((1, tq, cc), lambda i, p, pt: (i, 0, 0)),
        scratch_shapes=[pltpu.VMEM((n_heads * tq, dh), BF16), pltpu.VMEM((PAGE, PAGE * n_heads), BF16),
                        pltpu.VMEM((n_heads * tq, PAGE * n_heads), F32), pltpu.VMEM((n_heads * tq, 1), F32),
                        pltpu.VMEM((n_heads * tq, 1), F32), pltpu.VMEM((n_heads * tq, dh), F32)],
    )
    return pl.pallas_call(
        functools.partial(_dsa_sample_attn_kernel, n_steps=n_steps, group=group, n_heads=n_heads, dh=dh, tq=tq),
        grid_spec=grid_spec,
        out_shape=jax.ShapeDtypeStruct((b, tq, cc), BF16),
        compiler_params=_cparams(2),
        name="dsa_sample_attn",
    )(page_table, q, k_new_pad, v_new_pad, mask, mask_new, *([cache_k] * group), *([cache_v] * group))


def _dsa_sample(q, k_f32, v_f32, qi, ki, wi, cache_k, cache_v, cache_kidx, page_table, l, dims):
    b, tq, cc = q.shape
    h_idx, d_idx, n_heads, dh = dims["H_IDX"], dims["D_IDX"], dims["H_C"], dims["DH_C"]
    qi2 = jnp.transpose(qi.reshape(b, tq, h_idx, d_idx), (0, 2, 1, 3)).reshape(b, h_idx * tq, d_idx)
    wi2 = jnp.transpose(wi, (0, 2, 1)).reshape(b, h_idx * tq, 1)
    pad_rows = lambda x: jnp.pad(x, ((0, 0), (0, PAGE - tq)) + ((0, 0),) * (x.ndim - 2))
    scores = _dsa_sample_scores(page_table, qi2, wi2, cache_kidx, l, tq, h_idx)
    mask, mask_new = _dsa_sample_select(scores, qi2, wi2, pad_rows(ki), tq, h_idx)
    return _dsa_sample_attn(page_table, q, cache_k, cache_v, pad_rows(k_f32), pad_rows(v_f32), mask, mask_new, l,
                            dims)


def _prepare_weights(raw, dims):
    ca, cb, cc, qiw, d_idx, h_idx, d = (dims[k] for k in ("C_A", "C_B", "C_C", "QIW", "D_IDX", "H_IDX", "D"))
    dd, da, dg = dims["D_DECAY"], dims["D_AAA"], dims["D_GATE"]
    assert dd <= LANE and da <= LANE and dg % LANE == 0
    a_cols = 3 * ca + dd + da + dg
    o = [0, a_cols, a_cols + cb, a_cols + cb + cc, a_cols + cb + 2 * cc, a_cols + cb + 3 * cc]
    o += [o[-1] + qiw, o[-1] + qiw + d_idx, o[-1] + qiw + d_idx + h_idx]
    w_in = raw["w_in"]

    def pack_pa(x):
        zeros = lambda n: jnp.zeros(x.shape[:-1] + (n,), x.dtype)
        return jnp.concatenate([x[..., :3 * ca], x[..., 3 * ca:3 * ca + dd], zeros(LANE - dd),
                                x[..., 3 * ca + dd:3 * ca + dd + da], zeros(LANE - da),
                                x[..., 3 * ca + dd + da:a_cols]], axis=-1)

    assert d_idx + h_idx <= LANE and cb % LANE == 0 and cc % LANE == 0 and qiw % LANE == 0 and d % LANE == 0
    w_in_t = jnp.swapaxes(w_in, 1, 2)
    zero_rows = lambda n: jnp.zeros((w_in.shape[0], n, w_in.shape[1]), w_in.dtype)
    w_pa = jnp.concatenate([w_in_t[:, :3 * ca + dd], zero_rows(LANE - dd), w_in_t[:, 3 * ca + dd:3 * ca + dd + da],
                            zero_rows(LANE - da), w_in_t[:, 3 * ca + dd + da:a_cols]], axis=1).astype(BF16)
    dims["MID_ROWS"] = (o[1], cb + 3 * cc + qiw + LANE)
    dims["GATE_ROWS"] = (o[8], 3 * d)
    assert o[1] + dims["MID_ROWS"][1] <= w_in.shape[2]
    pad_rows = lambda x: jnp.pad(x, ((0, 0), (0, LANE - x.shape[1]), (0, 0)))
    depth = w_in.shape[0]
    vec = lambda x: x.reshape(depth, 1, -1)
    P = dict(
        w_pa=w_pa, w_in_t=w_in_t,
        mu=vec(pack_pa(raw["rwkv_mu"])), w0=vec(raw["rwkv_w0"]), w2=pad_rows(raw["rwkv_w2"]).astype(BF16),
        a0=vec(raw["rwkv_a0"]), a2=pad_rows(raw["rwkv_a2"]).astype(BF16), g2=raw["rwkv_g2"].astype(BF16),
        k_k=vec(raw["rwkv_k_k"]), k_a=vec(raw["rwkv_k_a"]), r_k=vec(raw["rwkv_r_k"]),
        ln_w=vec(raw["rwkv_ln_w"]), ln_b=vec(raw["rwkv_ln_b"]),
        pool_w=raw["pool_w"].astype(BF16), pool_scale=raw["pool_scale"],
    )
    for name in ("ffn1_w_down", "ffn2_w_down", "w_br_a", "w_br_b", "w_br_c", "w_out"):
        P[name] = raw[name].astype(BF16)
    for name in ("ffn1_w_gate", "ffn1_w_up", "ffn2_w_gate", "ffn2_w_up", "ffn1_norm", "mix_norm", "ffn2_norm"):
        P[name] = raw[name]
    P["pack_pa"] = pack_pa
    return P


def _unpack_pa(x, dims):
    ca, dd, da = dims["C_A"], dims["D_DECAY"], dims["D_AAA"]
    return jnp.concatenate([x[..., :3 * ca], x[..., 3 * ca:3 * ca + dd],
                            x[..., 3 * ca + LANE:3 * ca + LANE + da], x[..., 3 * ca + 2 * LANE:]], axis=-1)


def _branches(pa, mid, grp, P, l, dims):
    row0, b, t, pos0 = grp["row0"], grp["b"], grp["t"], grp["pos0"]
    m = b * t
    ca, cb, cc = dims["C_A"], dims["C_B"], dims["C_C"]

    def last_rows(x, n, width):
        idx = (row0 + t - n + jnp.arange(b)[:, None] * t + jnp.arange(n)[None, :]).reshape(-1)
        return jnp.take(x, idx, axis=0)[:, :width].reshape(b, n, width)

    chunk = min(N_A, t)
    rt, at, bt, kt, v_a, bkt, gct, bonus, g = _rwkv_prep(pa, row0, b, t, P["pack_pa"](grp["shift"]), P, l, dims,
                                                          chunk)
    y_a, new_wkv = _rwkv_scan(rt, at, bt, kt, v_a, bkt, gct, grp["wkv"], chunk)
    ya = _rwkv_post(y_a.reshape(m, ca), bonus.reshape(m, ca), g.reshape(m, ca), P["ln_w"][l], P["ln_b"][l])
    new_shift = _unpack_pa(last_rows(pa, 1, pa.shape[1])[:, 0], dims)

    hist16 = jnp.pad(grp["pool"], ((0, 0), (2 * SUBLANE - POOL_HIST, 0), (0, 0)))
    yb = _pool(mid, row0, b, t, hist16, P["pool_w"][l], P["pool_scale"][l], pos0, dims)
    new_pool = jnp.concatenate([grp["pool"], last_rows(mid, min(t, POOL_HIST), cb)], axis=1)[:, -POOL_HIST:]

    q, k, v, k_bf, v_bf, qi, kiwi = _rope_all(mid, row0, b, t, pos0, dims)
    ki = kiwi[..., :dims["D_IDX"]]
    wi = kiwi[..., dims["D_IDX"]:dims["D_IDX"] + dims["H_IDX"]]
    if grp["cache"] is None:
        yc = _dsa_prompt(q, k_bf, v_bf, qi, ki, wi, dims)
    else:
        cache_k, cache_v, cache_kidx, page_table = grp["cache"]
        yc = _dsa_sample(q, k, v, qi, ki, wi, cache_k, cache_v, cache_kidx, page_table, l, dims)
    return (ya, yb.reshape(m, cb), yc.reshape(m, cc)), (k, v, ki, new_wkv, new_shift, new_pool)


def _layer(carry, groups, P, l, next_g, dims):
    h, hb, ssq = carry
    h, u, ssq = _ffn(h, hb, ssq, P["ffn1_w_gate"], P["ffn1_w_up"], P["ffn1_w_down"], l, P["mix_norm"][l])
    pa = _mm(u, P["w_pa"], l, out_dtype=F32, ssq=ssq, w_t=True, name="proj_pa")
    mid = _mm(u, P["w_in_t"], l, out_dtype=F32, ssq=ssq, tm_target=1024, w_t=True, w_rows=dims["MID_ROWS"],
              name="proj_mid")
    gates = _mm(u, P["w_in_t"], l, out_dtype=BF16, ssq=ssq, act="sigmoid", tn_target=512, w_t=True,
                w_rows=dims["GATE_ROWS"], name="proj_gate")
    outs = [_branches(pa, mid, grp, P, l, dims) for grp in groups]
    ya, yb, yc = (jnp.concatenate([o[0][i] for o in outs], axis=0) for i in range(3))
    merged = _merge(ya, yb, yc, P["w_br_a"], P["w_br_b"], P["w_br_c"], l, gates)
    h, hb, ssq = _mm(merged, P["w_out"], l, out_dtype=F32, res=h, next_g=P["ffn2_norm"][l], tn_target=512,
                     name="w_out")
    carry = _ffn(h, hb, ssq, P["ffn2_w_gate"], P["ffn2_w_up"], P["ffn2_w_down"], l, next_g)
    return carry, [o[1] for o in outs]


def kernel(x_prompt, x_sample, cache_k, cache_v, cache_kidx, state_wkv, state_shift, state_pool, page_table,
           ffn1_norm, ffn1_w_gate, ffn1_w_up, ffn1_w_down, mix_norm, w_in, rwkv_mu, rwkv_w0, rwkv_w2,
           rwkv_a0, rwkv_a2, rwkv_g2, rwkv_k_k, rwkv_k_a, rwkv_r_k, rwkv_ln_w, rwkv_ln_b, w_br_a, pool_w,
           pool_scale, w_br_b, w_br_c, w_out, ffn2_norm, ffn2_w_gate, ffn2_w_up, ffn2_w_down, final_norm):
    raw = dict(ffn1_norm=ffn1_norm, ffn1_w_gate=ffn1_w_gate, ffn1_w_up=ffn1_w_up, ffn1_w_down=ffn1_w_down,
               mix_norm=mix_norm, w_in=w_in, rwkv_mu=rwkv_mu, rwkv_w0=rwkv_w0, rwkv_w2=rwkv_w2, rwkv_a0=rwkv_a0,
               rwkv_a2=rwkv_a2, rwkv_g2=rwkv_g2, rwkv_k_k=rwkv_k_k, rwkv_k_a=rwkv_k_a, rwkv_r_k=rwkv_r_k,
               rwkv_ln_w=rwkv_ln_w, rwkv_ln_b=rwkv_ln_b, w_br_a=w_br_a, pool_w=pool_w, pool_scale=pool_scale,
               w_br_b=w_br_b, w_br_c=w_br_c, w_out=w_out, ffn2_norm=ffn2_norm, ffn2_w_gate=ffn2_w_gate,
               ffn2_w_up=ffn2_w_up, ffn2_w_down=ffn2_w_down)
    depth, d = mix_norm.shape
    ca = rwkv_w0.shape[-1]
    cb = pool_scale.shape[-1]
    n_heads, dh = cache_k.shape[3], cache_k.shape[4]
    cc = n_heads * dh
    d_idx = cache_kidx.shape[-1]
    dd, da, dg = rwkv_w2.shape[1], rwkv_a2.shape[1], rwkv_g2.shape[1]
    a_cols = 3 * ca + dd + da + dg
    h_idx = (w_in.shape[-1] - a_cols - cb - 3 * cc - d_idx - 3 * d) // (d_idx + 1)
    dims = dict(D=d, C_A=ca, C_B=cb, C_C=cc, H_C=n_heads, DH_C=dh, D_IDX=d_idx, H_IDX=h_idx,
                QIW=h_idx * d_idx, D_DECAY=dd, D_AAA=da, D_GATE=dg)
    assert w_in.shape[-1] == a_cols + cb + 3 * cc + h_idx * d_idx + d_idx + h_idx + 3 * d
    P = _prepare_weights(raw, dims)

    bp = x_prompt.shape[0]
    past = page_table.shape[1] * PAGE
    bs, ts = x_sample.shape[:2]
    tp = x_prompt.shape[1]
    mp, ms = bp * tp, bs * ts
    x_all = jnp.concatenate([x_prompt.reshape(mp, d), x_sample.reshape(ms, d)], axis=0)
    carry = (x_all,) + tuple(_prenorm(x_all, ffn1_norm[0]))
    st_p, st_s = [], []
    for l in range(depth):
        next_g = ffn1_norm[l + 1] if l + 1 < depth else final_norm
        groups = [
            dict(row0=0, b=bp, t=tp, pos0=0, cache=None, shift=jnp.zeros((bp, a_cols), F32),
                 wkv=jnp.zeros((bp, ca // N_A, N_A, N_A), F32), pool=jnp.zeros((bp, POOL_HIST, cb), F32)),
            dict(row0=mp, b=bs, t=ts, pos0=past, cache=(cache_k, cache_v, cache_kidx, page_table),
                 shift=state_shift[l], wkv=state_wkv[l], pool=state_pool[l]),
        ]
        carry, (sp, ss) = _layer(carry, groups, P, l, next_g, dims)
        st_p.append(sp)
        st_s.append(ss)
    y_p = _rmsnorm(carry[0], 0, mp, final_norm, F32).reshape(x_prompt.shape)
    y_s = _rmsnorm(carry[0], mp, ms, final_norm, F32).reshape(x_sample.shape)
    stk = lambda sts, i: jnp.stack([s[i] for s in sts])
    return (y_p, y_s,
            stk(st_p, 0), stk(st_p, 1), stk(st_p, 2), stk(st_p, 3), stk(st_p, 4), stk(st_p, 5),
            stk(st_s, 0), stk(st_s, 1), stk(st_s, 2), stk(st_s, 3), stk(st_s, 4), stk(st_s, 5))
```

```python
import functools
import math

import jax
import jax.numpy as jnp
from jax import lax
from jax.experimental import pallas as pl
from jax.experimental.pallas import tpu as pltpu

F32 = jnp.float32
BF16 = jnp.bfloat16

LANE = 128
SUBLANE = 8
VMEM_LIMIT_BYTES = 56 * 1024 * 1024
ROW_TILE = 1400

N_A = 64
GN_EPS = 64e-5
POOL_WINDOWS = (2, 4, 8, 16)
POOL_HIST = max(POOL_WINDOWS) - 1
TOPK_MAX = 256
ROPE_THETA = 10000.0
RMS_EPS = 1e-6
PAGE = 128
NEG_BIG = -1e30


def _cparams(n_axes):
    return pltpu.CompilerParams(dimension_semantics=("arbitrary",) * n_axes,
                                vmem_limit_bytes=VMEM_LIMIT_BYTES)


def _tile(n, target, mult):
    best = None
    for t in range(mult, min(n, target) + 1, mult):
        if n % t == 0:
            best = t
    return best if best is not None else n


def _roundup(n, m):
    return -(-n // m) * m


def _dot(a, b):
    return jnp.dot(a.astype(BF16), b.astype(BF16), preferred_element_type=F32)


def _dot_nt(a, b):
    return lax.dot_general(a.astype(BF16), b.astype(BF16), (((1,), (1,)), ((), ())),
                           preferred_element_type=F32)


def _split3(x):
    h1 = x.astype(BF16)
    r1 = x - h1.astype(F32)
    h2 = r1.astype(BF16)
    h3 = (r1 - h2.astype(F32)).astype(BF16)
    return h1, h2, h3


def _dot_exact_rhs(x, m_bf16):
    h1, h2, h3 = _split3(x)
    return (jnp.dot(h1, m_bf16, preferred_element_type=F32)
            + jnp.dot(h2, m_bf16, preferred_element_type=F32)
            + jnp.dot(h3, m_bf16, preferred_element_type=F32))


def _dot_exact_lhs(m_bf16, x):
    h1, h2, h3 = _split3(x)
    return (jnp.dot(m_bf16, h1, preferred_element_type=F32)
            + jnp.dot(m_bf16, h2, preferred_element_type=F32)
            + jnp.dot(m_bf16, h3, preferred_element_type=F32))


def _rmsnorm_kernel(x_ref, g_ref, o_ref):
    x = x_ref[...]
    ms = jnp.mean(x * x, axis=-1, keepdims=True)
    o_ref[...] = (x * lax.rsqrt(ms + RMS_EPS) * g_ref[...]).astype(o_ref.dtype)


def _rmsnorm(x2d, row0, rows, g, out_dtype):
    d = x2d.shape[1]
    tm = _tile(rows, 256, SUBLANE)
    assert row0 % tm == 0
    return pl.pallas_call(
        _rmsnorm_kernel,
        grid=(rows // tm,),
        in_specs=[pl.BlockSpec((tm, d), lambda i: (row0 // tm + i, 0)), pl.BlockSpec((1, d), lambda i: (0, 0))],
        out_specs=pl.BlockSpec((tm, d), lambda i: (i, 0)),
        out_shape=jax.ShapeDtypeStruct((rows, d), out_dtype),
        compiler_params=_cparams(1),
        name="rmsnorm",
    )(x2d, g.reshape(1, d))


def _prenorm_kernel(x_ref, g_ref, xb_ref, ssq_ref):
    x = x_ref[...]
    xb_ref[...] = (x * g_ref[...]).astype(xb_ref.dtype)
    ssq_ref[...] = jnp.sum(x * x, axis=-1, keepdims=True)


def _prenorm(x2d, g):
    m, d = x2d.shape
    tm = _tile(m, 256, SUBLANE)
    return pl.pallas_call(
        _prenorm_kernel,
        grid=(m // tm,),
        in_specs=[pl.BlockSpec((tm, d), lambda i: (i, 0)), pl.BlockSpec((1, d), lambda i: (0, 0))],
        out_specs=[pl.BlockSpec((tm, d), lambda i: (i, 0)), pl.BlockSpec((tm, 1), lambda i: (i, 0))],
        out_shape=[jax.ShapeDtypeStruct((m, d), BF16), jax.ShapeDtypeStruct((m, 1), F32)],
        compiler_params=_cparams(1),
        name="prenorm",
    )(x2d, g.reshape(1, d))


def _row_scale(ssq, width):
    return lax.rsqrt(ssq * (1.0 / width) + RMS_EPS)


def _mm_kernel(*refs, nk, kdim, w_t, has_ssq, has_res, has_next, res_scale, act):
    it = iter(refs)
    x_ref, w_ref = next(it), next(it)
    ssq_ref = next(it) if has_ssq else None
    res_ref = next(it) if has_res else None
    gnext_ref = next(it) if has_next else None
    o_ref = next(it)
    ob_ref, ssqo_ref = (next(it), next(it)) if has_next else (None, None)

    def epilogue(acc):
        if has_ssq:
            acc = acc * _row_scale(ssq_ref[...], kdim)
        if act == "sigmoid":
            acc = jax.nn.sigmoid(acc)
        if has_res:
            acc = res_ref[...] + res_scale * acc
        o_ref[...] = acc.astype(o_ref.dtype)
        if has_next:
            ob_ref[...] = (acc * gnext_ref[...]).astype(ob_ref.dtype)
            part_ssq = jnp.sum(acc * acc, axis=-1, keepdims=True)
            j = pl.program_id(1)

            @pl.when(j == 0)
            def _():
                ssqo_ref[...] = part_ssq

            @pl.when(j > 0)
            def _():
                ssqo_ref[...] += part_ssq

    if w_t:
        part = lax.dot_general(x_ref[...], w_ref[0].astype(BF16), (((1,), (1,)), ((), ())),
                               preferred_element_type=F32)
    else:
        part = jnp.dot(x_ref[...], w_ref[0], preferred_element_type=F32)
    if nk == 1:
        epilogue(part)
    else:
        acc_ref = next(it)
        k = pl.program_id(2)

        @pl.when(k == 0)
        def _():
            acc_ref[...] = part

        @pl.when(k > 0)
        def _():
            acc_ref[...] += part

        @pl.when(k == nk - 1)
        def _():
            epilogue(acc_ref[...])


def _mm(x, w, l, *, out_dtype, ssq=None, res=None, res_scale=1.0, act=None, next_g=None, tm_target=ROW_TILE,
        tn_target=1024, tk_target=4096, w_t=False, w_rows=None, name="mm"):
    m, kdim = x.shape
    n = w_rows[1] if w_rows is not None else (w.shape[1] if w_t else w.shape[2])
    tm = _tile(m, tm_target, 2 * SUBLANE)
    tn = _tile(n, tn_target, LANE)
    tk = _tile(kdim, tk_target, LANE)
    nk = kdim // tk
    tile = pl.BlockSpec((tm, tn), lambda i, j, k: (i, j))
    col = pl.BlockSpec((tm, 1), lambda i, j, k: (i, 0))
    if w_rows is not None:
        assert w_t and w_rows[0] % (2 * SUBLANE) == 0
        wspec = pl.BlockSpec((pl.Element(1), pl.Element(tn), pl.Element(tk)),
                             lambda i, j, k: (l, pl.multiple_of(w_rows[0] + j * tn, 2 * SUBLANE),
                                              pl.multiple_of(k * tk, LANE)))
    elif w_t:
        wspec = pl.BlockSpec((1, tn, tk), lambda i, j, k: (l, j, k))
    else:
        wspec = pl.BlockSpec((1, tk, tn), lambda i, j, k: (l, k, j))
    in_specs = [pl.BlockSpec((tm, tk), lambda i, j, k: (i, k)), wspec]
    args = [x, w]
    if ssq is not None:
        in_specs.append(col)
        args.append(ssq)
    if res is not None:
        in_specs.append(tile)
        args.append(res)
    out_specs, out_shape = tile, jax.ShapeDtypeStruct((m, n), out_dtype)
    if next_g is not None:
        assert res is not None
        in_specs.append(pl.BlockSpec((1, tn), lambda i, j, k: (0, j)))
        args.append(next_g.reshape(1, n))
        out_specs = [tile, tile, col]
        out_shape = [out_shape, jax.ShapeDtypeStruct((m, n), BF16), jax.ShapeDtypeStruct((m, 1), F32)]
    scratch = [pltpu.VMEM((tm, tn), F32)] if nk > 1 else []
    return pl.pallas_call(
        functools.partial(_mm_kernel, nk=nk, kdim=kdim, w_t=w_t, has_ssq=ssq is not None, has_res=res is not None,
                          has_next=next_g is not None, res_scale=res_scale, act=act),
        grid=(m // tm, n // tn, nk),
        in_specs=in_specs,
        out_specs=out_specs,
        out_shape=out_shape,
        scratch_shapes=scratch,
        compiler_params=_cparams(3),
        name=name,
    )(*args)


def _swiglu_up_kernel(x_ref, ssq_ref, wg_ref, wu_ref, o_ref, *, d):
    x = x_ref[...]
    scale = _row_scale(ssq_ref[...], d)
    g = jnp.dot(x, wg_ref[0].astype(BF16), preferred_element_type=F32) * scale
    u = jnp.dot(x, wu_ref[0].astype(BF16), preferred_element_type=F32) * scale
    o_ref[...] = (g * jax.nn.sigmoid(g) * u).astype(o_ref.dtype)


def _swiglu_up(x, ssq, wg, wu, l):
    m, d = x.shape
    f = wg.shape[2]
    tm = _tile(m, ROW_TILE, 2 * SUBLANE)
    tn = _tile(f, 512, LANE)
    return pl.pallas_call(
        functools.partial(_swiglu_up_kernel, d=d),
        grid=(m // tm, f // tn),
        in_specs=[pl.BlockSpec((tm, d), lambda i, j: (i, 0)),
                  pl.BlockSpec((tm, 1), lambda i, j: (i, 0)),
                  pl.BlockSpec((1, d, tn), lambda i, j: (l, 0, j)),
                  pl.BlockSpec((1, d, tn), lambda i, j: (l, 0, j))],
        out_specs=pl.BlockSpec((tm, tn), lambda i, j: (i, j)),
        out_shape=jax.ShapeDtypeStruct((m, f), BF16),
        compiler_params=_cparams(2),
        name="swiglu_up",
    )(x, ssq, wg, wu)


def _ffn_down_kernel(a_ref, w_ref, res_ref, gnext_ref, o_ref, ob_ref, ssq_ref, acc_ref, *, nk, res_scale):
    k = pl.program_id(1)
    j = pl.program_id(2)
    part = jnp.dot(a_ref[...], w_ref[0], preferred_element_type=F32)

    if nk > 1:
        @pl.when(k == 0)
        def _():
            acc_ref[j] = part

        @pl.when((k > 0) & (k < nk - 1))
        def _():
            acc_ref[j] += part

    @pl.when(k == nk - 1)
    def _():
        acc = acc_ref[j] + part if nk > 1 else part
        h = res_ref[...] + res_scale * acc
        o_ref[...] = h
        ob_ref[...] = (h * gnext_ref[...]).astype(ob_ref.dtype)
        part_ssq = jnp.sum(h * h, axis=-1, keepdims=True)

        @pl.when(j == 0)
        def _():
            ssq_ref[...] = part_ssq

        @pl.when(j > 0)
        def _():
            ssq_ref[...] += part_ssq


def _ffn_down(a, w, l, res, res_scale, next_g):
    m, kdim = a.shape
    n = w.shape[2]
    tm = _tile(m, 1024, 2 * SUBLANE)
    tn = _tile(n, 512 if tm <= 768 else 256, LANE)
    tk = _tile(kdim, 5632, LANE)
    nk, nn = kdim // tk, n // tn
    last = lambda k, j: jnp.where(k == nk - 1, j, 0)
    tile = pl.BlockSpec((tm, tn), lambda i, k, j: (i, last(k, j)))
    col = pl.BlockSpec((tm, 1), lambda i, k, j: (i, 0))
    return pl.pallas_call(
        functools.partial(_ffn_down_kernel, nk=nk, res_scale=res_scale),
        grid=(m // tm, nk, nn),
        in_specs=[pl.BlockSpec((tm, tk), lambda i, k, j: (i, k)),
                  pl.BlockSpec((1, tk, tn), lambda i, k, j: (l, k, j)),
                  tile,
                  pl.BlockSpec((1, tn), lambda i, k, j: (0, last(k, j)))],
        out_specs=[tile, tile, col],
        out_shape=[jax.ShapeDtypeStruct((m, n), F32), jax.ShapeDtypeStruct((m, n), BF16),
                   jax.ShapeDtypeStruct((m, 1), F32)],
        scratch_shapes=[pltpu.VMEM((nn, tm, tn), F32)],
        compiler_params=_cparams(3),
        name="ffn_down",
    )(a, w, res, next_g.reshape(1, n))


def _ffn(h, hb, ssq, wg, wu, wd, l, next_g):
    a = _swiglu_up(hb, ssq, wg, wu, l)
    return _ffn_down(a, wd, l, h, 0.5, next_g)


def _merge_kernel(ya_ref, yb_ref, yc_ref, wa_ref, wb_ref, wc_ref, ga_ref, gb_ref, gc_ref, o_ref):
    a = jnp.dot(ya_ref[...], wa_ref[0], preferred_element_type=F32)
    b = jnp.dot(yb_ref[...], wb_ref[0], preferred_element_type=F32)
    c = jnp.dot(yc_ref[...], wc_ref[0], preferred_element_type=F32)
    out = ga_ref[...].astype(F32) * a + gb_ref[...].astype(F32) * b + gc_ref[...].astype(F32) * c
    o_ref[...] = out.astype(o_ref.dtype)


def _merge(ya, yb, yc, wa, wb, wc, l, gates):
    m = ya.shape[0]
    d = wa.shape[2]
    tm = _tile(m, ROW_TILE, 2 * SUBLANE)
    tn = _tile(d, 512, LANE)
    nb = d // tn
    wspec = lambda w: pl.BlockSpec((1, w.shape[1], tn), lambda i, j: (l, 0, j))
    return pl.pallas_call(
        _merge_kernel,
        grid=(m // tm, nb),
        in_specs=[pl.BlockSpec((tm, ya.shape[1]), lambda i, j: (i, 0)),
                  pl.BlockSpec((tm, yb.shape[1]), lambda i, j: (i, 0)),
                  pl.BlockSpec((tm, yc.shape[1]), lambda i, j: (i, 0)),
                  wspec(wa), wspec(wb), wspec(wc),
                  pl.BlockSpec((tm, tn), lambda i, j: (i, j)),
                  pl.BlockSpec((tm, tn), lambda i, j: (i, nb + j)),
                  pl.BlockSpec((tm, tn), lambda i, j: (i, 2 * nb + j))],
        out_specs=pl.BlockSpec((tm, tn), lambda i, j: (i, j)),
        out_shape=jax.ShapeDtypeStruct((m, d), BF16),
        compiler_params=_cparams(2),
        name="merge",
    )(ya, yb, yc, wa, wb, wc, gates, gates, gates)


def _rope_kernel(q_ref, k_ref, v_ref, qi_ref, kw_ref, c128_ref, s128_ref, c64_ref, s64_ref,
                 qo_ref, ko_ref, vo_ref, kb_ref, vb_ref, qio_ref, kwo_ref, *, n_heads, n_idx_groups, d_idx,
                 idx_scale):
    c128, s128 = c128_ref[...], s128_ref[...]
    c64, s64 = c64_ref[...], s64_ref[...]
    lane = lax.broadcasted_iota(jnp.int32, c64.shape, 1)
    first_half = (lane % d_idx) < (d_idx // 2)

    def rope128(x):
        return x * c128 + pltpu.roll(x, LANE // 2, axis=1) * s128

    def rope64(x):
        rot = jnp.where(first_half, pltpu.roll(x, LANE - d_idx // 2, axis=1), pltpu.roll(x, d_idx // 2, axis=1))
        return x * c64 + rot * s64

    for h in range(n_heads):
        sl = slice(h * LANE, (h + 1) * LANE)
        qo_ref[0, :, sl] = rope128(q_ref[0, :, sl]).astype(qo_ref.dtype)
        kr = rope128(k_ref[0, :, sl])
        ko_ref[0, :, h, :] = kr
        kb_ref[0, :, sl] = kr.astype(kb_ref.dtype)
        vo_ref[0, :, h, :] = v_ref[0, :, sl]
    vb_ref[0] = v_ref[0].astype(vb_ref.dtype)
    for g in range(n_idx_groups):
        sl = slice(g * LANE, (g + 1) * LANE)
        qio_ref[0, :, sl] = rope64(qi_ref[0, :, sl])
    kw = kw_ref[0]
    kwo_ref[0] = jnp.where(lane < d_idx, rope64(kw), kw * idx_scale)


def _rope_tables(pos, d):
    inv = ROPE_THETA ** (-jnp.arange(0, d, 2, dtype=F32) / d)
    ang = pos.astype(F32)[:, None] * inv[None, :]
    cos, sin = jnp.cos(ang), jnp.sin(ang)
    reps = LANE // d
    c = jnp.tile(jnp.concatenate([cos, cos], axis=-1), (1, reps))
    s = jnp.tile(jnp.concatenate([-sin, sin], axis=-1), (1, reps))
    return c, s


def _rope_all(mid_all, row0, b, t, pos0, dims):
    mid = mid_all.reshape((1,) + mid_all.shape)
    cb, cc, qiw, d_idx, h_idx = dims["C_B"], dims["C_C"], dims["QIW"], dims["D_IDX"], dims["H_IDX"]
    assert dims["DH_C"] == LANE and LANE % d_idx == 0
    assert cb % cc == 0 and (cb + 3 * cc) % qiw == 0
    tm = _tile(t, 256, SUBLANE)
    pos = pos0 + jnp.arange(t)
    c128, s128 = _rope_tables(pos, LANE)
    c64, s64 = _rope_tables(pos, d_idx)
    ob = cb // cc
    assert row0 % tm == 0
    row = lambda width, idx: pl.BlockSpec((1, tm, width), lambda i, j: (0, row0 // tm + i * (t // tm) + j, idx))
    tab = pl.BlockSpec((tm, LANE), lambda i, j: (j, 0))
    out = lambda width: pl.BlockSpec((1, tm, width), lambda i, j: (i, j, 0))
    split = pl.BlockSpec((1, tm, cc // LANE, LANE), lambda i, j: (i, j, 0, 0))
    return pl.pallas_call(
        functools.partial(_rope_kernel, n_heads=cc // LANE, n_idx_groups=qiw // LANE, d_idx=d_idx,
                          idx_scale=(h_idx * d_idx) ** -0.5),
        grid=(b, t // tm),
        in_specs=[row(cc, ob), row(cc, ob + 1), row(cc, ob + 2), row(qiw, (cb + 3 * cc) // qiw),
                  row(LANE, (cb + 3 * cc + qiw) // LANE), tab, tab, tab, tab],
        out_specs=[out(cc), split, split, out(cc), out(cc), out(qiw), out(LANE)],
        out_shape=[jax.ShapeDtypeStruct((b, t, cc), BF16), jax.ShapeDtypeStruct((b, t, cc // LANE, LANE), F32),
                   jax.ShapeDtypeStruct((b, t, cc // LANE, LANE), F32),
                   jax.ShapeDtypeStruct((b, t, cc), BF16), jax.ShapeDtypeStruct((b, t, cc), BF16),
                   jax.ShapeDtypeStruct((b, t, qiw), F32), jax.ShapeDtypeStruct((b, t, LANE), F32)],
        compiler_params=_cparams(2),
        name="rope",
    )(mid, mid, mid, mid, mid, c128, s128, c64, s64)


def _head_sum(x, jmat):
    parts = []
    for g in range(x.shape[1] // LANE):
        parts.append(_dot_exact_rhs(x[:, g * LANE:(g + 1) * LANE], jmat))
    return parts[0] if len(parts) == 1 else jnp.concatenate(parts, axis=1)


def _head_ones():
    r = lax.broadcasted_iota(jnp.int32, (LANE, LANE), 0)
    c = lax.broadcasted_iota(jnp.int32, (LANE, LANE), 1)
    return jnp.where((r // N_A) == (c // N_A), 1.0, 0.0).astype(BF16)


def _rwkv_prep_kernel(pa_ref, halo_ref, shift_ref, mu_ref, w0_ref, w2_ref, a0_ref, a2_ref, g2_ref,
                      kk_ref, ka_ref, rk_ref,
                      rt_ref, at_ref, bt_ref, kt_ref, v_ref, bkt_ref, gct_ref, bonus_ref, g_ref, *, ca, chunk, tm):
    j = pl.program_id(1)
    prev_last = jnp.where(j == 0, shift_ref[0], halo_ref[0, SUBLANE - 1:SUBLANE, :])
    row = lax.broadcasted_iota(jnp.int32, (tm, 1), 0)

    def shifted(lo, hi):
        x = pa_ref[0, :, lo:hi]
        prev = jnp.where(row == 0, prev_last[:, lo:hi], pltpu.roll(x, 1, axis=0))
        return x + mu_ref[:, lo:hi] * (prev - x)

    r = shifted(0, ca)
    k = shifted(ca, 2 * ca)
    v = shifted(2 * ca, 3 * ca)
    wl = shifted(3 * ca, 3 * ca + LANE)
    al = shifted(3 * ca + LANE, 3 * ca + 2 * LANE)
    gl = shifted(3 * ca + 2 * LANE, pa_ref.shape[2])

    z = -(w0_ref[...] + _dot(jnp.tanh(wl), w2_ref[...]))
    softplus = jnp.maximum(z, 0.0) + jnp.log(1.0 + jnp.exp(-jnp.abs(z)))
    log_decay = -jnp.exp(-softplus - 0.5)
    rr = lax.broadcasted_iota(jnp.int32, (tm, tm), 0)
    cc = lax.broadcasted_iota(jnp.int32, (tm, tm), 1)
    same_chunk = (rr // chunk) == (cc // chunk)
    tril = jnp.where(same_chunk & (cc <= rr), 1.0, 0.0).astype(BF16)
    ones_blk = jnp.where(same_chunk, 1.0, 0.0).astype(BF16)
    cum = _dot_exact_lhs(tril, log_decay)
    cum_c = _dot_exact_lhs(ones_blk, log_decay)
    fwd = jnp.exp(cum_c - cum)
    back = jnp.exp(cum - cum_c)
    back_prev = jnp.exp(cum - log_decay - cum_c)
    gam_c = jnp.exp(cum_c)

    a = jax.nn.sigmoid(a0_ref[...] + _dot(al, a2_ref[...]))
    jmat = _head_ones()
    kk = k * kk_ref[...]
    kk = kk / jnp.maximum(jnp.sqrt(_head_sum(kk * kk, jmat)), 1e-12)
    kh = k * (1.0 + (a - 1.0) * ka_ref[...])
    bonus_ref[0] = _head_sum(r * kh * rk_ref[...], jmat) * v
    g_ref[0] = _dot(jax.nn.sigmoid(gl), g2_ref[...])
    bh = kk * a * fwd
    kf = kh * fwd
    rt_ref[0] = (r * back).astype(rt_ref.dtype)
    at_ref[0] = (-kk * back_prev).astype(at_ref.dtype)
    bt_ref[0] = bh.astype(bt_ref.dtype)
    kt_ref[0] = kf.astype(kt_ref.dtype)
    v_ref[0] = v.astype(v_ref.dtype)
    pad = N_A - chunk
    for c in range(tm // chunk):
        rows = slice(c * chunk, (c + 1) * chunk)
        if pad:
            zeros = jnp.zeros((pad, ca), F32)
            stacked = jnp.concatenate([bh[rows], zeros, kf[rows], zeros], axis=0)
        else:
            stacked = jnp.concatenate([bh[rows], kf[rows]], axis=0)
        bkt_ref[0, c] = stacked.T.astype(bkt_ref.dtype)
        gct_ref[0, c] = jnp.broadcast_to(gam_c[c * chunk:c * chunk + 1], (LANE, ca)).T


def _rwkv_prep(pa_all, row0, b, t, shift_p, P, l, dims, chunk):
    paw = pa_all.shape[1]
    pa = pa_all.reshape((1,) + pa_all.shape)
    ca = dims["C_A"]
    tm = chunk * max(1, min(128, t) // chunk)
    assert t % tm == 0 and row0 % tm == 0 and 2 * N_A == LANE
    hb = tm // SUBLANE
    nct = tm // chunk
    row = pl.BlockSpec((1, tm, ca), lambda i, j: (i, j, 0))
    cmaj = pl.BlockSpec((1, nct, ca, LANE), lambda i, j: (i, j, 0, 0))
    vec = lambda w: pl.BlockSpec((1, w), lambda i, j: (0, 0))
    mat = lambda r: pl.BlockSpec((r, ca), lambda i, j: (0, 0))
    tmaj = lambda dt: jax.ShapeDtypeStruct((b, t, ca), dt)
    return pl.pallas_call(
        functools.partial(_rwkv_prep_kernel, ca=ca, chunk=chunk, tm=tm),
        grid=(b, t // tm),
        in_specs=[pl.BlockSpec((1, tm, paw), lambda i, j: (0, row0 // tm + i * (t // tm) + j, 0)),
                  pl.BlockSpec((1, SUBLANE, paw),
                               lambda i, j: (0, jnp.maximum((row0 + i * t) // SUBLANE + j * hb - 1, 0), 0)),
                  pl.BlockSpec((1, 1, paw), lambda i, j: (i, 0, 0)),
                  vec(paw), vec(ca), mat(LANE), vec(ca), mat(LANE), mat(P["g2"].shape[1]),
                  vec(ca), vec(ca), vec(ca)],
        out_specs=[row] * 5 + [cmaj, cmaj, row, row],
        out_shape=[tmaj(BF16)] * 5 + [jax.ShapeDtypeStruct((b, t // chunk, ca, LANE), BF16),
                                      jax.ShapeDtypeStruct((b, t // chunk, ca, LANE), F32), tmaj(F32), tmaj(F32)],
        compiler_params=_cparams(2),
        name="rwkv_prep",
    )(pa, pa, shift_p.reshape(b, 1, paw), P["mu"][l], P["w0"][l], P["w2"][l], P["a0"][l], P["a2"][l],
      P["g2"][l], P["k_k"][l], P["k_a"][l], P["r_k"][l])


def _rwkv_scan_kernel(at_ref, rt_ref, v_ref, bt_ref, kt_ref, bkt_ref, gct_ref, s0_ref, y_ref, s_ref, *,
                      n_pairs, chunk, unroll):
    c = pl.program_id(1)

    @pl.when(c == 0)
    def _():
        s_ref[...] = s0_ref[...]

    rr = lax.broadcasted_iota(jnp.int32, (chunk, chunk), 0)
    cc = lax.broadcasted_iota(jnp.int32, (chunk, chunk), 1)
    lower_strict = cc < rr
    lower_incl = cc <= rr
    eye = jnp.where(rr == cc, 1.0, 0.0)
    first = lax.broadcasted_iota(jnp.int32, (chunk, LANE), 1) < N_A
    br = lax.broadcasted_iota(jnp.int32, (LANE, LANE), 0)
    bc = lax.broadcasted_iota(jnp.int32, (LANE, LANE), 1)
    block_diag = (br // N_A) == (bc // N_A)
    n_double = max(int(math.log2(chunk)) - 1, 0)
    pad = N_A - chunk

    def load(p):
        sl = pl.ds(pl.multiple_of(p * LANE, LANE), LANE)
        return (sl,
                at_ref[0, :, sl],
                rt_ref[0, :, sl],
                v_ref[0, :, sl],
                bt_ref[0, :, sl],
                kt_ref[0, :, sl],
                bkt_ref[0, 0, sl, :],
                s_ref[0, sl, :] * gct_ref[0, 0, sl, :])

    def group(g, carry):
        loaded = [load(g * unroll + j) for j in range(unroll)]
        sls, a, r, v, b, k, bk_t, sb = (list(col) for col in zip(*loaded))
        pairs = range(unroll)
        heads = [(i, hh) for i in pairs for hh in range(2)]
        own = lambda hh: first if hh == 0 else jnp.logical_not(first)
        sb16 = [sb[i].astype(BF16) for i in pairs]
        x = [jnp.dot(a[i], sb16[i], preferred_element_type=F32) for i in pairs]
        y = [jnp.dot(r[i], sb16[i], preferred_element_type=F32) for i in pairs]
        a_h = [jnp.where(own(hh), a[i], jnp.zeros_like(a[i])) for i, hh in heads]
        r_h = [jnp.where(own(hh), r[i], jnp.zeros_like(r[i])) for i, hh in heads]
        n_ab = [jnp.where(lower_strict, _dot_nt(a_h[j], b[i]), 0.0) for j, (i, _) in enumerate(heads)]
        n_ak = [jnp.where(lower_strict, _dot_nt(a_h[j], k[i]), 0.0) for j, (i, _) in enumerate(heads)]
        m_rb = [jnp.where(lower_incl, _dot_nt(r_h[j], b[i]), 0.0) for j, (i, _) in enumerate(heads)]
        m_rk = [jnp.where(lower_incl, _dot_nt(r_h[j], k[i]), 0.0) for j, (i, _) in enumerate(heads)]
        inv = [eye + n for n in n_ab]
        npow = n_ab
        for _ in range(n_double):
            npow = [_dot(n, n) for n in npow]
            inv = [iv + _dot(iv, n) for iv, n in zip(inv, npow)]
        w = [x[i] + _dot(n_ak[j], v[i]) for j, (i, _) in enumerate(heads)]
        u = [_dot(inv[j], w[j]) for j in range(len(heads))]
        y_h = [_dot(m_rb[j], u[j]) + _dot(m_rk[j], v[i]) for j, (i, _) in enumerate(heads)]
        for i in pairs:
            u_p = jnp.where(first, u[2 * i], u[2 * i + 1])
            y_new = y[i] + jnp.where(first, y_h[2 * i], y_h[2 * i + 1])
            v32 = v[i].astype(F32)
            if pad:
                zeros = jnp.zeros((pad, LANE), F32)
                stacked = jnp.concatenate([u_p, zeros, v32, zeros], axis=0)
            else:
                stacked = jnp.concatenate([u_p, v32], axis=0)
            s_new = sb[i] + jnp.where(block_diag, _dot(bk_t[i], stacked), 0.0)
            y_ref[0, :, sls[i]] = y_new
            s_ref[0, sls[i], :] = s_new
        return carry

    lax.fori_loop(0, n_pairs // unroll, group, 0)


def _rwkv_scan(rt, at, bt, kt, v, bkt, gct, s0, chunk):
    b, t, ca = rt.shape
    h = ca // N_A
    nc = t // chunk
    eye2 = jnp.eye(2, dtype=F32)
    s0t = jnp.swapaxes(s0, -1, -2).reshape(b, h // 2, 2, N_A, 1, N_A)
    sb0 = (s0t * eye2[None, None, :, None, :, None]).reshape(b, ca, LANE)
    tmaj = pl.BlockSpec((1, chunk, ca), lambda i, j: (i, j, 0))
    cmaj = pl.BlockSpec((1, 1, ca, LANE), lambda i, j: (i, j, 0, 0))
    st = pl.BlockSpec((1, ca, LANE), lambda i, j: (i, 0, 0))
    y, sb = pl.pallas_call(
        functools.partial(_rwkv_scan_kernel, n_pairs=h // 2, chunk=chunk, unroll=_tile(h // 2, 16, 1)),
        grid=(b, nc),
        in_specs=[tmaj, tmaj, tmaj, tmaj, tmaj, cmaj, cmaj, st],
        out_specs=[tmaj, st],
        out_shape=[jax.ShapeDtypeStruct((b, t, ca), F32), jax.ShapeDtypeStruct((b, ca, LANE), F32)],
        compiler_params=_cparams(2),
        name="rwkv_scan",
    )(at, rt, v, bt, kt, bkt, gct, sb0)
    sb = sb.reshape(b, h // 2, 2, N_A, 2, N_A)
    s_t = jnp.stack([sb[:, :, 0, :, 0, :], sb[:, :, 1, :, 1, :]], axis=2).reshape(b, h, N_A, N_A)
    return y, jnp.swapaxes(s_t, -1, -2)


def _rwkv_post_kernel(y_ref, bonus_ref, g_ref, lnw_ref, lnb_ref, o_ref):
    jmat = _head_ones()
    y = y_ref[...]
    mean = _head_sum(y, jmat) * (1.0 / N_A)
    d = y - mean
    var = _head_sum(d * d, jmat) * (1.0 / N_A)
    out = (d * lax.rsqrt(var + GN_EPS) * lnw_ref[...] + lnb_ref[...] + bonus_ref[...]) * g_ref[...]
    o_ref[...] = out.astype(o_ref.dtype)


def _rwkv_post(y, bonus, g, ln_w, ln_b):
    m, ca = y.shape
    tm = _tile(m, 256, SUBLANE)
    row = pl.BlockSpec((tm, ca), lambda i: (i, 0))
    vec = pl.BlockSpec((1, ca), lambda i: (0, 0))
    return pl.pallas_call(
        _rwkv_post_kernel,
        grid=(m // tm,),
        in_specs=[row, row, row, vec, vec],
        out_specs=row,
        out_shape=jax.ShapeDtypeStruct((m, ca), BF16),
        compiler_params=_cparams(1),
        name="rwkv_post",
    )(y, bonus, g, ln_w, ln_b)


def _pool_kernel(z_ref, hist_ref, w_ref, scale_ref, o_ref, x_scr, *, t, tm, group, pos0):
    halo = 2 * SUBLANE
    x_scr[0:halo, :] = hist_ref[0]
    x_scr[halo:halo + t, :] = z_ref[0]
    for i in range(t // tm):
        r0 = i * tm
        pos = pos0 + r0 + lax.broadcasted_iota(jnp.int32, (tm, 1), 0)
        for gi, win in enumerate(POOL_WINDOWS):
            lo, hi = gi * group, (gi + 1) * group
            cur = x_scr[halo + r0:halo + r0 + tm, lo:hi]
            tot = cur
            for back in range(1, win):
                tot = tot + x_scr[halo + r0 - back:halo + r0 - back + tm, lo:hi]
            cnt = jnp.minimum(pos + 1, win).astype(F32)
            d = tot / cnt - cur
            y = _dot(d, w_ref[gi]) * scale_ref[:, lo:hi]
            o_ref[0, r0:r0 + tm, lo:hi] = y.astype(o_ref.dtype)


def _pool(mid_all, row0, b, t, hist16, pool_w, pool_scale, pos0, dims):
    assert row0 % t == 0
    mid = mid_all.reshape((1,) + mid_all.shape)
    cb = dims["C_B"]
    group = pool_w.shape[1]
    assert len(POOL_WINDOWS) * group == cb
    tm = _tile(t, 256, SUBLANE)
    return pl.pallas_call(
        functools.partial(_pool_kernel, t=t, tm=tm, group=group, pos0=pos0),
        grid=(b,),
        in_specs=[pl.BlockSpec((1, t, cb), lambda i: (0, row0 // t + i, 0)),
                  pl.BlockSpec((1, 2 * SUBLANE, cb), lambda i: (i, 0, 0)),
                  pl.BlockSpec(pool_w.shape, lambda i: (0, 0, 0)),
                  pl.BlockSpec((1, cb), lambda i: (0, 0))],
        out_specs=pl.BlockSpec((1, t, cb), lambda i: (i, 0, 0)),
        out_shape=jax.ShapeDtypeStruct((b, t, cb), BF16),
        scratch_shapes=[pltpu.VMEM((t + 2 * SUBLANE, cb), F32)],
        compiler_params=_cparams(1),
        name="pool",
    )(mid, hist16, pool_w, pool_scale.reshape(1, cb))


def _monotone_key(x):
    x = jnp.where(x == 0.0, 0.0, x)
    bits = lax.bitcast_convert_type(x, jnp.int32)
    return jnp.where(bits < 0, bits ^ jnp.int32(0x7FFFFFFF), bits)


def _kth_largest_key(key, k):
    def body(i, tau):
        cand = tau + jnp.left_shift(jnp.int32(1), jnp.int32(31) - i)
        cnt = jnp.sum(jnp.where(key >= cand, 1.0, 0.0), axis=-1, keepdims=True)
        return jnp.where(cnt >= k, cand, tau)

    tau0 = jnp.full((key.shape[0], 1), -2 ** 31, jnp.int32)
    return lax.fori_loop(0, 32, body, tau0)


def _prefix_count(ind):
    r = lax.broadcasted_iota(jnp.int32, (LANE, LANE), 0)
    c = lax.broadcasted_iota(jnp.int32, (LANE, LANE), 1)
    tri = jnp.where(r <= c, 1.0, 0.0).astype(BF16)
    run = jnp.zeros((ind.shape[0], 1), F32)
    outs = []
    for j in range(ind.shape[1] // LANE):
        pj = jnp.dot(ind[:, j * LANE:(j + 1) * LANE].astype(BF16), tri, preferred_element_type=F32)
        outs.append(pj + run)
        run = run + pj[:, LANE - 1:LANE]
    return outs[0] if len(outs) == 1 else jnp.concatenate(outs, axis=1)


def _topk_mask(scores, k):
    key = _monotone_key(scores)
    tau = _kth_largest_key(key, float(k))
    gt = key > tau
    eq = key == tau
    need = float(k) - jnp.sum(jnp.where(gt, 1.0, 0.0), axis=-1, keepdims=True)
    prefix = _prefix_count(jnp.where(eq, 1.0, 0.0))
    return gt | (eq & (prefix <= need))


def _dsa_prompt_kernel(q_ref, k_ref, v_ref, qi_ref, ki_ref, wi_ref, o_ref, key_s, sc_s, m_s, l_s, acc_s, *,
                       tq, t, kc, topk, h_idx, d_idx, n_heads, dh):
    q0 = pl.program_id(1) * tq
    n_kc = t // kc
    needed = (q0 + tq + kc - 1) // kc
    qpos = q0 + lax.broadcasted_iota(jnp.int32, (tq, 1), 0)
    qi = [qi_ref[0, :, h * d_idx:(h + 1) * d_idx].astype(BF16) for h in range(h_idx)]
    wi = [wi_ref[0, :, h:h + 1] for h in range(h_idx)]

    def score_chunk(c, carry):
        keys = ki_ref[0, pl.ds(pl.multiple_of(c * kc, kc), kc), :].astype(BF16)
        acc = jnp.zeros((tq, kc), F32)
        for h in range(h_idx):
            acc = acc + jnp.maximum(_dot_nt(qi[h], keys), 0.0) * wi[h]
        spos = c * kc + lax.broadcasted_iota(jnp.int32, (1, kc), 1)
        key_s[c] = _monotone_key(jnp.where(spos <= qpos, acc, -jnp.inf))
        return carry

    lax.fori_loop(0, needed, score_chunk, 0)

    def count(pred):
        def body(c, tot):
            ind = jnp.where(pred(key_s[c]), 1.0, 0.0)
            for j in range(kc // LANE):
                tot = tot + ind[:, j * LANE:(j + 1) * LANE]
            return tot
        lanes = lax.fori_loop(0, needed, body, jnp.zeros((tq, LANE), F32))
        return jnp.sum(lanes, axis=-1, keepdims=True)

    def search(i, tau):
        cand = tau + jnp.left_shift(jnp.int32(1), jnp.int32(31) - i)
        return jnp.where(count(lambda key: key >= cand) >= float(topk), cand, tau)

    tau = lax.fori_loop(0, 32, search, jnp.full((tq, 1), -2 ** 31, jnp.int32))
    need = float(topk) - count(lambda key: key > tau)

    def mask_chunk(c, run):
        key = key_s[c]
        eq = jnp.where(key == tau, 1.0, 0.0)
        prefix = _prefix_count(eq) + run
        spos = c * kc + lax.broadcasted_iota(jnp.int32, (1, kc), 1)
        sel = (key > tau) | ((key == tau) & (prefix <= need))
        sc_s[c] = jnp.where(sel & (spos <= qpos), 1.0, 0.0)
        return prefix[:, kc - 1:kc]

    lax.fori_loop(0, needed, mask_chunk, jnp.zeros((tq, 1), F32))

    m_s[...] = jnp.full(m_s.shape, NEG_BIG, F32)
    l_s[...] = jnp.zeros(l_s.shape, F32)
    acc_s[...] = jnp.zeros(acc_s.shape, F32)
    scale = dh ** -0.5

    def attn_chunk(c, carry):
        rows = pl.ds(pl.multiple_of(c * kc, kc), kc)
        ok = sc_s[c] > 0.5
        heads = range(n_heads)
        sls = [slice(h * dh, (h + 1) * dh) for h in heads]
        lg = [_dot_nt(q_ref[0, :, sls[h]], k_ref[0, rows, sls[h]]) for h in heads]
        lg = [jnp.where(ok, x * scale, NEG_BIG) for x in lg]
        m_old = [m_s[h] for h in heads]
        m_new = [jnp.maximum(m_old[h], jnp.max(lg[h], axis=-1, keepdims=True)) for h in heads]
        p = [jnp.where(ok, jnp.exp(lg[h] - m_new[h]), 0.0) for h in heads]
        pv = [_dot(p[h], v_ref[0, rows, sls[h]]) for h in heads]
        for h in heads:
            alpha = jnp.exp(m_old[h] - m_new[h])
            l_s[h] = alpha * l_s[h] + jnp.sum(p[h], axis=-1, keepdims=True)
            acc_s[:, sls[h]] = alpha * acc_s[:, sls[h]] + pv[h]
            m_s[h] = m_new[h]
        return carry

    lax.fori_loop(0, needed, attn_chunk, 0)
    for h in range(n_heads):
        sl = slice(h * dh, (h + 1) * dh)
        o_ref[0, :, sl] = (acc_s[:, sl] / l_s[h]).astype(o_ref.dtype)


def _dsa_prompt(q, k, v, qi, ki, wi, dims):
    b, t, cc = q.shape
    tq = _tile(t, 256, SUBLANE)
    kc = _tile(t, 512, tq)
    topk = min(TOPK_MAX, t // 4)
    assert kc >= topk
    h_idx, d_idx, n_heads = dims["H_IDX"], dims["D_IDX"], dims["H_C"]
    qb = lambda w: pl.BlockSpec((1, tq, w), lambda i, j: (i, j, 0))
    full = lambda w: pl.BlockSpec((1, t, w), lambda i, j: (i, 0, 0))
    return pl.pallas_call(
        functools.partial(_dsa_prompt_kernel, tq=tq, t=t, kc=kc, topk=topk, h_idx=h_idx, d_idx=d_idx,
                          n_heads=n_heads, dh=dims["DH_C"]),
        grid=(b, t // tq),
        in_specs=[qb(cc), full(cc), full(cc), qb(h_idx * d_idx), full(d_idx), qb(h_idx)],
        out_specs=qb(cc),
        out_shape=jax.ShapeDtypeStruct((b, t, cc), BF16),
        scratch_shapes=[pltpu.VMEM((t // kc, tq, kc), jnp.int32), pltpu.VMEM((t // kc, tq, kc), F32),
                        pltpu.VMEM((n_heads, tq, 1), F32),
                        pltpu.VMEM((n_heads, tq, 1), F32), pltpu.VMEM((tq, cc), F32)],
        compiler_params=_cparams(2),
        name="dsa_prompt",
    )(q, k, v, qi, ki, wi)


def _idx_rows_scores(qi2, wi2, keys, tq, h_idx):
    s = jnp.maximum(_dot_nt(qi2, keys), 0.0) * wi2
    acc = s[0:tq]
    for h in range(1, h_idx):
        acc = acc + s[h * tq:(h + 1) * tq]
    return acc


def _dsa_sample_scores_kernel(pt_ref, qi_ref, wi_ref, *refs, tq, h_idx):
    o_ref = refs[-1]
    for g, kc_ref in enumerate(refs[:-1]):
        o_ref[0, :, g * PAGE:(g + 1) * PAGE] = _idx_rows_scores(qi_ref[0], wi_ref[0], kc_ref[0, 0], tq, h_idx)


def _dsa_sample_scores(page_table, qi2, wi2, cache_kidx, l, tq, h_idx):
    b, n_pages = page_table.shape
    d_idx = cache_kidx.shape[-1]
    rows = qi2.shape[1]
    group = _tile(n_pages, 16, 1)
    page = lambda g: pl.BlockSpec((1, 1, PAGE, d_idx), lambda i, p, pt: (l, pt[i, p * group + g], 0, 0))
    grid_spec = pltpu.PrefetchScalarGridSpec(
        num_scalar_prefetch=1,
        grid=(b, n_pages // group),
        in_specs=[pl.BlockSpec((1, rows, d_idx), lambda i, p, pt: (i, 0, 0)),
                  pl.BlockSpec((1, rows, 1), lambda i, p, pt: (i, 0, 0))] + [page(g) for g in range(group)],
        out_specs=pl.BlockSpec((1, tq, group * PAGE), lambda i, p, pt: (i, 0, p)),
    )
    return pl.pallas_call(
        functools.partial(_dsa_sample_scores_kernel, tq=tq, h_idx=h_idx),
        grid_spec=grid_spec,
        out_shape=jax.ShapeDtypeStruct((b, tq, n_pages * PAGE), F32),
        compiler_params=_cparams(2),
        name="dsa_sample_scores",
    )(page_table, qi2, wi2, *([cache_kidx] * group))


def _dsa_sample_select_kernel(sc_ref, qi_ref, wi_ref, kn_ref, o_ref, on_ref, *, tq, h_idx, topk, past):
    new = _idx_rows_scores(qi_ref[0], wi_ref[0], kn_ref[0], tq, h_idx)
    qrow = lax.broadcasted_iota(jnp.int32, (tq, PAGE), 0)
    col = lax.broadcasted_iota(jnp.int32, (tq, PAGE), 1)
    new_ok = col <= qrow
    scores = jnp.concatenate([sc_ref[0], jnp.where(new_ok, new, -jnp.inf)], axis=1)
    sel = jnp.where(_topk_mask(scores, topk), 1.0, 0.0)
    o_ref[0] = sel[:, :past]
    on_ref[0] = jnp.where(new_ok, sel[:, past:], 0.0)


def _dsa_sample_select(scores, qi2, wi2, ki_new_pad, tq, h_idx):
    b, _, past = scores.shape
    topk = min(TOPK_MAX, (past + tq) // 4)
    rows, d_idx = qi2.shape[1], qi2.shape[2]
    return pl.pallas_call(
        functools.partial(_dsa_sample_select_kernel, tq=tq, h_idx=h_idx, topk=topk, past=past),
        grid=(b,),
        in_specs=[pl.BlockSpec((1, tq, past), lambda i: (i, 0, 0)),
                  pl.BlockSpec((1, rows, d_idx), lambda i: (i, 0, 0)),
                  pl.BlockSpec((1, rows, 1), lambda i: (i, 0, 0)),
                  pl.BlockSpec((1, PAGE, d_idx), lambda i: (i, 0, 0))],
        out_specs=[pl.BlockSpec((1, tq, past), lambda i: (i, 0, 0)), pl.BlockSpec((1, tq, PAGE), lambda i: (i, 0, 0))],
        out_shape=[jax.ShapeDtypeStruct((b, tq, past), F32), jax.ShapeDtypeStruct((b, tq, PAGE), F32)],
        compiler_params=_cparams(1),
        name="dsa_sample_select",
    )(scores, qi2, wi2, ki_new_pad)


def _dsa_sample_attn_kernel(pt_ref, q_ref, kn_ref, vn_ref, m_ref, mn_ref, *refs, n_steps, group, n_heads, dh, tq):
    kc_refs, vc_refs = refs[:group], refs[group:2 * group]
    o_ref, q2_s, exp_s, hm_s, m_s, l_s, acc_s = refs[2 * group:]
    p = pl.program_id(1)
    rows = n_heads * tq
    cols = PAGE * n_heads

    @pl.when(p == 0)
    def _():
        for h in range(n_heads):
            q2_s[h * tq:(h + 1) * tq, :] = q_ref[0, :, h * dh:(h + 1) * dh]
        pos = lax.broadcasted_iota(jnp.int32, (PAGE, cols), 0)
        col = lax.broadcasted_iota(jnp.int32, (PAGE, cols), 1)
        exp_s[...] = jnp.where(col // n_heads == pos, 1.0, 0.0).astype(exp_s.dtype)
        rr = lax.broadcasted_iota(jnp.int32, (rows, cols), 0)
        cc = lax.broadcasted_iota(jnp.int32, (rows, cols), 1)
        hm_s[...] = jnp.where((cc % n_heads) == (rr // tq), 1.0, 0.0)
        m_s[...] = jnp.full(m_s.shape, NEG_BIG, F32)
        l_s[...] = jnp.zeros(l_s.shape, F32)
        acc_s[...] = jnp.zeros(acc_s.shape, F32)

    def update(kmats, vmats, sels):
        q2, hm, exp_m = q2_s[...], hm_s[...], exp_s[...]
        lgs = [_dot_nt(q2, km) * (dh ** -0.5) for km in kmats]
        sel8 = [jnp.dot(s.astype(BF16), exp_m, preferred_element_type=F32) for s in sels]
        msks = [jnp.concatenate([s8] * n_heads, axis=0) * hm > 0.5 for s8 in sel8]
        lgm = [jnp.where(mk, lg, NEG_BIG) for mk, lg in zip(msks, lgs)]
        m_new = m_s[...]
        for x in lgm:
            m_new = jnp.maximum(m_new, jnp.max(x, axis=-1, keepdims=True))
        alpha = jnp.exp(m_s[...] - m_new)
        pms = [jnp.where(mk, jnp.exp(x - m_new), 0.0) for mk, x in zip(msks, lgm)]
        pvs = [_dot(pm, vm) for pm, vm in zip(pms, vmats)]
        l_new = alpha * l_s[...]
        acc_new = alpha * acc_s[...]
        for pm, pv in zip(pms, pvs):
            l_new = l_new + jnp.sum(pm, axis=-1, keepdims=True)
            acc_new = acc_new + pv
        l_s[...] = l_new
        acc_s[...] = acc_new
        m_s[...] = m_new

    @pl.when(p < n_steps)
    def _():
        update([kc_refs[g][0, 0].reshape(cols, dh) for g in range(group)],
               [vc_refs[g][0, 0].reshape(cols, dh) for g in range(group)],
               [m_ref[0, :, g * PAGE:(g + 1) * PAGE] for g in range(group)])

    @pl.when(p == n_steps)
    def _():
        update([kn_ref[0].reshape(cols, dh)], [vn_ref[0].reshape(cols, dh)], [mn_ref[0]])
        res = acc_s[...] / l_s[...]
        for h in range(n_heads):
            o_ref[0, :, h * dh:(h + 1) * dh] = res[h * tq:(h + 1) * tq, :].astype(o_ref.dtype)


def _dsa_sample_attn(page_table, q, cache_k, cache_v, k_new_pad, v_new_pad, mask, mask_new, l, dims):
    b, n_pages = page_table.shape
    tq = q.shape[1]
    n_heads, dh = dims["H_C"], dims["DH_C"]
    cc = n_heads * dh
    group = _tile(n_pages, 8, 1)
    n_steps = n_pages // group
    page = lambda g: pl.BlockSpec(
        (1, 1, PAGE, n_heads, dh),
        lambda i, p, pt: (l, pt[i, jnp.minimum(p * group + g, n_pages - 1)], 0, 0, 0))
    new = pl.BlockSpec((1, PAGE, n_heads, dh), lambda i, p, pt: (i, 0, 0, 0))
    grid_spec = pltpu.PrefetchScalarGridSpec(
        num_scalar_prefetch=1,
        grid=(b, n_steps + 1),
        in_specs=[pl.BlockSpec((1, tq, cc), lambda i, p, pt: (i, 0, 0)), new, new,
                  pl.BlockSpec((1, tq, group * PAGE), lambda i, p, pt: (i, 0, jnp.minimum(p, n_steps - 1))),
                  pl.BlockSpec((1, tq, PAGE), lambda i, p, pt: (i, 0, 0))]
                 + [page(g) for g in range(group)] * 2,
        out_specs=pl.BlockSpec((1, tq, cc), lambda i, p, pt: (i, 0, 0)),
        scratch_shapes=[pltpu.VMEM((n_heads * tq, dh), BF16), pltpu.VMEM((PAGE, PAGE * n_heads), BF16),
                        pltpu.VMEM((n_heads * tq, PAGE * n_heads), F32), pltpu.VMEM((n_heads * tq, 1), F32),
                        pltpu.VMEM((n_heads * tq, 1), F32), pltpu.VMEM((n_heads * tq, dh), F32)],
    )
    return pl.pallas_call(
        functools.partial(_dsa_sample_attn_kernel, n_steps=n_steps, group=group, n_heads=n_heads, dh=dh, tq=tq),
        grid_spec=grid_spec,
        out_shape=jax.ShapeDtypeStruct((b, tq, cc), BF16),
        compiler_params=_cparams(2),
        name="dsa_sample_attn",
    )(page_table, q, k_new_pad, v_new_pad, mask, mask_new, *([cache_k] * group), *([cache_v] * group))


def _dsa_sample(q, k_f32, v_f32, qi, ki, wi, cache_k, cache_v, cache_kidx, page_table, l, dims):
    b, tq, cc = q.shape
    h_idx, d_idx, n_heads, dh = dims["H_IDX"], dims["D_IDX"], dims["H_C"], dims["DH_C"]
    qi2 = jnp.transpose(qi.reshape(b, tq, h_idx, d_idx), (0, 2, 1, 3)).reshape(b, h_idx * tq, d_idx)
    wi2 = jnp.transpose(wi, (0, 2, 1)).reshape(b, h_idx * tq, 1)
    pad_rows = lambda x: jnp.pad(x, ((0, 0), (0, PAGE - tq)) + ((0, 0),) * (x.ndim - 2))
    scores = _dsa_sample_scores(page_table, qi2, wi2, cache_kidx, l, tq, h_idx)
    mask, mask_new = _dsa_sample_select(scores, qi2, wi2, pad_rows(ki), tq, h_idx)
    return _dsa_sample_attn(page_table, q, cache_k, cache_v, pad_rows(k_f32), pad_rows(v_f32), mask, mask_new, l,
                            dims)


def _prepare_weights(raw, dims):
    ca, cb, cc, qiw, d_idx, h_idx, d = (dims[k] for k in ("C_A", "C_B", "C_C", "QIW", "D_IDX", "H_IDX", "D"))
    dd, da, dg = dims["D_DECAY"], dims["D_AAA"], dims["D_GATE"]
    assert dd <= LANE and da <= LANE and dg % LANE == 0
    a_cols = 3 * ca + dd + da + dg
    o = [0, a_cols, a_cols + cb, a_cols + cb + cc, a_cols + cb + 2 * cc, a_cols + cb + 3 * cc]
    o += [o[-1] + qiw, o[-1] + qiw + d_idx, o[-1] + qiw + d_idx + h_idx]
    w_in = raw["w_in"]

    def pack_pa(x):
        zeros = lambda n: jnp.zeros(x.shape[:-1] + (n,), x.dtype)
        return jnp.concatenate([x[..., :3 * ca], x[..., 3 * ca:3 * ca + dd], zeros(LANE - dd),
                                x[..., 3 * ca + dd:3 * ca + dd + da], zeros(LANE - da),
                                x[..., 3 * ca + dd + da:a_cols]], axis=-1)

    assert d_idx + h_idx <= LANE and cb % LANE == 0 and cc % LANE == 0 and qiw % LANE == 0 and d % LANE == 0
    w_in_t = jnp.swapaxes(w_in, 1, 2)
    zero_rows = lambda n: jnp.zeros((w_in.shape[0], n, w_in.shape[1]), w_in.dtype)
    w_pa = jnp.concatenate([w_in_t[:, :3 * ca + dd], zero_rows(LANE - dd), w_in_t[:, 3 * ca + dd:3 * ca + dd + da],
                            zero_rows(LANE - da), w_in_t[:, 3 * ca + dd + da:a_cols]], axis=1)
    dims["MID_ROWS"] = (o[1], cb + 3 * cc + qiw + LANE)
    dims["GATE_ROWS"] = (o[8], 3 * d)
    assert o[1] + dims["MID_ROWS"][1] <= w_in.shape[2]
    pad_rows = lambda x: jnp.pad(x, ((0, 0), (0, LANE - x.shape[1]), (0, 0)))
    depth = w_in.shape[0]
    vec = lambda x: x.reshape(depth, 1, -1)
    P = dict(
        w_pa=w_pa, w_in_t=w_in_t,
        mu=vec(pack_pa(raw["rwkv_mu"])), w0=vec(raw["rwkv_w0"]), w2=pad_rows(raw["rwkv_w2"]).astype(BF16),
        a0=vec(raw["rwkv_a0"]), a2=pad_rows(raw["rwkv_a2"]).astype(BF16), g2=raw["rwkv_g2"].astype(BF16),
        k_k=vec(raw["rwkv_k_k"]), k_a=vec(raw["rwkv_k_a"]), r_k=vec(raw["rwkv_r_k"]),
        ln_w=vec(raw["rwkv_ln_w"]), ln_b=vec(raw["rwkv_ln_b"]),
        pool_w=raw["pool_w"].astype(BF16), pool_scale=raw["pool_scale"],
    )
    for name in ("ffn1_w_down", "ffn2_w_down", "w_br_a", "w_br_b", "w_br_c", "w_out"):
        P[name] = raw[name].astype(BF16)
    for name in ("ffn1_w_gate", "ffn1_w_up", "ffn2_w_gate", "ffn2_w_up", "ffn1_norm", "mix_norm", "ffn2_norm"):
        P[name] = raw[name]
    P["pack_pa"] = pack_pa
    return P


def _unpack_pa(x, dims):
    ca, dd, da = dims["C_A"], dims["D_DECAY"], dims["D_AAA"]
    return jnp.concatenate([x[..., :3 * ca], x[..., 3 * ca:3 * ca + dd],
                            x[..., 3 * ca + LANE:3 * ca + LANE + da], x[..., 3 * ca + 2 * LANE:]], axis=-1)


def _branches(pa, mid, grp, P, l, dims):
    row0, b, t, pos0 = grp["row0"], grp["b"], grp["t"], grp["pos0"]
    m = b * t
    ca, cb, cc = dims["C_A"], dims["C_B"], dims["C_C"]

    def last_rows(x, n, width):
        idx = (row0 + t - n + jnp.arange(b)[:, None] * t + jnp.arange(n)[None, :]).reshape(-1)
        return jnp.take(x, idx, axis=0)[:, :width].reshape(b, n, width)

    chunk = min(N_A, t)
    rt, at, bt, kt, v_a, bkt, gct, bonus, g = _rwkv_prep(pa, row0, b, t, P["pack_pa"](grp["shift"]), P, l, dims,
                                                          chunk)
    y_a, new_wkv = _rwkv_scan(rt, at, bt, kt, v_a, bkt, gct, grp["wkv"], chunk)
    ya = _rwkv_post(y_a.reshape(m, ca), bonus.reshape(m, ca), g.reshape(m, ca), P["ln_w"][l], P["ln_b"][l])
    new_shift = _unpack_pa(last_rows(pa, 1, pa.shape[1])[:, 0], dims)

    hist16 = jnp.pad(grp["pool"], ((0, 0), (2 * SUBLANE - POOL_HIST, 0), (0, 0)))
    yb = _pool(mid, row0, b, t, hist16, P["pool_w"][l], P["pool_scale"][l], pos0, dims)
    new_pool = jnp.concatenate([grp["pool"], last_rows(mid, min(t, POOL_HIST), cb)], axis=1)[:, -POOL_HIST:]

    q, k, v, k_bf, v_bf, qi, kiwi = _rope_all(mid, row0, b, t, pos0, dims)
    ki = kiwi[..., :dims["D_IDX"]]
    wi = kiwi[..., dims["D_IDX"]:dims["D_IDX"] + dims["H_IDX"]]
    if grp["cache"] is None:
        yc = _dsa_prompt(q, k_bf, v_bf, qi, ki, wi, dims)
    else:
        cache_k, cache_v, cache_kidx, page_table = grp["cache"]
        yc = _dsa_sample(q, k, v, qi, ki, wi, cache_k, cache_v, cache_kidx, page_table, l, dims)
    return (ya, yb.reshape(m, cb), yc.reshape(m, cc)), (k, v, ki, new_wkv, new_shift, new_pool)


def _layer(carry, groups, P, l, next_g, dims):
    h, hb, ssq = carry
    h, u, ssq = _ffn(h, hb, ssq, P["ffn1_w_gate"], P["ffn1_w_up"], P["ffn1_w_down"], l, P["mix_norm"][l])
    pa = _mm(u, P["w_pa"], l, out_dtype=F32, ssq=ssq, w_t=True, name="proj_pa")
    mid = _mm(u, P["w_in_t"], l, out_dtype=F32, ssq=ssq, tm_target=1024, w_t=True, w_rows=dims["MID_ROWS"],
              name="proj_mid")
    gates = _mm(u, P["w_in_t"], l, out_dtype=BF16, ssq=ssq, act="sigmoid", tn_target=512, w_t=True,
                w_rows=dims["GATE_ROWS"], name="proj_gate")
    outs = [_branches(pa, mid, grp, P, l, dims) for grp in groups]
    ya, yb, yc = (jnp.concatenate([o[0][i] for o in outs], axis=0) for i in range(3))
    merged = _merge(ya, yb, yc, P["w_br_a"], P["w_br_b"], P["w_br_c"], l, gates)
    h, hb, ssq = _mm(merged, P["w_out"], l, out_dtype=F32, res=h, next_g=P["ffn2_norm"][l], tn_target=512,
                     name="w_out")
    carry = _ffn(h, hb, ssq, P["ffn2_w_gate"], P["ffn2_w_up"], P["ffn2_w_down"], l, next_g)
    return carry, [o[1] for o in outs]


def kernel(x_prompt, x_sample, cache_k, cache_v, cache_kidx, state_wkv, state_shift, state_pool, page_table,
           ffn1_norm, ffn1_w_gate, ffn1_w_up, ffn1_w_down, mix_norm, w_in, rwkv_mu, rwkv_w0, rwkv_w2,
           rwkv_a0, rwkv_a2, rwkv_g2, rwkv_k_k, rwkv_k_a, rwkv_r_k, rwkv_ln_w, rwkv_ln_b, w_br_a, pool_w,
           pool_scale, w_br_b, w_br_c, w_out, ffn2_norm, ffn2_w_gate, ffn2_w_up, ffn2_w_down, final_norm):
    raw = dict(ffn1_norm=ffn1_norm, ffn1_w_gate=ffn1_w_gate, ffn1_w_up=ffn1_w_up, ffn1_w_down=ffn1_w_down,
               mix_norm=mix_norm, w_in=w_in, rwkv_mu=rwkv_mu, rwkv_w0=rwkv_w0, rwkv_w2=rwkv_w2, rwkv_a0=rwkv_a0,
               rwkv_a2=rwkv_a2, rwkv_g2=rwkv_g2, rwkv_k_k=rwkv_k_k, rwkv_k_a=rwkv_k_a, rwkv_r_k=rwkv_r_k,
               rwkv_ln_w=rwkv_ln_w, rwkv_ln_b=rwkv_ln_b, w_br_a=w_br_a, pool_w=pool_w, pool_scale=pool_scale,
               w_br_b=w_br_b, w_br_c=w_br_c, w_out=w_out, ffn2_norm=ffn2_norm, ffn2_w_gate=ffn2_w_gate,
               ffn2_w_up=ffn2_w_up, ffn2_w_down=ffn2_w_down)
    depth, d = mix_norm.shape
    ca = rwkv_w0.shape[-1]
    cb = pool_scale.shape[-1]
    n_heads, dh = cache_k.shape[3], cache_k.shape[4]
    cc = n_heads * dh
    d_idx = cache_kidx.shape[-1]
    dd, da, dg = rwkv_w2.shape[1], rwkv_a2.shape[1], rwkv_g2.shape[1]
    a_cols = 3 * ca + dd + da + dg
    h_idx = (w_in.shape[-1] - a_cols - cb - 3 * cc - d_idx - 3 * d) // (d_idx + 1)
    dims = dict(D=d, C_A=ca, C_B=cb, C_C=cc, H_C=n_heads, DH_C=dh, D_IDX=d_idx, H_IDX=h_idx,
                QIW=h_idx * d_idx, D_DECAY=dd, D_AAA=da, D_GATE=dg)
    assert w_in.shape[-1] == a_cols + cb + 3 * cc + h_idx * d_idx + d_idx + h_idx + 3 * d
    P = _prepare_weights(raw, dims)

    bp = x_prompt.shape[0]
    past = page_table.shape[1] * PAGE
    bs, ts = x_sample.shape[:2]
    tp = x_prompt.shape[1]
    mp, ms = bp * tp, bs * ts
    x_all = jnp.concatenate([x_prompt.reshape(mp, d), x_sample.reshape(ms, d)], axis=0)
    carry = (x_all,) + tuple(_prenorm(x_all, ffn1_norm[0]))
    st_p, st_s = [], []
    for l in range(depth):
        next_g = ffn1_norm[l + 1] if l + 1 < depth else final_norm
        groups = [
            dict(row0=0, b=bp, t=tp, pos0=0, cache=None, shift=jnp.zeros((bp, a_cols), F32),
                 wkv=jnp.zeros((bp, ca // N_A, N_A, N_A), F32), pool=jnp.zeros((bp, POOL_HIST, cb), F32)),
            dict(row0=mp, b=bs, t=ts, pos0=past, cache=(cache_k, cache_v, cache_kidx, page_table),
                 shift=state_shift[l], wkv=state_wkv[l], pool=state_pool[l]),
        ]
        carry, (sp, ss) = _layer(carry, groups, P, l, next_g, dims)
        st_p.append(sp)
        st_s.append(ss)
    y_p = _rmsnorm(carry[0], 0, mp, final_norm, F32).reshape(x_prompt.shape)
    y_s = _rmsnorm(carry[0], mp, ms, final_norm, F32).reshape(x_sample.shape)
    stk = lambda sts, i: jnp.stack([s[i] for s in sts])
    return (y_p, y_s,
            stk(st_p, 0), stk(st_p, 1), stk(st_p, 2), stk(st_p, 3), stk(st_p, 4), stk(st_p, 5),
            stk(st_s, 0), stk(st_s, 1), stk(st_s, 2), stk(st_s, 3), stk(st_s, 4), stk(st_s, 5))
```

```python
import functools
import math

import jax
import jax.numpy as jnp
from jax import lax
from jax.experimental import pallas as pl
from jax.experimental.pallas import tpu as pltpu

F32 = jnp.float32
BF16 = jnp.bfloat16

LANE = 128
SUBLANE = 8
VMEM_LIMIT_BYTES = 56 * 1024 * 1024
ROW_TILE = 1400

N_A = 64
GN_EPS = 64e-5
POOL_WINDOWS = (2, 4, 8, 16)
POOL_HIST = max(POOL_WINDOWS) - 1
TOPK_MAX = 256
ROPE_THETA = 10000.0
RMS_EPS = 1e-6
PAGE = 128
NEG_BIG = -1e30


def _cparams(n_axes):
    return pltpu.CompilerParams(dimension_semantics=("arbitrary",) * n_axes,
                                vmem_limit_bytes=VMEM_LIMIT_BYTES)


def _tile(n, target, mult):
    best = None
    for t in range(mult, min(n, target) + 1, mult):
        if n % t == 0:
            best = t
    return best if best is not None else n


def _roundup(n, m):
    return -(-n // m) * m


def _dot(a, b):
    return jnp.dot(a.astype(BF16), b.astype(BF16), preferred_element_type=F32)


def _dot_nt(a, b):
    return lax.dot_general(a.astype(BF16), b.astype(BF16), (((1,), (1,)), ((), ())),
                           preferred_element_type=F32)


def _split3(x):
    h1 = x.astype(BF16)
    r1 = x - h1.astype(F32)
    h2 = r1.astype(BF16)
    h3 = (r1 - h2.astype(F32)).astype(BF16)
    return h1, h2, h3


def _dot_exact_rhs(x, m_bf16):
    h1, h2, h3 = _split3(x)
    return (jnp.dot(h1, m_bf16, preferred_element_type=F32)
            + jnp.dot(h2, m_bf16, preferred_element_type=F32)
            + jnp.dot(h3, m_bf16, preferred_element_type=F32))


def _dot_exact_lhs(m_bf16, x):
    h1, h2, h3 = _split3(x)
    return (jnp.dot(m_bf16, h1, preferred_element_type=F32)
            + jnp.dot(m_bf16, h2, preferred_element_type=F32)
            + jnp.dot(m_bf16, h3, preferred_element_type=F32))


def _rmsnorm_kernel(x_ref, g_ref, o_ref):
    x = x_ref[...]
    ms = jnp.mean(x * x, axis=-1, keepdims=True)
    o_ref[...] = (x * lax.rsqrt(ms + RMS_EPS) * g_ref[...]).astype(o_ref.dtype)


def _rmsnorm(x2d, row0, rows, g, out_dtype):
    d = x2d.shape[1]
    tm = _tile(rows, 256, SUBLANE)
    assert row0 % tm == 0
    return pl.pallas_call(
        _rmsnorm_kernel,
        grid=(rows // tm,),
        in_specs=[pl.BlockSpec((tm, d), lambda i: (row0 // tm + i, 0)), pl.BlockSpec((1, d), lambda i: (0, 0))],
        out_specs=pl.BlockSpec((tm, d), lambda i: (i, 0)),
        out_shape=jax.ShapeDtypeStruct((rows, d), out_dtype),
        compiler_params=_cparams(1),
        name="rmsnorm",
    )(x2d, g.reshape(1, d))


def _prenorm_kernel(x_ref, g_ref, xb_ref, ssq_ref):
    x = x_ref[...]
    xb_ref[...] = (x * g_ref[...]).astype(xb_ref.dtype)
    ssq_ref[...] = jnp.sum(x * x, axis=-1, keepdims=True)


def _prenorm(x2d, g):
    m, d = x2d.shape
    tm = _tile(m, 256, SUBLANE)
    return pl.pallas_call(
        _prenorm_kernel,
        grid=(m // tm,),
        in_specs=[pl.BlockSpec((tm, d), lambda i: (i, 0)), pl.BlockSpec((1, d), lambda i: (0, 0))],
        out_specs=[pl.BlockSpec((tm, d), lambda i: (i, 0)), pl.BlockSpec((tm, 1), lambda i: (i, 0))],
        out_shape=[jax.ShapeDtypeStruct((m, d), BF16), jax.ShapeDtypeStruct((m, 1), F32)],
        compiler_params=_cparams(1),
        name="prenorm",
    )(x2d, g.reshape(1, d))


def _row_scale(ssq, width):
    return lax.rsqrt(ssq * (1.0 / width) + RMS_EPS)


def _mm_kernel(*refs, nk, kdim, w_t, has_ssq, has_res, has_next, res_scale, act):
    it = iter(refs)
    x_ref, w_ref = next(it), next(it)
    ssq_ref = next(it) if has_ssq else None
    res_ref = next(it) if has_res else None
    gnext_ref = next(it) if has_next else None
    o_ref = next(it)
    ob_ref, ssqo_ref = (next(it), next(it)) if has_next else (None, None)

    def epilogue(acc):
        if has_ssq:
            acc = acc * _row_scale(ssq_ref[...], kdim)
        if act == "sigmoid":
            acc = jax.nn.sigmoid(acc)
        if has_res:
            acc = res_ref[...] + res_scale * acc
        o_ref[...] = acc.astype(o_ref.dtype)
        if has_next:
            ob_ref[...] = (acc * gnext_ref[...]).astype(ob_ref.dtype)
            part_ssq = jnp.sum(acc * acc, axis=-1, keepdims=True)
            j = pl.program_id(1)

            @pl.when(j == 0)
            def _():
                ssqo_ref[...] = part_ssq

            @pl.when(j > 0)
            def _():
                ssqo_ref[...] += part_ssq

    if w_t:
        part = lax.dot_general(x_ref[...], w_ref[0].astype(BF16), (((1,), (1,)), ((), ())),
                               preferred_element_type=F32)
    else:
        part = jnp.dot(x_ref[...], w_ref[0], preferred_element_type=F32)
    if nk == 1:
        epilogue(part)
    else:
        acc_ref = next(it)
        k = pl.program_id(2)

        @pl.when(k == 0)
        def _():
            acc_ref[...] = part

        @pl.when(k > 0)
        def _():
            acc_ref[...] += part

        @pl.when(k == nk - 1)
        def _():
            epilogue(acc_ref[...])


def _mm(x, w, l, *, out_dtype, ssq=None, res=None, res_scale=1.0, act=None, next_g=None, tm_target=ROW_TILE,
        tn_target=1024, tk_target=4096, w_t=False, w_rows=None, name="mm"):
    m, kdim = x.shape
    n = w_rows[1] if w_rows is not None else (w.shape[1] if w_t else w.shape[2])
    tm = _tile(m, tm_target, 2 * SUBLANE)
    tn = _tile(n, tn_target, LANE)
    tk = _tile(kdim, tk_target, LANE)
    nk = kdim // tk
    tile = pl.BlockSpec((tm, tn), lambda i, j, k: (i, j))
    col = pl.BlockSpec((tm, 1), lambda i, j, k: (i, 0))
    if w_rows is not None:
        assert w_t and w_rows[0] % (2 * SUBLANE) == 0
        wspec = pl.BlockSpec((pl.Element(1), pl.Element(tn), pl.Element(tk)),
                             lambda i, j, k: (l, pl.multiple_of(w_rows[0] + j * tn, 2 * SUBLANE),
                                              pl.multiple_of(k * tk, LANE)))
    elif w_t:
        wspec = pl.BlockSpec((1, tn, tk), lambda i, j, k: (l, j, k))
    else:
        wspec = pl.BlockSpec((1, tk, tn), lambda i, j, k: (l, k, j))
    in_specs = [pl.BlockSpec((tm, tk), lambda i, j, k: (i, k)), wspec]
    args = [x, w]
    if ssq is not None:
        in_specs.append(col)
        args.append(ssq)
    if res is not None:
        in_specs.append(tile)
        args.append(res)
    out_specs, out_shape = tile, jax.ShapeDtypeStruct((m, n), out_dtype)
    if next_g is not None:
        assert res is not None
        in_specs.append(pl.BlockSpec((1, tn), lambda i, j, k: (0, j)))
        args.append(next_g.reshape(1, n))
        out_specs = [tile, tile, col]
        out_shape = [out_shape, jax.ShapeDtypeStruct((m, n), BF16), jax.ShapeDtypeStruct((m, 1), F32)]
    scratch = [pltpu.VMEM((tm, tn), F32)] if nk > 1 else []
    return pl.pallas_call(
        functools.partial(_mm_kernel, nk=nk, kdim=kdim, w_t=w_t, has_ssq=ssq is not None, has_res=res is not None,
                          has_next=next_g is not None, res_scale=res_scale, act=act),
        grid=(m // tm, n // tn, nk),
        in_specs=in_specs,
        out_specs=out_specs,
        out_shape=out_shape,
        scratch_shapes=scratch,
        compiler_params=_cparams(3),
        name=name,
    )(*args)


def _swiglu_up_kernel(x_ref, ssq_ref, wg_ref, wu_ref, o_ref, *, d):
    x = x_ref[...]
    scale = _row_scale(ssq_ref[...], d)
    g = jnp.dot(x, wg_ref[0].astype(BF16), preferred_element_type=F32) * scale
    u = jnp.dot(x, wu_ref[0].astype(BF16), preferred_element_type=F32) * scale
    o_ref[...] = (g * jax.nn.sigmoid(g) * u).astype(o_ref.dtype)


def _swiglu_up(x, ssq, wg, wu, l):
    m, d = x.shape
    f = wg.shape[2]
    tm = _tile(m, ROW_TILE, 2 * SUBLANE)
    tn = _tile(f, 512, LANE)
    return pl.pallas_call(
        functools.partial(_swiglu_up_kernel, d=d),
        grid=(m // tm, f // tn),
        in_specs=[pl.BlockSpec((tm, d), lambda i, j: (i, 0)),
                  pl.BlockSpec((tm, 1), lambda i, j: (i, 0)),
                  pl.BlockSpec((1, d, tn), lambda i, j: (l, 0, j)),
                  pl.BlockSpec((1, d, tn), lambda i, j: (l, 0, j))],
        out_specs=pl.BlockSpec((tm, tn), lambda i, j: (i, j)),
        out_shape=jax.ShapeDtypeStruct((m, f), BF16),
        compiler_params=_cparams(2),
        name="swiglu_up",
    )(x, ssq, wg, wu)


def _ffn_down_kernel(a_ref, w_ref, res_ref, gnext_ref, o_ref, ob_ref, ssq_ref, acc_ref, *, nk, res_scale):
    k = pl.program_id(1)
    j = pl.program_id(2)
    part = jnp.dot(a_ref[...], w_ref[0], preferred_element_type=F32)

    if nk > 1:
        @pl.when(k == 0)
        def _():
            acc_ref[j] = part

        @pl.when((k > 0) & (k < nk - 1))
        def _():
            acc_ref[j] += part

    @pl.when(k == nk - 1)
    def _():
        acc = acc_ref[j] + part if nk > 1 else part
        h = res_ref[...] + res_scale * acc
        o_ref[...] = h
        ob_ref[...] = (h * gnext_ref[...]).astype(ob_ref.dtype)
        part_ssq = jnp.sum(h * h, axis=-1, keepdims=True)

        @pl.when(j == 0)
        def _():
            ssq_ref[...] = part_ssq

        @pl.when(j > 0)
        def _():
            ssq_ref[...] += part_ssq


def _ffn_down(a, w, l, res, res_scale, next_g):
    m, kdim = a.shape
    n = w.shape[2]
    tm = _tile(m, 1024, 2 * SUBLANE)
    tn = _tile(n, 512 if tm <= 768 else 256, LANE)
    tk = _tile(kdim, 5632, LANE)
    nk, nn = kdim // tk, n // tn
    last = lambda k, j: jnp.where(k == nk - 1, j, 0)
    tile = pl.BlockSpec((tm, tn), lambda i, k, j: (i, last(k, j)))
    col = pl.BlockSpec((tm, 1), lambda i, k, j: (i, 0))
    return pl.pallas_call(
        functools.partial(_ffn_down_kernel, nk=nk, res_scale=res_scale),
        grid=(m // tm, nk, nn),
        in_specs=[pl.BlockSpec((tm, tk), lambda i, k, j: (i, k)),
                  pl.BlockSpec((1, tk, tn), lambda i, k, j: (l, k, j)),
                  tile,
                  pl.BlockSpec((1, tn), lambda i, k, j: (0, last(k, j)))],
        out_specs=[tile, tile, col],
        out_shape=[jax.ShapeDtypeStruct((m, n), F32), jax.ShapeDtypeStruct((m, n), BF16),
                   jax.ShapeDtypeStruct((m, 1), F32)],
        scratch_shapes=[pltpu.VMEM((nn, tm, tn), F32)],
        compiler_params=_cparams(3),
        name="ffn_down",
    )(a, w, res, next_g.reshape(1, n))


def _ffn(h, hb, ssq, wg, wu, wd, l, next_g):
    a = _swiglu_up(hb, ssq, wg, wu, l)
    return _ffn_down(a, wd, l, h, 0.5, next_g)


def _merge_kernel(ya_ref, yb_ref, yc_ref, wa_ref, wb_ref, wc_ref, ga_ref, gb_ref, gc_ref, o_ref):
    a = jnp.dot(ya_ref[...], wa_ref[0], preferred_element_type=F32)
    b = jnp.dot(yb_ref[...], wb_ref[0], preferred_element_type=F32)
    c = jnp.dot(yc_ref[...], wc_ref[0], preferred_element_type=F32)
    out = ga_ref[...].astype(F32) * a + gb_ref[...].astype(F32) * b + gc_ref[...].astype(F32) * c
    o_ref[...] = out.astype(o_ref.dtype)


def _merge(ya, yb, yc, wa, wb, wc, l, gates):
    m = ya.shape[0]
    d = wa.shape[2]
    tm = _tile(m, ROW_TILE, 2 * SUBLANE)
    tn = _tile(d, 512, LANE)
    nb = d // tn
    wspec = lambda w: pl.BlockSpec((1, w.shape[1], tn), lambda i, j: (l, 0, j))
    return pl.pallas_call(
        _merge_kernel,
        grid=(m // tm, nb),
        in_specs=[pl.BlockSpec((tm, ya.shape[1]), lambda i, j: (i, 0)),
                  pl.BlockSpec((tm, yb.shape[1]), lambda i, j: (i, 0)),
                  pl.BlockSpec((tm, yc.shape[1]), lambda i, j: (i, 0)),
                  wspec(wa), wspec(wb), wspec(wc),
                  pl.BlockSpec((tm, tn), lambda i, j: (i, j)),
                  pl.BlockSpec((tm, tn), lambda i, j: (i, nb + j)),
                  pl.BlockSpec((tm, tn), lambda i, j: (i, 2 * nb + j))],
        out_specs=pl.BlockSpec((tm, tn), lambda i, j: (i, j)),
        out_shape=jax.ShapeDtypeStruct((m, d), BF16),
        compiler_params=_cparams(2),
        name="merge",
    )(ya, yb, yc, wa, wb, wc, gates, gates, gates)


def _rope_kernel(q_ref, k_ref, v_ref, qi_ref, kw_ref, c128_ref, s128_ref, c64_ref, s64_ref,
                 qo_ref, ko_ref, vo_ref, kb_ref, vb_ref, qio_ref, kwo_ref, *, n_heads, n_idx_groups, d_idx,
                 idx_scale):
    c128, s128 = c128_ref[...], s128_ref[...]
    c64, s64 = c64_ref[...], s64_ref[...]
    lane = lax.broadcasted_iota(jnp.int32, c64.shape, 1)
    first_half = (lane % d_idx) < (d_idx // 2)

    def rope128(x):
        return x * c128 + pltpu.roll(x, LANE // 2, axis=1) * s128

    def rope64(x):
        rot = jnp.where(first_half, pltpu.roll(x, LANE - d_idx // 2, axis=1), pltpu.roll(x, d_idx // 2, axis=1))
        return x * c64 + rot * s64

    for h in range(n_heads):
        sl = slice(h * LANE, (h + 1) * LANE)
        qo_ref[0, :, sl] = rope128(q_ref[0, :, sl]).astype(qo_ref.dtype)
        kr = rope128(k_ref[0, :, sl])
        ko_ref[0, :, h, :] = kr
        kb_ref[0, :, sl] = kr.astype(kb_ref.dtype)
        vo_ref[0, :, h, :] = v_ref[0, :, sl]
    vb_ref[0] = v_ref[0].astype(vb_ref.dtype)
    for g in range(n_idx_groups):
        sl = slice(g * LANE, (g + 1) * LANE)
        qio_ref[0, :, sl] = rope64(qi_ref[0, :, sl])
    kw = kw_ref[0]
    kwo_ref[0] = jnp.where(lane < d_idx, rope64(kw), kw * idx_scale)


def _rope_tables(pos, d):
    inv = ROPE_THETA ** (-jnp.arange(0, d, 2, dtype=F32) / d)
    ang = pos.astype(F32)[:, None] * inv[None, :]
    cos, sin = jnp.cos(ang), jnp.sin(ang)
    reps = LANE // d
    c = jnp.tile(jnp.concatenate([cos, cos], axis=-1), (1, reps))
    s = jnp.tile(jnp.concatenate([-sin, sin], axis=-1), (1, reps))
    return c, s


def _rope_all(mid_all, row0, b, t, pos0, dims):
    mid = mid_all.reshape((1,) + mid_all.shape)
    cb, cc, qiw, d_idx, h_idx = dims["C_B"], dims["C_C"], dims["QIW"], dims["D_IDX"], dims["H_IDX"]
    assert dims["DH_C"] == LANE and LANE % d_idx == 0
    assert cb % cc == 0 and (cb + 3 * cc) % qiw == 0
    tm = _tile(t, 256, SUBLANE)
    pos = pos0 + jnp.arange(t)
    c128, s128 = _rope_tables(pos, LANE)
    c64, s64 = _rope_tables(pos, d_idx)
    ob = cb // cc
    assert row0 % tm == 0
    row = lambda width, idx: pl.BlockSpec((1, tm, width), lambda i, j: (0, row0 // tm + i * (t // tm) + j, idx))
    tab = pl.BlockSpec((tm, LANE), lambda i, j: (j, 0))
    out = lambda width: pl.BlockSpec((1, tm, width), lambda i, j: (i, j, 0))
    split = pl.BlockSpec((1, tm, cc // LANE, LANE), lambda i, j: (i, j, 0, 0))
    return pl.pallas_call(
        functools.partial(_rope_kernel, n_heads=cc // LANE, n_idx_groups=qiw // LANE, d_idx=d_idx,
                          idx_scale=(h_idx * d_idx) ** -0.5),
        grid=(b, t // tm),
        in_specs=[row(cc, ob), row(cc, ob + 1), row(cc, ob + 2), row(qiw, (cb + 3 * cc) // qiw),
                  row(LANE, (cb + 3 * cc + qiw) // LANE), tab, tab, tab, tab],
        out_specs=[out(cc), split, split, out(cc), out(cc), out(qiw), out(LANE)],
        out_shape=[jax.ShapeDtypeStruct((b, t, cc), BF16), jax.ShapeDtypeStruct((b, t, cc // LANE, LANE), F32),
                   jax.ShapeDtypeStruct((b, t, cc // LANE, LANE), F32),
                   jax.ShapeDtypeStruct((b, t, cc), BF16), jax.ShapeDtypeStruct((b, t, cc), BF16),
                   jax.ShapeDtypeStruct((b, t, qiw), F32), jax.ShapeDtypeStruct((b, t, LANE), F32)],
        compiler_params=_cparams(2),
        name="rope",
    )(mid, mid, mid, mid, mid, c128, s128, c64, s64)


def _head_sum(x, jmat):
    parts = []
    for g in range(x.shape[1] // LANE):
        parts.append(_dot_exact_rhs(x[:, g * LANE:(g + 1) * LANE], jmat))
    return parts[0] if len(parts) == 1 else jnp.concatenate(parts, axis=1)


def _head_ones():
    r = lax.broadcasted_iota(jnp.int32, (LANE, LANE), 0)
    c = lax.broadcasted_iota(jnp.int32, (LANE, LANE), 1)
    return jnp.where((r // N_A) == (c // N_A), 1.0, 0.0).astype(BF16)


def _rwkv_prep_kernel(pa_ref, halo_ref, shift_ref, mu_ref, w0_ref, w2_ref, a0_ref, a2_ref, g2_ref,
                      kk_ref, ka_ref, rk_ref,
                      rt_ref, at_ref, bt_ref, kt_ref, v_ref, bkt_ref, gct_ref, bonus_ref, g_ref, *, ca, chunk, tm):
    j = pl.program_id(1)
    prev_last = jnp.where(j == 0, shift_ref[0], halo_ref[0, SUBLANE - 1:SUBLANE, :])
    row = lax.broadcasted_iota(jnp.int32, (tm, 1), 0)

    def shifted(lo, hi):
        x = pa_ref[0, :, lo:hi]
        prev = jnp.where(row == 0, prev_last[:, lo:hi], pltpu.roll(x, 1, axis=0))
        return x + mu_ref[:, lo:hi] * (prev - x)

    r = shifted(0, ca)
    k = shifted(ca, 2 * ca)
    v = shifted(2 * ca, 3 * ca)
    wl = shifted(3 * ca, 3 * ca + LANE)
    al = shifted(3 * ca + LANE, 3 * ca + 2 * LANE)
    gl = shifted(3 * ca + 2 * LANE, pa_ref.shape[2])

    z = -(w0_ref[...] + _dot(jnp.tanh(wl), w2_ref[...]))
    softplus = jnp.maximum(z, 0.0) + jnp.log(1.0 + jnp.exp(-jnp.abs(z)))
    log_decay = -jnp.exp(-softplus - 0.5)
    rr = lax.broadcasted_iota(jnp.int32, (tm, tm), 0)
    cc = lax.broadcasted_iota(jnp.int32, (tm, tm), 1)
    same_chunk = (rr // chunk) == (cc // chunk)
    tril = jnp.where(same_chunk & (cc <= rr), 1.0, 0.0).astype(BF16)
    ones_blk = jnp.where(same_chunk, 1.0, 0.0).astype(BF16)
    cum = _dot_exact_lhs(tril, log_decay)
    cum_c = _dot_exact_lhs(ones_blk, log_decay)
    fwd = jnp.exp(cum_c - cum)
    back = jnp.exp(cum - cum_c)
    back_prev = jnp.exp(cum - log_decay - cum_c)
    gam_c = jnp.exp(cum_c)

    a = jax.nn.sigmoid(a0_ref[...] + _dot(al, a2_ref[...]))
    jmat = _head_ones()
    kk = k * kk_ref[...]
    kk = kk / jnp.maximum(jnp.sqrt(_head_sum(kk * kk, jmat)), 1e-12)
    kh = k * (1.0 + (a - 1.0) * ka_ref[...])
    bonus_ref[0] = _head_sum(r * kh * rk_ref[...], jmat) * v
    g_ref[0] = _dot(jax.nn.sigmoid(gl), g2_ref[...])
    bh = kk * a * fwd
    kf = kh * fwd
    rt_ref[0] = (r * back).astype(rt_ref.dtype)
    at_ref[0] = (-kk * back_prev).astype(at_ref.dtype)
    bt_ref[0] = bh.astype(bt_ref.dtype)
    kt_ref[0] = kf.astype(kt_ref.dtype)
    v_ref[0] = v.astype(v_ref.dtype)
    pad = N_A - chunk
    for c in range(tm // chunk):
        rows = slice(c * chunk, (c + 1) * chunk)
        if pad:
            zeros = jnp.zeros((pad, ca), F32)
            stacked = jnp.concatenate([bh[rows], zeros, kf[rows], zeros], axis=0)
        else:
            stacked = jnp.concatenate([bh[rows], kf[rows]], axis=0)
        bkt_ref[0, c] = stacked.T.astype(bkt_ref.dtype)
        gct_ref[0, c] = jnp.broadcast_to(gam_c[c * chunk:c * chunk + 1], (LANE, ca)).T


def _rwkv_prep(pa_all, row0, b, t, shift_p, P, l, dims, chunk):
    paw = pa_all.shape[1]
    pa = pa_all.reshape((1,) + pa_all.shape)
    ca = dims["C_A"]
    tm = chunk * max(1, min(128, t) // chunk)
    assert t % tm == 0 and row0 % tm == 0 and 2 * N_A == LANE
    hb = tm // SUBLANE
    nct = tm // chunk
    row = pl.BlockSpec((1, tm, ca), lambda i, j: (i, j, 0))
    cmaj = pl.BlockSpec((1, nct, ca, LANE), lambda i, j: (i, j, 0, 0))
    vec = lambda w: pl.BlockSpec((1, w), lambda i, j: (0, 0))
    mat = lambda r: pl.BlockSpec((r, ca), lambda i, j: (0, 0))
    tmaj = lambda dt: jax.ShapeDtypeStruct((b, t, ca), dt)
    return pl.pallas_call(
        functools.partial(_rwkv_prep_kernel, ca=ca, chunk=chunk, tm=tm),
        grid=(b, t // tm),
        in_specs=[pl.BlockSpec((1, tm, paw), lambda i, j: (0, row0 // tm + i * (t // tm) + j, 0)),
                  pl.BlockSpec((1, SUBLANE, paw),
                               lambda i, j: (0, jnp.maximum((row0 + i * t) // SUBLANE + j * hb - 1, 0), 0)),
                  pl.BlockSpec((1, 1, paw), lambda i, j: (i, 0, 0)),
                  vec(paw), vec(ca), mat(LANE), vec(ca), mat(LANE), mat(P["g2"].shape[1]),
                  vec(ca), vec(ca), vec(ca)],
        out_specs=[row] * 5 + [cmaj, cmaj, row, row],
        out_shape=[tmaj(BF16)] * 5 + [jax.ShapeDtypeStruct((b, t // chunk, ca, LANE), BF16),
                                      jax.ShapeDtypeStruct((b, t // chunk, ca, LANE), F32), tmaj(F32), tmaj(F32)],
        compiler_params=_cparams(2),
        name="rwkv_prep",
    )(pa, pa, shift_p.reshape(b, 1, paw), P["mu"][l], P["w0"][l], P["w2"][l], P["a0"][l], P["a2"][l],
      P["g2"][l], P["k_k"][l], P["k_a"][l], P["r_k"][l])


def _rwkv_scan_kernel(at_ref, rt_ref, v_ref, bt_ref, kt_ref, bkt_ref, gct_ref, s0_ref, y_ref, s_ref, *,
                      n_pairs, chunk, unroll):
    c = pl.program_id(1)

    @pl.when(c == 0)
    def _():
        s_ref[...] = s0_ref[...]

    rr = lax.broadcasted_iota(jnp.int32, (chunk, chunk), 0)
    cc = lax.broadcasted_iota(jnp.int32, (chunk, chunk), 1)
    lower_strict = cc < rr
    lower_incl = cc <= rr
    eye = jnp.where(rr == cc, 1.0, 0.0)
    first = lax.broadcasted_iota(jnp.int32, (chunk, LANE), 1) < N_A
    br = lax.broadcasted_iota(jnp.int32, (LANE, LANE), 0)
    bc = lax.broadcasted_iota(jnp.int32, (LANE, LANE), 1)
    block_diag = (br // N_A) == (bc // N_A)
    n_double = max(int(math.log2(chunk)) - 1, 0)
    pad = N_A - chunk

    def load(p):
        sl = pl.ds(pl.multiple_of(p * LANE, LANE), LANE)
        return (sl,
                at_ref[0, :, sl],
                rt_ref[0, :, sl],
                v_ref[0, :, sl],
                bt_ref[0, :, sl],
                kt_ref[0, :, sl],
                bkt_ref[0, 0, sl, :],
                s_ref[0, sl, :] * gct_ref[0, 0, sl, :])

    def group(g, carry):
        loaded = [load(g * unroll + j) for j in range(unroll)]
        sls, a, r, v, b, k, bk_t, sb = (list(col) for col in zip(*loaded))
        pairs = range(unroll)
        heads = [(i, hh) for i in pairs for hh in range(2)]
        own = lambda hh: first if hh == 0 else jnp.logical_not(first)
        sb16 = [sb[i].astype(BF16) for i in pairs]
        x = [jnp.dot(a[i], sb16[i], preferred_element_type=F32) for i in pairs]
        y = [jnp.dot(r[i], sb16[i], preferred_element_type=F32) for i in pairs]
        a_h = [jnp.where(own(hh), a[i], jnp.zeros_like(a[i])) for i, hh in heads]
        r_h = [jnp.where(own(hh), r[i], jnp.zeros_like(r[i])) for i, hh in heads]
        n_ab = [jnp.where(lower_strict, _dot_nt(a_h[j], b[i]), 0.0) for j, (i, _) in enumerate(heads)]
        n_ak = [jnp.where(lower_strict, _dot_nt(a_h[j], k[i]), 0.0) for j, (i, _) in enumerate(heads)]
        m_rb = [jnp.where(lower_incl, _dot_nt(r_h[j], b[i]), 0.0) for j, (i, _) in enumerate(heads)]
        m_rk = [jnp.where(lower_incl, _dot_nt(r_h[j], k[i]), 0.0) for j, (i, _) in enumerate(heads)]
        inv = [eye + n for n in n_ab]
        npow = n_ab
        for _ in range(n_double):
            npow = [_dot(n, n) for n in npow]
            inv = [iv + _dot(iv, n) for iv, n in zip(inv, npow)]
        w = [x[i] + _dot(n_ak[j], v[i]) for j, (i, _) in enumerate(heads)]
        u = [_dot(inv[j], w[j]) for j in range(len(heads))]
        y_h = [_dot(m_rb[j], u[j]) + _dot(m_rk[j], v[i]) for j, (i, _) in enumerate(heads)]
        for i in pairs:
            u_p = jnp.where(first, u[2 * i], u[2 * i + 1])
            y_new = y[i] + jnp.where(first, y_h[2 * i], y_h[2 * i + 1])
            v32 = v[i].astype(F32)
            if pad:
                zeros = jnp.zeros((pad, LANE), F32)
                stacked = jnp.concatenate([u_p, zeros, v32, zeros], axis=0)
            else:
                stacked = jnp.concatenate([u_p, v32], axis=0)
            s_new = sb[i] + jnp.where(block_diag, _dot(bk_t[i], stacked), 0.0)
            y_ref[0, :, sls[i]] = y_new
            s_ref[0, sls[i], :] = s_new
        return carry

    lax.fori_loop(0, n_pairs // unroll, group, 0)


def _rwkv_scan(rt, at, bt, kt, v, bkt, gct, s0, chunk):
    b, t, ca = rt.shape
    h = ca // N_A
    nc = t // chunk
    eye2 = jnp.eye(2, dtype=F32)
    s0t = jnp.swapaxes(s0, -1, -2).reshape(b, h // 2, 2, N_A, 1, N_A)
    sb0 = (s0t * eye2[None, None, :, None, :, None]).reshape(b, ca, LANE)
    tmaj = pl.BlockSpec((1, chunk, ca), lambda i, j: (i, j, 0))
    cmaj = pl.BlockSpec((1, 1, ca, LANE), lambda i, j: (i, j, 0, 0))
    st = pl.BlockSpec((1, ca, LANE), lambda i, j: (i, 0, 0))
    y, sb = pl.pallas_call(
        functools.partial(_rwkv_scan_kernel, n_pairs=h // 2, chunk=chunk, unroll=_tile(h // 2, 16, 1)),
        grid=(b, nc),
        in_specs=[tmaj, tmaj, tmaj, tmaj, tmaj, cmaj, cmaj, st],
        out_specs=[tmaj, st],
        out_shape=[jax.ShapeDtypeStruct((b, t, ca), F32), jax.ShapeDtypeStruct((b, ca, LANE), F32)],
        compiler_params=_cparams(2),
        name="rwkv_scan",
    )(at, rt, v, bt, kt, bkt, gct, sb0)
    sb = sb.reshape(b, h // 2, 2, N_A, 2, N_A)
    s_t = jnp.stack([sb[:, :, 0, :, 0, :], sb[:, :, 1, :, 1, :]], axis=2).reshape(b, h, N_A, N_A)
    return y, jnp.swapaxes(s_t, -1, -2)


def _rwkv_post_kernel(y_ref, bonus_ref, g_ref, lnw_ref, lnb_ref, o_ref):
    jmat = _head_ones()
    y = y_ref[...]
    mean = _head_sum(y, jmat) * (1.0 / N_A)
    d = y - mean
    var = _head_sum(d * d, jmat) * (1.0 / N_A)
    out = (d * lax.rsqrt(var + GN_EPS) * lnw_ref[...] + lnb_ref[...] + bonus_ref[...]) * g_ref[...]
    o_ref[...] = out.astype(o_ref.dtype)


def _rwkv_post(y, bonus, g, ln_w, ln_b):
    m, ca = y.shape
    tm = _tile(m, 256, SUBLANE)
    row = pl.BlockSpec((tm, ca), lambda i: (i, 0))
    vec = pl.BlockSpec((1, ca), lambda i: (0, 0))
    return pl.pallas_call(
        _rwkv_post_kernel,
        grid=(m // tm,),
        in_specs=[row, row, row, vec, vec],
        out_specs=row,
        out_shape=jax.ShapeDtypeStruct((m, ca), BF16),
        compiler_params=_cparams(1),
        name="rwkv_post",
    )(y, bonus, g, ln_w, ln_b)


def _pool_kernel(z_ref, hist_ref, w_ref, scale_ref, o_ref, x_scr, *, t, tm, group, pos0):
    halo = 2 * SUBLANE
    x_scr[0:halo, :] = hist_ref[0]
    x_scr[halo:halo + t, :] = z_ref[0]
    for i in range(t // tm):
        r0 = i * tm
        pos = pos0 + r0 + lax.broadcasted_iota(jnp.int32, (tm, 1), 0)
        for gi, win in enumerate(POOL_WINDOWS):
            lo, hi = gi * group, (gi + 1) * group
            cur = x_scr[halo + r0:halo + r0 + tm, lo:hi]
            tot = cur
            for back in range(1, win):
                tot = tot + x_scr[halo + r0 - back:halo + r0 - back + tm, lo:hi]
            cnt = jnp.minimum(pos + 1, win).astype(F32)
            d = tot / cnt - cur
            y = _dot(d, w_ref[gi]) * scale_ref[:, lo:hi]
            o_ref[0, r0:r0 + tm, lo:hi] = y.astype(o_ref.dtype)


def _pool(mid_all, row0, b, t, hist16, pool_w, pool_scale, pos0, dims):
    assert row0 % t == 0
    mid = mid_all.reshape((1,) + mid_all.shape)
    cb = dims["C_B"]
    group = pool_w.shape[1]
    assert len(POOL_WINDOWS) * group == cb
    tm = _tile(t, 256, SUBLANE)
    return pl.pallas_call(
        functools.partial(_pool_kernel, t=t, tm=tm, group=group, pos0=pos0),
        grid=(b,),
        in_specs=[pl.BlockSpec((1, t, cb), lambda i: (0, row0 // t + i, 0)),
                  pl.BlockSpec((1, 2 * SUBLANE, cb), lambda i: (i, 0, 0)),
                  pl.BlockSpec(pool_w.shape, lambda i: (0, 0, 0)),
                  pl.BlockSpec((1, cb), lambda i: (0, 0))],
        out_specs=pl.BlockSpec((1, t, cb), lambda i: (i, 0, 0)),
        out_shape=jax.ShapeDtypeStruct((b, t, cb), BF16),
        scratch_shapes=[pltpu.VMEM((t + 2 * SUBLANE, cb), F32)],
        compiler_params=_cparams(1),
        name="pool",
    )(mid, hist16, pool_w, pool_scale.reshape(1, cb))


def _monotone_key(x):
    x = jnp.where(x == 0.0, 0.0, x)
    bits = lax.bitcast_convert_type(x, jnp.int32)
    return jnp.where(bits < 0, bits ^ jnp.int32(0x7FFFFFFF), bits)


def _kth_largest_key(key, k):
    def body(i, tau):
        cand = tau + jnp.left_shift(jnp.int32(1), jnp.int32(31) - i)
        cnt = jnp.sum(jnp.where(key >= cand, 1.0, 0.0), axis=-1, keepdims=True)
        return jnp.where(cnt >= k, cand, tau)

    tau0 = jnp.full((key.shape[0], 1), -2 ** 31, jnp.int32)
    return lax.fori_loop(0, 32, body, tau0)


def _prefix_count(ind):
    r = lax.broadcasted_iota(jnp.int32, (LANE, LANE), 0)
    c = lax.broadcasted_iota(jnp.int32, (LANE, LANE), 1)
    tri = jnp.where(r <= c, 1.0, 0.0).astype(BF16)
    run = jnp.zeros((ind.shape[0], 1), F32)
    outs = []
    for j in range(ind.shape[1] // LANE):
        pj = jnp.dot(ind[:, j * LANE:(j + 1) * LANE].astype(BF16), tri, preferred_element_type=F32)
        outs.append(pj + run)
        run = run + pj[:, LANE - 1:LANE]
    return outs[0] if len(outs) == 1 else jnp.concatenate(outs, axis=1)


def _topk_mask(scores, k):
    key = _monotone_key(scores)
    tau = _kth_largest_key(key, float(k))
    gt = key > tau
    eq = key == tau
    need = float(k) - jnp.sum(jnp.where(gt, 1.0, 0.0), axis=-1, keepdims=True)
    prefix = _prefix_count(jnp.where(eq, 1.0, 0.0))
    return gt | (eq & (prefix <= need))


def _dsa_prompt_kernel(q_ref, k_ref, v_ref, qi_ref, ki_ref, wi_ref, o_ref, key_s, sc_s, m_s, l_s, acc_s, *,
                       tq, t, kc, topk, h_idx, d_idx, n_heads, dh):
    q0 = pl.program_id(1) * tq
    n_kc = t // kc
    needed = (q0 + tq + kc - 1) // kc
    qpos = q0 + lax.broadcasted_iota(jnp.int32, (tq, 1), 0)
    qi = [qi_ref[0, :, h * d_idx:(h + 1) * d_idx].astype(BF16) for h in range(h_idx)]
    wi = [wi_ref[0, :, h:h + 1] for h in range(h_idx)]

    def score_chunk(c, carry):
        keys = ki_ref[0, pl.ds(pl.multiple_of(c * kc, kc), kc), :].astype(BF16)
        acc = jnp.zeros((tq, kc), F32)
        for h in range(h_idx):
            acc = acc + jnp.maximum(_dot_nt(qi[h], keys), 0.0) * wi[h]
        spos = c * kc + lax.broadcasted_iota(jnp.int32, (1, kc), 1)
        key_s[c] = _monotone_key(jnp.where(spos <= qpos, acc, -jnp.inf))
        return carry

    lax.fori_loop(0, needed, score_chunk, 0)

    def count(pred):
        def body(c, tot):
            ind = jnp.where(pred(key_s[c]), 1.0, 0.0)
            for j in range(kc // LANE):
                tot = tot + ind[:, j * LANE:(j + 1) * LANE]
            return tot
        lanes = lax.fori_loop(0, needed, body, jnp.zeros((tq, LANE), F32))
        return jnp.sum(lanes, axis=-1, keepdims=True)

    def search(i, tau):
        cand = tau + jnp.left_shift(jnp.int32(1), jnp.int32(31) - i)
        return jnp.where(count(lambda key: key >= cand) >= float(topk), cand, tau)

    tau = lax.fori_loop(0, 32, search, jnp.full((tq, 1), -2 ** 31, jnp.int32))
    need = float(topk) - count(lambda key: key > tau)

    def mask_chunk(c, run):
        key = key_s[c]
        eq = jnp.where(key == tau, 1.0, 0.0)
        prefix = _prefix_count(eq) + run
        spos = c * kc + lax.broadcasted_iota(jnp.int32, (1, kc), 1)
        sel = (key > tau) | ((key == tau) & (prefix <= need))
        sc_s[c] = jnp.where(sel & (spos <= qpos), 1.0, 0.0)
        return prefix[:, kc - 1:kc]

    lax.fori_loop(0, needed, mask_chunk, jnp.zeros((tq, 1), F32))

    m_s[...] = jnp.full(m_s.shape, NEG_BIG, F32)
    l_s[...] = jnp.zeros(l_s.shape, F32)
    acc_s[...] = jnp.zeros(acc_s.shape, F32)
    scale = dh ** -0.5

    def attn_chunk(c, carry):
        rows = pl.ds(pl.multiple_of(c * kc, kc), kc)
        ok = sc_s[c] > 0.5
        heads = range(n_heads)
        sls = [slice(h * dh, (h + 1) * dh) for h in heads]
        lg = [_dot_nt(q_ref[0, :, sls[h]], k_ref[0, rows, sls[h]]) for h in heads]
        lg = [jnp.where(ok, x * scale, NEG_BIG) for x in lg]
        m_old = [m_s[h] for h in heads]
        m_new = [jnp.maximum(m_old[h], jnp.max(lg[h], axis=-1, keepdims=True)) for h in heads]
        p = [jnp.where(ok, jnp.exp(lg[h] - m_new[h]), 0.0) for h in heads]
        pv = [_dot(p[h], v_ref[0, rows, sls[h]]) for h in heads]
        for h in heads:
            alpha = jnp.exp(m_old[h] - m_new[h])
            l_s[h] = alpha * l_s[h] + jnp.sum(p[h], axis=-1, keepdims=True)
            acc_s[:, sls[h]] = alpha * acc_s[:, sls[h]] + pv[h]
            m_s[h] = m_new[h]
        return carry

    lax.fori_loop(0, needed, attn_chunk, 0)
    for h in range(n_heads):
        sl = slice(h * dh, (h + 1) * dh)
        o_ref[0, :, sl] = (acc_s[:, sl] / l_s[h]).astype(o_ref.dtype)


def _dsa_prompt(q, k, v, qi, ki, wi, dims):
    b, t, cc = q.shape
    tq = _tile(t, 256, SUBLANE)
    kc = _tile(t, 512, tq)
    topk = min(TOPK_MAX, t // 4)
    assert kc >= topk
    h_idx, d_idx, n_heads = dims["H_IDX"], dims["D_IDX"], dims["H_C"]
    qb = lambda w: pl.BlockSpec((1, tq, w), lambda i, j: (i, j, 0))
    full = lambda w: pl.BlockSpec((1, t, w), lambda i, j: (i, 0, 0))
    return pl.pallas_call(
        functools.partial(_dsa_prompt_kernel, tq=tq, t=t, kc=kc, topk=topk, h_idx=h_idx, d_idx=d_idx,
                          n_heads=n_heads, dh=dims["DH_C"]),
        grid=(b, t // tq),
        in_specs=[qb(cc), full(cc), full(cc), qb(h_idx * d_idx), full(d_idx), qb(h_idx)],
        out_specs=qb(cc),
        out_shape=jax.ShapeDtypeStruct((b, t, cc), BF16),
        scratch_shapes=[pltpu.VMEM((t // kc, tq, kc), jnp.int32), pltpu.VMEM((t // kc, tq, kc), F32),
                        pltpu.VMEM((n_heads, tq, 1), F32),
                        pltpu.VMEM((n_heads, tq, 1), F32), pltpu.VMEM((tq, cc), F32)],
        compiler_params=_cparams(2),
        name="dsa_prompt",
    )(q, k, v, qi, ki, wi)


def _idx_rows_scores(qi2, wi2, keys, tq, h_idx, keys_t=False):
    s = jnp.maximum(_dot(qi2, keys) if keys_t else _dot_nt(qi2, keys), 0.0) * wi2
    acc = s[0:tq]
    for h in range(1, h_idx):
        acc = acc + s[h * tq:(h + 1) * tq]
    return acc


def _dsa_sample_scores_kernel(pt_ref, qi_ref, wi_ref, *refs, tq, h_idx):
    o_ref = refs[-1]
    for g, kc_ref in enumerate(refs[:-1]):
        o_ref[0, :, g * PAGE:(g + 1) * PAGE] = _idx_rows_scores(qi_ref[0], wi_ref[0], kc_ref[0, 0], tq, h_idx,
                                                                keys_t=True)


def _dsa_sample_scores(page_table, qi2, wi2, cache_kidx, l, tq, h_idx):
    b, n_pages = page_table.shape
    d_idx = cache_kidx.shape[-1]
    rows = qi2.shape[1]
    group = _tile(n_pages, 16, 1)
    cache_kidx = jnp.swapaxes(cache_kidx, 2, 3)
    page = lambda g: pl.BlockSpec((1, 1, d_idx, PAGE), lambda i, p, pt: (l, pt[i, p * group + g], 0, 0))
    grid_spec = pltpu.PrefetchScalarGridSpec(
        num_scalar_prefetch=1,
        grid=(b, n_pages // group),
        in_specs=[pl.BlockSpec((1, rows, d_idx), lambda i, p, pt: (i, 0, 0)),
                  pl.BlockSpec((1, rows, 1), lambda i, p, pt: (i, 0, 0))] + [page(g) for g in range(group)],
        out_specs=pl.BlockSpec((1, tq, group * PAGE), lambda i, p, pt: (i, 0, p)),
    )
    return pl.pallas_call(
        functools.partial(_dsa_sample_scores_kernel, tq=tq, h_idx=h_idx),
        grid_spec=grid_spec,
        out_shape=jax.ShapeDtypeStruct((b, tq, n_pages * PAGE), F32),
        compiler_params=_cparams(2),
        name="dsa_sample_scores",
    )(page_table, qi2, wi2, *([cache_kidx] * group))


def _dsa_sample_select_kernel(sc_ref, qi_ref, wi_ref, kn_ref, o_ref, on_ref, *, tq, h_idx, topk, past):
    new = _idx_rows_scores(qi_ref[0], wi_ref[0], kn_ref[0], tq, h_idx)
    qrow = lax.broadcasted_iota(jnp.int32, (tq, PAGE), 0)
    col = lax.broadcasted_iota(jnp.int32, (tq, PAGE), 1)
    new_ok = col <= qrow
    scores = jnp.concatenate([sc_ref[0], jnp.where(new_ok, new, -jnp.inf)], axis=1)
    sel = jnp.where(_topk_mask(scores, topk), 1.0, 0.0)
    o_ref[0] = sel[:, :past]
    on_ref[0] = jnp.where(new_ok, sel[:, past:], 0.0)


def _dsa_sample_select(scores, qi2, wi2, ki_new_pad, tq, h_idx):
    b, _, past = scores.shape
    topk = min(TOPK_MAX, (past + tq) // 4)
    rows, d_idx = qi2.shape[1], qi2.shape[2]
    return pl.pallas_call(
        functools.partial(_dsa_sample_select_kernel, tq=tq, h_idx=h_idx, topk=topk, past=past),
        grid=(b,),
        in_specs=[pl.BlockSpec((1, tq, past), lambda i: (i, 0, 0)),
                  pl.BlockSpec((1, rows, d_idx), lambda i: (i, 0, 0)),
                  pl.BlockSpec((1, rows, 1), lambda i: (i, 0, 0)),
                  pl.BlockSpec((1, PAGE, d_idx), lambda i: (i, 0, 0))],
        out_specs=[pl.BlockSpec((1, tq, past), lambda i: (i, 0, 0)), pl.BlockSpec((1, tq, PAGE), lambda i: (i, 0, 0))],
        out_shape=[jax.ShapeDtypeStruct((b, tq, past), F32), jax.ShapeDtypeStruct((b, tq, PAGE), F32)],
        compiler_params=_cparams(1),
        name="dsa_sample_select",
    )(scores, qi2, wi2, ki_new_pad)


def _dsa_sample_attn_kernel(pt_ref, q_ref, kn_ref, vn_ref, m_ref, mn_ref, *refs, n_steps, group, n_heads, dh, tq):
    kc_refs, vc_refs = refs[:group], refs[group:2 * group]
    o_ref, q2_s, exp_s, hm_s, m_s, l_s, acc_s = refs[2 * group:]
    p = pl.program_id(1)
    rows = n_heads * tq
    cols = PAGE * n_heads

    @pl.when(p == 0)
    def _():
        for h in range(n_heads):
            q2_s[h * tq:(h + 1) * tq, :] = q_ref[0, :, h * dh:(h + 1) * dh]
        pos = lax.broadcasted_iota(jnp.int32, (PAGE, cols), 0)
        col = lax.broadcasted_iota(jnp.int32, (PAGE, cols), 1)
        exp_s[...] = jnp.where(col // n_heads == pos, 1.0, 0.0).astype(exp_s.dtype)
        rr = lax.broadcasted_iota(jnp.int32, (rows, cols), 0)
        cc = lax.broadcasted_iota(jnp.int32, (rows, cols), 1)
        hm_s[...] = jnp.where((cc % n_heads) == (rr // tq), 1.0, 0.0)
        m_s[...] = jnp.full(m_s.shape, NEG_BIG, F32)
        l_s[...] = jnp.zeros(l_s.shape, F32)
        acc_s[...] = jnp.zeros(acc_s.shape, F32)

    def update(kmats, vmats, sels):
        q2, hm, exp_m = q2_s[...], hm_s[...], exp_s[...]
        lgs = [_dot_nt(q2, km) * (dh ** -0.5) for km in kmats]
        sel8 = [jnp.dot(s.astype(BF16), exp_m, preferred_element_type=F32) for s in sels]
        msks = [jnp.concatenate([s8] * n_heads, axis=0) * hm > 0.5 for s8 in sel8]
        lgm = [jnp.where(mk, lg, NEG_BIG) for mk, lg in zip(msks, lgs)]
        m_new = m_s[...]
        for x in lgm:
            m_new = jnp.maximum(m_new, jnp.max(x, axis=-1, keepdims=True))
        alpha = jnp.exp(m_s[...] - m_new)
        pms = [jnp.where(mk, jnp.exp(x - m_new), 0.0) for mk, x in zip(msks, lgm)]
        pvs = [_dot(pm, vm) for pm, vm in zip(pms, vmats)]
        l_new = alpha * l_s[...]
        acc_new = alpha * acc_s[...]
        for pm, pv in zip(pms, pvs):
            l_new = l_new + jnp.sum(pm, axis=-1, keepdims=True)
            acc_new = acc_new + pv
        l_s[...] = l_new
        acc_s[...] = acc_new
        m_s[...] = m_new

    @pl.when(p < n_steps)
    def _():
        update([kc_refs[g][0, 0].reshape(cols, dh) for g in range(group)],
               [vc_refs[g][0, 0].reshape(cols, dh) for g in range(group)],
               [m_ref[0, :, g * PAGE:(g + 1) * PAGE] for g in range(group)])

    @pl.when(p == n_steps)
    def _():
        update([kn_ref[0].reshape(cols, dh)], [vn_ref[0].reshape(cols, dh)], [mn_ref[0]])
        res = acc_s[...] / l_s[...]
        for h in range(n_heads):
            o_ref[0, :, h * dh:(h + 1) * dh] = res[h * tq:(h + 1) * tq, :].astype(o_ref.dtype)


def _dsa_sample_attn(page_table, q, cache_k, cache_v, k_new_pad, v_new_pad, mask, mask_new, l, dims):
    b, n_pages = page_table.shape
    tq = q.shape[1]
    n_heads, dh = dims["H_C"], dims["DH_C"]
    cc = n_heads * dh
    group = _tile(n_pages, 8, 1)
    n_steps = n_pages // group
    page = lambda g: pl.BlockSpec(
        (1, 1, PAGE, n_heads, dh),
        lambda i, p, pt: (l, pt[i, jnp.minimum(p * group + g, n_pages - 1)], 0, 0, 0))
    new = pl.BlockSpec((1, PAGE, n_heads, dh), lambda i, p, pt: (i, 0, 0, 0))
    grid_spec = pltpu.PrefetchScalarGridSpec(
        num_scalar_prefetch=1,
        grid=(b, n_steps + 1),
        in_specs=[pl.BlockSpec((1, tq, cc), lambda i, p, pt: (i, 0, 0)), new, new,
                  pl.BlockSpec((1, tq, group * PAGE), lambda i, p, pt: (i, 0, jnp.minimum(p, n_steps - 1))),
                  pl.BlockSpec((1, tq, PAGE), lambda i, p, pt: (i, 0, 0))]
                 + [page(g) for g in range(group)] * 2,
        out_specs=pl.BlockSpec((1, tq, cc), lambda i, p, pt: (i, 0, 0)),
        scratch_shapes=[pltpu.VMEM((n_heads * tq, dh), BF16), pltpu.VMEM((PAGE, PAGE * n_heads), BF16),
                        pltpu.VMEM((n_heads * tq, PAGE * n_heads), F32), pltpu.VMEM((n_heads * tq, 1), F32),
                        pltpu.VMEM((n_heads * tq, 1), F32), pltpu.VMEM((n_heads * tq, dh), F32)],
    )
    return pl.pallas_call(
        functools.partial(_dsa_sample_attn_kernel, n_steps=n_steps, group=group, n_heads=n_heads, dh=dh, tq=tq),
        grid_spec=grid_spec,
        out_shape=jax.ShapeDtypeStruct((b, tq, cc), BF16),
        compiler_params=_cparams(2),
        name="dsa_sample_attn",
    )(page_table, q, k_new_pad, v_new_pad, mask, mask_new, *([cache_k] * group), *([cache_v] * group))


def _dsa_sample(q, k_f32, v_f32, qi, ki, wi, cache_k, cache_v, cache_kidx, page_table, l, dims):
    b, tq, cc = q.shape
    h_idx, d_idx, n_heads, dh = dims["H_IDX"], dims["D_IDX"], dims["H_C"], dims["DH_C"]
    qi2 = jnp.transpose(qi.reshape(b, tq, h_idx, d_idx), (0, 2, 1, 3)).reshape(b, h_idx * tq, d_idx)
    wi2 = jnp.transpose(wi, (0, 2, 1)).reshape(b, h_idx * tq, 1)
    pad_rows = lambda x: jnp.pad(x, ((0, 0), (0, PAGE - tq)) + ((0, 0),) * (x.ndim - 2))
    scores = _dsa_sample_scores(page_table, qi2, wi2, cache_kidx, l, tq, h_idx)
    mask, mask_new = _dsa_sample_select(scores, qi2, wi2, pad_rows(ki), tq, h_idx)
    return _dsa_sample_attn(page_table, q, cache_k, cache_v, pad_rows(k_f32), pad_rows(v_f32), mask, mask_new, l,
                            dims)


def _prepare_weights(raw, dims):
    ca, cb, cc, qiw, d_idx, h_idx, d = (dims[k] for k in ("C_A", "C_B", "C_C", "QIW", "D_IDX", "H_IDX", "D"))
    dd, da, dg = dims["D_DECAY"], dims["D_AAA"], dims["D_GATE"]
    assert dd <= LANE and da <= LANE and dg % LANE == 0
    a_cols = 3 * ca + dd + da + dg
    o = [0, a_cols, a_cols + cb, a_cols + cb + cc, a_cols + cb + 2 * cc, a_cols + cb + 3 * cc]
    o += [o[-1] + qiw, o[-1] + qiw + d_idx, o[-1] + qiw + d_idx + h_idx]
    w_in = raw["w_in"]

    def pack_pa(x):
        zeros = lambda n: jnp.zeros(x.shape[:-1] + (n,), x.dtype)
        return jnp.concatenate([x[..., :3 * ca], x[..., 3 * ca:3 * ca + dd], zeros(LANE - dd),
                                x[..., 3 * ca + dd:3 * ca + dd + da], zeros(LANE - da),
                                x[..., 3 * ca + dd + da:a_cols]], axis=-1)

    assert d_idx + h_idx <= LANE and cb % LANE == 0 and cc % LANE == 0 and qiw % LANE == 0 and d % LANE == 0
    w_in_t = jnp.swapaxes(w_in, 1, 2)
    zero_rows = lambda n: jnp.zeros((w_in.shape[0], n, w_in.shape[1]), w_in.dtype)
    w_pa = jnp.concatenate([w_in_t[:, :3 * ca + dd], zero_rows(LANE - dd), w_in_t[:, 3 * ca + dd:3 * ca + dd + da],
                            zero_rows(LANE - da), w_in_t[:, 3 * ca + dd + da:a_cols]], axis=1)
    dims["MID_ROWS"] = (o[1], cb + 3 * cc + qiw + LANE)
    dims["GATE_ROWS"] = (o[8], 3 * d)
    assert o[1] + dims["MID_ROWS"][1] <= w_in.shape[2]
    pad_rows = lambda x: jnp.pad(x, ((0, 0), (0, LANE - x.shape[1]), (0, 0)))
    depth = w_in.shape[0]
    vec = lambda x: x.reshape(depth, 1, -1)
    P = dict(
        w_pa=w_pa, w_in_t=w_in_t,
        mu=vec(pack_pa(raw["rwkv_mu"])), w0=vec(raw["rwkv_w0"]), w2=pad_rows(raw["rwkv_w2"]).astype(BF16),
        a0=vec(raw["rwkv_a0"]), a2=pad_rows(raw["rwkv_a2"]).astype(BF16), g2=raw["rwkv_g2"].astype(BF16),
        k_k=vec(raw["rwkv_k_k"]), k_a=vec(raw["rwkv_k_a"]), r_k=vec(raw["rwkv_r_k"]),
        ln_w=vec(raw["rwkv_ln_w"]), ln_b=vec(raw["rwkv_ln_b"]),
        pool_w=raw["pool_w"].astype(BF16), pool_scale=raw["pool_scale"],
    )
    for name in ("ffn1_w_down", "ffn2_w_down", "w_br_a", "w_br_b", "w_br_c", "w_out"):
        P[name] = raw[name].astype(BF16)
    for name in ("ffn1_w_gate", "ffn1_w_up", "ffn2_w_gate", "ffn2_w_up", "ffn1_norm", "mix_norm", "ffn2_norm"):
        P[name] = raw[name]
    P["pack_pa"] = pack_pa
    return P


def _unpack_pa(x, dims):
    ca, dd, da = dims["C_A"], dims["D_DECAY"], dims["D_AAA"]
    return jnp.concatenate([x[..., :3 * ca], x[..., 3 * ca:3 * ca + dd],
                            x[..., 3 * ca + LANE:3 * ca + LANE + da], x[..., 3 * ca + 2 * LANE:]], axis=-1)


def _branches(pa, mid, grp, P, l, dims):
    row0, b, t, pos0 = grp["row0"], grp["b"], grp["t"], grp["pos0"]
    m = b * t
    ca, cb, cc = dims["C_A"], dims["C_B"], dims["C_C"]

    def last_rows(x, n, width):
        idx = (row0 + t - n + jnp.arange(b)[:, None] * t + jnp.arange(n)[None, :]).reshape(-1)
        return jnp.take(x, idx, axis=0)[:, :width].reshape(b, n, width)

    chunk = min(N_A, t)
    rt, at, bt, kt, v_a, bkt, gct, bonus, g = _rwkv_prep(pa, row0, b, t, P["pack_pa"](grp["shift"]), P, l, dims,
                                                          chunk)
    y_a, new_wkv = _rwkv_scan(rt, at, bt, kt, v_a, bkt, gct, grp["wkv"], chunk)
    ya = _rwkv_post(y_a.reshape(m, ca), bonus.reshape(m, ca), g.reshape(m, ca), P["ln_w"][l], P["ln_b"][l])
    new_shift = _unpack_pa(last_rows(pa, 1, pa.shape[1])[:, 0], dims)

    hist16 = jnp.pad(grp["pool"], ((0, 0), (2 * SUBLANE - POOL_HIST, 0), (0, 0)))
    yb = _pool(mid, row0, b, t, hist16, P["pool_w"][l], P["pool_scale"][l], pos0, dims)
    new_pool = jnp.concatenate([grp["pool"], last_rows(mid, min(t, POOL_HIST), cb)], axis=1)[:, -POOL_HIST:]

    q, k, v, k_bf, v_bf, qi, kiwi = _rope_all(mid, row0, b, t, pos0, dims)
    ki = kiwi[..., :dims["D_IDX"]]
    wi = kiwi[..., dims["D_IDX"]:dims["D_IDX"] + dims["H_IDX"]]
    if grp["cache"] is None:
        yc = _dsa_prompt(q, k_bf, v_bf, qi, ki, wi, dims)
    else:
        cache_k, cache_v, cache_kidx, page_table = grp["cache"]
        yc = _dsa_sample(q, k, v, qi, ki, wi, cache_k, cache_v, cache_kidx, page_table, l, dims)
    return (ya, yb.reshape(m, cb), yc.reshape(m, cc)), (k, v, ki, new_wkv, new_shift, new_pool)


def _layer(carry, groups, P, l, next_g, dims):
    h, hb, ssq = carry
    h, u, ssq = _ffn(h, hb, ssq, P["ffn1_w_gate"], P["ffn1_w_up"], P["ffn1_w_down"], l, P["mix_norm"][l])
    pa = _mm(u, P["w_pa"], l, out_dtype=F32, ssq=ssq, w_t=True, name="proj_pa")
    mid = _mm(u, P["w_in_t"], l, out_dtype=F32, ssq=ssq, tm_target=1024, w_t=True, w_rows=dims["MID_ROWS"],
              name="proj_mid")
    gates = _mm(u, P["w_in_t"], l, out_dtype=BF16, ssq=ssq, act="sigmoid", tn_target=512, w_t=True,
                w_rows=dims["GATE_ROWS"], name="proj_gate")
    outs = [_branches(pa, mid, grp, P, l, dims) for grp in groups]
    ya, yb, yc = (jnp.concatenate([o[0][i] for o in outs], axis=0) for i in range(3))
    merged = _merge(ya, yb, yc, P["w_br_a"], P["w_br_b"], P["w_br_c"], l, gates)
    h, hb, ssq = _mm(merged, P["w_out"], l, out_dtype=F32, res=h, next_g=P["ffn2_norm"][l], tn_target=512,
                     name="w_out")
    carry = _ffn(h, hb, ssq, P["ffn2_w_gate"], P["ffn2_w_up"], P["ffn2_w_down"], l, next_g)
    return carry, [o[1] for o in outs]


def kernel(x_prompt, x_sample, cache_k, cache_v, cache_kidx, state_wkv, state_shift, state_pool, page_table,
           ffn1_norm, ffn1_w_gate, ffn1_w_up, ffn1_w_down, mix_norm, w_in, rwkv_mu, rwkv_w0, rwkv_w2,
           rwkv_a0, rwkv_a2, rwkv_g2, rwkv_k_k, rwkv_k_a, rwkv_r_k, rwkv_ln_w, rwkv_ln_b, w_br_a, pool_w,
           pool_scale, w_br_b, w_br_c, w_out, ffn2_norm, ffn2_w_gate, ffn2_w_up, ffn2_w_down, final_norm):
    raw = dict(ffn1_norm=ffn1_norm, ffn1_w_gate=ffn1_w_gate, ffn1_w_up=ffn1_w_up, ffn1_w_down=ffn1_w_down,
               mix_norm=mix_norm, w_in=w_in, rwkv_mu=rwkv_mu, rwkv_w0=rwkv_w0, rwkv_w2=rwkv_w2, rwkv_a0=rwkv_a0,
               rwkv_a2=rwkv_a2, rwkv_g2=rwkv_g2, rwkv_k_k=rwkv_k_k, rwkv_k_a=rwkv_k_a, rwkv_r_k=rwkv_r_k,
               rwkv_ln_w=rwkv_ln_w, rwkv_ln_b=rwkv_ln_b, w_br_a=w_br_a, pool_w=pool_w, pool_scale=pool_scale,
               w_br_b=w_br_b, w_br_c=w_br_c, w_out=w_out, ffn2_norm=ffn2_norm, ffn2_w_gate=ffn2_w_gate,
               ffn2_w_up=ffn2_w_up, ffn2_w_down=ffn2_w_down)
    depth, d = mix_norm.shape
    ca = rwkv_w0.shape[-1]
    cb = pool_scale.shape[-1]
    n_heads, dh = cache_k.shape[3], cache_k.shape[4]
    cc = n_heads * dh
    d_idx = cache_kidx.shape[-1]
    dd, da, dg = rwkv_w2.shape[1], rwkv_a2.shape[1], rwkv_g2.shape[1]
    a_cols = 3 * ca + dd + da + dg
    h_idx = (w_in.shape[-1] - a_cols - cb - 3 * cc - d_idx - 3 * d) // (d_idx + 1)
    dims = dict(D=d, C_A=ca, C_B=cb, C_C=cc, H_C=n_heads, DH_C=dh, D_IDX=d_idx, H_IDX=h_idx,
                QIW=h_idx * d_idx, D_DECAY=dd, D_AAA=da, D_GATE=dg)
    assert w_in.shape[-1] == a_cols + cb + 3 * cc + h_idx * d_idx + d_idx + h_idx + 3 * d
    P = _prepare_weights(raw, dims)

    bp = x_prompt.shape[0]
    past = page_table.shape[1] * PAGE
    bs, ts = x_sample.shape[:2]
    tp = x_prompt.shape[1]
    mp, ms = bp * tp, bs * ts
    x_all = jnp.concatenate([x_prompt.reshape(mp, d), x_sample.reshape(ms, d)], axis=0)
    carry = (x_all,) + tuple(_prenorm(x_all, ffn1_norm[0]))
    st_p, st_s = [], []
    for l in range(depth):
        next_g = ffn1_norm[l + 1] if l + 1 < depth else final_norm
        groups = [
            dict(row0=0, b=bp, t=tp, pos0=0, cache=None, shift=jnp.zeros((bp, a_cols), F32),
                 wkv=jnp.zeros((bp, ca // N_A, N_A, N_A), F32), pool=jnp.zeros((bp, POOL_HIST, cb), F32)),
            dict(row0=mp, b=bs, t=ts, pos0=past, cache=(cache_k, cache_v, cache_kidx, page_table),
                 shift=state_shift[l], wkv=state_wkv[l], pool=state_pool[l]),
        ]
        carry, (sp, ss) = _layer(carry, groups, P, l, next_g, dims)
        st_p.append(sp)
        st_s.append(ss)
    y_p = _rmsnorm(carry[0], 0, mp, final_norm, F32).reshape(x_prompt.shape)
    y_s = _rmsnorm(carry[0], mp, ms, final_norm, F32).reshape(x_sample.shape)
    stk = lambda sts, i: jnp.stack([s[i] for s in sts])
    return (y_p, y_s,
            stk(st_p, 0), stk(st_p, 1), stk(st_p, 2), stk(st_p, 3), stk(st_p, 4), stk(st_p, 5),
            stk(st_s, 0), stk(st_s, 1), stk(st_s, 2), stk(st_s, 3), stk(st_s, 4), stk(st_s, 5))
```

```python
import functools
import math

import jax
import jax.numpy as jnp
from jax import lax
from jax.experimental import pallas as pl
from jax.experimental.pallas import tpu as pltpu

F32 = jnp.float32
BF16 = jnp.bfloat16

LANE = 128
SUBLANE = 8
VMEM_LIMIT_BYTES = 56 * 1024 * 1024
ROW_TILE = 1400

N_A = 64
GN_EPS = 64e-5
POOL_WINDOWS = (2, 4, 8, 16)
POOL_HIST = max(POOL_WINDOWS) - 1
TOPK_MAX = 256
ROPE_THETA = 10000.0
RMS_EPS = 1e-6
PAGE = 128
NEG_BIG = -1e30


def _cparams(n_axes):
    return pltpu.CompilerParams(dimension_semantics=("arbitrary",) * n_axes,
                                vmem_limit_bytes=VMEM_LIMIT_BYTES)


def _tile(n, target, mult):
    best = None
    for t in range(mult, min(n, target) + 1, mult):
        if n % t == 0:
            best = t
    return best if best is not None else n


def _roundup(n, m):
    return -(-n // m) * m


def _dot(a, b):
    return jnp.dot(a.astype(BF16), b.astype(BF16), preferred_element_type=F32)


def _dot_nt(a, b):
    return lax.dot_general(a.astype(BF16), b.astype(BF16), (((1,), (1,)), ((), ())),
                           preferred_element_type=F32)


def _split3(x):
    h1 = x.astype(BF16)
    r1 = x - h1.astype(F32)
    h2 = r1.astype(BF16)
    h3 = (r1 - h2.astype(F32)).astype(BF16)
    return h1, h2, h3


def _dot_exact_rhs(x, m_bf16):
    h1, h2, h3 = _split3(x)
    return (jnp.dot(h1, m_bf16, preferred_element_type=F32)
            + jnp.dot(h2, m_bf16, preferred_element_type=F32)
            + jnp.dot(h3, m_bf16, preferred_element_type=F32))


def _dot_exact_lhs(m_bf16, x):
    h1, h2, h3 = _split3(x)
    return (jnp.dot(m_bf16, h1, preferred_element_type=F32)
            + jnp.dot(m_bf16, h2, preferred_element_type=F32)
            + jnp.dot(m_bf16, h3, preferred_element_type=F32))


def _rmsnorm_kernel(x_ref, g_ref, o_ref):
    x = x_ref[...]
    ms = jnp.mean(x * x, axis=-1, keepdims=True)
    o_ref[...] = (x * lax.rsqrt(ms + RMS_EPS) * g_ref[...]).astype(o_ref.dtype)


def _rmsnorm(x2d, row0, rows, g, out_dtype):
    d = x2d.shape[1]
    tm = _tile(rows, 256, SUBLANE)
    assert row0 % tm == 0
    return pl.pallas_call(
        _rmsnorm_kernel,
        grid=(rows // tm,),
        in_specs=[pl.BlockSpec((tm, d), lambda i: (row0 // tm + i, 0)), pl.BlockSpec((1, d), lambda i: (0, 0))],
        out_specs=pl.BlockSpec((tm, d), lambda i: (i, 0)),
        out_shape=jax.ShapeDtypeStruct((rows, d), out_dtype),
        compiler_params=_cparams(1),
        name="rmsnorm",
    )(x2d, g.reshape(1, d))


def _prenorm_kernel(x_ref, g_ref, xb_ref, ssq_ref):
    x = x_ref[...]
    xb_ref[...] = (x * g_ref[...]).astype(xb_ref.dtype)
    ssq_ref[...] = jnp.sum(x * x, axis=-1, keepdims=True)


def _prenorm(x2d, g):
    m, d = x2d.shape
    tm = _tile(m, 256, SUBLANE)
    return pl.pallas_call(
        _prenorm_kernel,
        grid=(m // tm,),
        in_specs=[pl.BlockSpec((tm, d), lambda i: (i, 0)), pl.BlockSpec((1, d), lambda i: (0, 0))],
        out_specs=[pl.BlockSpec((tm, d), lambda i: (i, 0)), pl.BlockSpec((tm, 1), lambda i: (i, 0))],
        out_shape=[jax.ShapeDtypeStruct((m, d), BF16), jax.ShapeDtypeStruct((m, 1), F32)],
        compiler_params=_cparams(1),
        name="prenorm",
    )(x2d, g.reshape(1, d))


def _row_scale(ssq, width):
    return lax.rsqrt(ssq * (1.0 / width) + RMS_EPS)


def _mm_kernel(*refs, nk, kdim, w_t, has_ssq, has_res, has_next, res_scale, act):
    it = iter(refs)
    x_ref, w_ref = next(it), next(it)
    ssq_ref = next(it) if has_ssq else None
    res_ref = next(it) if has_res else None
    gnext_ref = next(it) if has_next else None
    o_ref = next(it)
    ob_ref, ssqo_ref = (next(it), next(it)) if has_next else (None, None)

    def epilogue(acc):
        if has_ssq:
            acc = acc * _row_scale(ssq_ref[...], kdim)
        if act == "sigmoid":
            acc = jax.nn.sigmoid(acc)
        if has_res:
            acc = res_ref[...] + res_scale * acc
        o_ref[...] = acc.astype(o_ref.dtype)
        if has_next:
            ob_ref[...] = (acc * gnext_ref[...]).astype(ob_ref.dtype)
            part_ssq = jnp.sum(acc * acc, axis=-1, keepdims=True)
            j = pl.program_id(1)

            @pl.when(j == 0)
            def _():
                ssqo_ref[...] = part_ssq

            @pl.when(j > 0)
            def _():
                ssqo_ref[...] += part_ssq

    if w_t:
        part = lax.dot_general(x_ref[...], w_ref[0].astype(BF16), (((1,), (1,)), ((), ())),
                               preferred_element_type=F32)
    else:
        part = jnp.dot(x_ref[...], w_ref[0], preferred_element_type=F32)
    if nk == 1:
        epilogue(part)
    else:
        acc_ref = next(it)
        k = pl.program_id(2)

        @pl.when(k == 0)
        def _():
            acc_ref[...] = part

        @pl.when(k > 0)
        def _():
            acc_ref[...] += part

        @pl.when(k == nk - 1)
        def _():
            epilogue(acc_ref[...])


def _mm(x, w, l, *, out_dtype, ssq=None, res=None, res_scale=1.0, act=None, next_g=None, tm_target=ROW_TILE,
        tn_target=1024, tk_target=4096, w_t=False, w_rows=None, name="mm"):
    m, kdim = x.shape
    n = w_rows[1] if w_rows is not None else (w.shape[1] if w_t else w.shape[2])
    tm = _tile(m, tm_target, 2 * SUBLANE)
    tn = _tile(n, tn_target, LANE)
    tk = _tile(kdim, tk_target, LANE)
    nk = kdim // tk
    tile = pl.BlockSpec((tm, tn), lambda i, j, k: (i, j))
    col = pl.BlockSpec((tm, 1), lambda i, j, k: (i, 0))
    if w_rows is not None:
        assert w_t and w_rows[0] % (2 * SUBLANE) == 0
        wspec = pl.BlockSpec((pl.Element(1), pl.Element(tn), pl.Element(tk)),
                             lambda i, j, k: (l, pl.multiple_of(w_rows[0] + j * tn, 2 * SUBLANE),
                                              pl.multiple_of(k * tk, LANE)))
    elif w_t:
        wspec = pl.BlockSpec((1, tn, tk), lambda i, j, k: (l, j, k))
    else:
        wspec = pl.BlockSpec((1, tk, tn), lambda i, j, k: (l, k, j))
    in_specs = [pl.BlockSpec((tm, tk), lambda i, j, k: (i, k)), wspec]
    args = [x, w]
    if ssq is not None:
        in_specs.append(col)
        args.append(ssq)
    if res is not None:
        in_specs.append(tile)
        args.append(res)
    out_specs, out_shape = tile, jax.ShapeDtypeStruct((m, n), out_dtype)
    if next_g is not None:
        assert res is not None
        in_specs.append(pl.BlockSpec((1, tn), lambda i, j, k: (0, j)))
        args.append(next_g.reshape(1, n))
        out_specs = [tile, tile, col]
        out_shape = [out_shape, jax.ShapeDtypeStruct((m, n), BF16), jax.ShapeDtypeStruct((m, 1), F32)]
    scratch = [pltpu.VMEM((tm, tn), F32)] if nk > 1 else []
    return pl.pallas_call(
        functools.partial(_mm_kernel, nk=nk, kdim=kdim, w_t=w_t, has_ssq=ssq is not None, has_res=res is not None,
                          has_next=next_g is not None, res_scale=res_scale, act=act),
        grid=(m // tm, n // tn, nk),
        in_specs=in_specs,
        out_specs=out_specs,
        out_shape=out_shape,
        scratch_shapes=scratch,
        compiler_params=_cparams(3),
        name=name,
    )(*args)


def _swiglu_up_kernel(x_ref, ssq_ref, wg_ref, wu_ref, o_ref, *, d):
    x = x_ref[...]
    scale = _row_scale(ssq_ref[...], d)
    g = jnp.dot(x, wg_ref[0].astype(BF16), preferred_element_type=F32) * scale
    u = jnp.dot(x, wu_ref[0].astype(BF16), preferred_element_type=F32) * scale
    o_ref[...] = (g * jax.nn.sigmoid(g) * u).astype(o_ref.dtype)


def _swiglu_up(x, ssq, wg, wu, l):
    m, d = x.shape
    f = wg.shape[2]
    tm = _tile(m, ROW_TILE, 2 * SUBLANE)
    tn = _tile(f, 512, LANE)
    return pl.pallas_call(
        functools.partial(_swiglu_up_kernel, d=d),
        grid=(m // tm, f // tn),
        in_specs=[pl.BlockSpec((tm, d), lambda i, j: (i, 0)),
                  pl.BlockSpec((tm, 1), lambda i, j: (i, 0)),
                  pl.BlockSpec((1, d, tn), lambda i, j: (l, 0, j)),
                  pl.BlockSpec((1, d, tn), lambda i, j: (l, 0, j))],
        out_specs=pl.BlockSpec((tm, tn), lambda i, j: (i, j)),
        out_shape=jax.ShapeDtypeStruct((m, f), BF16),
        compiler_params=_cparams(2),
        name="swiglu_up",
    )(x, ssq, wg, wu)


def _ffn(h, hb, ssq, wg, wu, wd, l, next_g):
    a = _swiglu_up(hb, ssq, wg, wu, l)
    return _mm(a, wd, l, out_dtype=F32, res=h, res_scale=0.5, next_g=next_g, tm_target=1024, tn_target=256,
               tk_target=a.shape[1], name="ffn_down")


def _merge_kernel(ya_ref, yb_ref, yc_ref, wa_ref, wb_ref, wc_ref, ga_ref, gb_ref, gc_ref, o_ref):
    a = jnp.dot(ya_ref[...], wa_ref[0], preferred_element_type=F32)
    b = jnp.dot(yb_ref[...], wb_ref[0], preferred_element_type=F32)
    c = jnp.dot(yc_ref[...], wc_ref[0], preferred_element_type=F32)
    out = ga_ref[...].astype(F32) * a + gb_ref[...].astype(F32) * b + gc_ref[...].astype(F32) * c
    o_ref[...] = out.astype(o_ref.dtype)


def _merge(ya, yb, yc, wa, wb, wc, l, gates):
    m = ya.shape[0]
    d = wa.shape[2]
    tm = _tile(m, ROW_TILE, 2 * SUBLANE)
    tn = _tile(d, 512, LANE)
    nb = d // tn
    wspec = lambda w: pl.BlockSpec((1, w.shape[1], tn), lambda i, j: (l, 0, j))
    return pl.pallas_call(
        _merge_kernel,
        grid=(m // tm, nb),
        in_specs=[pl.BlockSpec((tm, ya.shape[1]), lambda i, j: (i, 0)),
                  pl.BlockSpec((tm, yb.shape[1]), lambda i, j: (i, 0)),
                  pl.BlockSpec((tm, yc.shape[1]), lambda i, j: (i, 0)),
                  wspec(wa), wspec(wb), wspec(wc),
                  pl.BlockSpec((tm, tn), lambda i, j: (i, j)),
                  pl.BlockSpec((tm, tn), lambda i, j: (i, nb + j)),
                  pl.BlockSpec((tm, tn), lambda i, j: (i, 2 * nb + j))],
        out_specs=pl.BlockSpec((tm, tn), lambda i, j: (i, j)),
        out_shape=jax.ShapeDtypeStruct((m, d), BF16),
        compiler_params=_cparams(2),
        name="merge",
    )(ya, yb, yc, wa, wb, wc, gates, gates, gates)


def _rope_kernel(q_ref, k_ref, v_ref, qi_ref, kw_ref, c128_ref, s128_ref, c64_ref, s64_ref,
                 qo_ref, ko_ref, vo_ref, kb_ref, vb_ref, qio_ref, kwo_ref, *, n_heads, n_idx_groups, d_idx,
                 idx_scale):
    c128, s128 = c128_ref[...], s128_ref[...]
    c64, s64 = c64_ref[...], s64_ref[...]
    lane = lax.broadcasted_iota(jnp.int32, c64.shape, 1)
    first_half = (lane % d_idx) < (d_idx // 2)

    def rope128(x):
        return x * c128 + pltpu.roll(x, LANE // 2, axis=1) * s128

    def rope64(x):
        rot = jnp.where(first_half, pltpu.roll(x, LANE - d_idx // 2, axis=1), pltpu.roll(x, d_idx // 2, axis=1))
        return x * c64 + rot * s64

    for h in range(n_heads):
        sl = slice(h * LANE, (h + 1) * LANE)
        qo_ref[0, :, sl] = rope128(q_ref[0, :, sl]).astype(qo_ref.dtype)
        kr = rope128(k_ref[0, :, sl])
        ko_ref[0, :, h, :] = kr
        kb_ref[0, :, sl] = kr.astype(kb_ref.dtype)
        vo_ref[0, :, h, :] = v_ref[0, :, sl]
    vb_ref[0] = v_ref[0].astype(vb_ref.dtype)
    for g in range(n_idx_groups):
        sl = slice(g * LANE, (g + 1) * LANE)
        qio_ref[0, :, sl] = rope64(qi_ref[0, :, sl])
    kw = kw_ref[0]
    kwo_ref[0] = jnp.where(lane < d_idx, rope64(kw), kw * idx_scale)


def _rope_tables(pos, d):
    inv = ROPE_THETA ** (-jnp.arange(0, d, 2, dtype=F32) / d)
    ang = pos.astype(F32)[:, None] * inv[None, :]
    cos, sin = jnp.cos(ang), jnp.sin(ang)
    reps = LANE // d
    c = jnp.tile(jnp.concatenate([cos, cos], axis=-1), (1, reps))
    s = jnp.tile(jnp.concatenate([-sin, sin], axis=-1), (1, reps))
    return c, s


def _rope_all(mid_all, row0, b, t, pos0, dims):
    mid = mid_all.reshape((1,) + mid_all.shape)
    cb, cc, qiw, d_idx, h_idx = dims["C_B"], dims["C_C"], dims["QIW"], dims["D_IDX"], dims["H_IDX"]
    assert dims["DH_C"] == LANE and LANE % d_idx == 0
    assert cb % cc == 0 and (cb + 3 * cc) % qiw == 0
    tm = _tile(t, 256, SUBLANE)
    pos = pos0 + jnp.arange(t)
    c128, s128 = _rope_tables(pos, LANE)
    c64, s64 = _rope_tables(pos, d_idx)
    ob = cb // cc
    assert row0 % tm == 0
    row = lambda width, idx: pl.BlockSpec((1, tm, width), lambda i, j: (0, row0 // tm + i * (t // tm) + j, idx))
    tab = pl.BlockSpec((tm, LANE), lambda i, j: (j, 0))
    out = lambda width: pl.BlockSpec((1, tm, width), lambda i, j: (i, j, 0))
    split = pl.BlockSpec((1, tm, cc // LANE, LANE), lambda i, j: (i, j, 0, 0))
    return pl.pallas_call(
        functools.partial(_rope_kernel, n_heads=cc // LANE, n_idx_groups=qiw // LANE, d_idx=d_idx,
                          idx_scale=(h_idx * d_idx) ** -0.5),
        grid=(b, t // tm),
        in_specs=[row(cc, ob), row(cc, ob + 1), row(cc, ob + 2), row(qiw, (cb + 3 * cc) // qiw),
                  row(LANE, (cb + 3 * cc + qiw) // LANE), tab, tab, tab, tab],
        out_specs=[out(cc), split, split, out(cc), out(cc), out(qiw), out(LANE)],
        out_shape=[jax.ShapeDtypeStruct((b, t, cc), BF16), jax.ShapeDtypeStruct((b, t, cc // LANE, LANE), F32),
                   jax.ShapeDtypeStruct((b, t, cc // LANE, LANE), F32),
                   jax.ShapeDtypeStruct((b, t, cc), BF16), jax.ShapeDtypeStruct((b, t, cc), BF16),
                   jax.ShapeDtypeStruct((b, t, qiw), F32), jax.ShapeDtypeStruct((b, t, LANE), F32)],
        compiler_params=_cparams(2),
        name="rope",
    )(mid, mid, mid, mid, mid, c128, s128, c64, s64)


def _head_sum(x, jmat):
    parts = []
    for g in range(x.shape[1] // LANE):
        parts.append(_dot_exact_rhs(x[:, g * LANE:(g + 1) * LANE], jmat))
    return parts[0] if len(parts) == 1 else jnp.concatenate(parts, axis=1)


def _head_ones():
    r = lax.broadcasted_iota(jnp.int32, (LANE, LANE), 0)
    c = lax.broadcasted_iota(jnp.int32, (LANE, LANE), 1)
    return jnp.where((r // N_A) == (c // N_A), 1.0, 0.0).astype(BF16)


def _rwkv_prep_kernel(pa_ref, halo_ref, shift_ref, mu_ref, w0_ref, w2_ref, a0_ref, a2_ref, g2_ref,
                      kk_ref, ka_ref, rk_ref,
                      rt_ref, at_ref, bt_ref, kt_ref, v_ref, bkt_ref, gct_ref, bonus_ref, g_ref, *, ca, chunk, tm):
    j = pl.program_id(1)
    prev_last = jnp.where(j == 0, shift_ref[0], halo_ref[0, SUBLANE - 1:SUBLANE, :])
    row = lax.broadcasted_iota(jnp.int32, (tm, 1), 0)

    def shifted(lo, hi):
        x = pa_ref[0, :, lo:hi]
        prev = jnp.where(row == 0, prev_last[:, lo:hi], pltpu.roll(x, 1, axis=0))
        return x + mu_ref[:, lo:hi] * (prev - x)

    r = shifted(0, ca)
    k = shifted(ca, 2 * ca)
    v = shifted(2 * ca, 3 * ca)
    wl = shifted(3 * ca, 3 * ca + LANE)
    al = shifted(3 * ca + LANE, 3 * ca + 2 * LANE)
    gl = shifted(3 * ca + 2 * LANE, pa_ref.shape[2])

    z = -(w0_ref[...] + _dot(jnp.tanh(wl), w2_ref[...]))
    softplus = jnp.maximum(z, 0.0) + jnp.log(1.0 + jnp.exp(-jnp.abs(z)))
    log_decay = -jnp.exp(-softplus - 0.5)
    rr = lax.broadcasted_iota(jnp.int32, (tm, tm), 0)
    cc = lax.broadcasted_iota(jnp.int32, (tm, tm), 1)
    same_chunk = (rr // chunk) == (cc // chunk)
    tril = jnp.where(same_chunk & (cc <= rr), 1.0, 0.0).astype(BF16)
    ones_blk = jnp.where(same_chunk, 1.0, 0.0).astype(BF16)
    cum = _dot_exact_lhs(tril, log_decay)
    cum_c = _dot_exact_lhs(ones_blk, log_decay)
    fwd = jnp.exp(cum_c - cum)
    back = jnp.exp(cum - cum_c)
    back_prev = jnp.exp(cum - log_decay - cum_c)
    gam_c = jnp.exp(cum_c)

    a = jax.nn.sigmoid(a0_ref[...] + _dot(al, a2_ref[...]))
    jmat = _head_ones()
    kk = k * kk_ref[...]
    kk = kk / jnp.maximum(jnp.sqrt(_head_sum(kk * kk, jmat)), 1e-12)
    kh = k * (1.0 + (a - 1.0) * ka_ref[...])
    bonus_ref[0] = _head_sum(r * kh * rk_ref[...], jmat) * v
    g_ref[0] = _dot(jax.nn.sigmoid(gl), g2_ref[...])
    bh = kk * a * fwd
    kf = kh * fwd
    rt_ref[0] = (r * back).astype(rt_ref.dtype)
    at_ref[0] = (-kk * back_prev).astype(at_ref.dtype)
    bt_ref[0] = bh.astype(bt_ref.dtype)
    kt_ref[0] = kf.astype(kt_ref.dtype)
    v_ref[0] = v.astype(v_ref.dtype)
    pad = N_A - chunk
    for c in range(tm // chunk):
        rows = slice(c * chunk, (c + 1) * chunk)
        if pad:
            zeros = jnp.zeros((pad, ca), F32)
            stacked = jnp.concatenate([bh[rows], zeros, kf[rows], zeros], axis=0)
        else:
            stacked = jnp.concatenate([bh[rows], kf[rows]], axis=0)
        bkt_ref[0, c] = stacked.T.astype(bkt_ref.dtype)
        gct_ref[0, c] = jnp.broadcast_to(gam_c[c * chunk:c * chunk + 1], (LANE, ca)).T


def _rwkv_prep(pa_all, row0, b, t, shift_p, P, l, dims, chunk):
    paw = pa_all.shape[1]
    pa = pa_all.reshape((1,) + pa_all.shape)
    ca = dims["C_A"]
    tm = chunk * max(1, min(128, t) // chunk)
    assert t % tm == 0 and row0 % tm == 0 and 2 * N_A == LANE
    hb = tm // SUBLANE
    nct = tm // chunk
    row = pl.BlockSpec((1, tm, ca), lambda i, j: (i, j, 0))
    cmaj = pl.BlockSpec((1, nct, ca, LANE), lambda i, j: (i, j, 0, 0))
    vec = lambda w: pl.BlockSpec((1, w), lambda i, j: (0, 0))
    mat = lambda r: pl.BlockSpec((r, ca), lambda i, j: (0, 0))
    tmaj = lambda dt: jax.ShapeDtypeStruct((b, t, ca), dt)
    return pl.pallas_call(
        functools.partial(_rwkv_prep_kernel, ca=ca, chunk=chunk, tm=tm),
        grid=(b, t // tm),
        in_specs=[pl.BlockSpec((1, tm, paw), lambda i, j: (0, row0 // tm + i * (t // tm) + j, 0)),
                  pl.BlockSpec((1, SUBLANE, paw),
                               lambda i, j: (0, jnp.maximum((row0 + i * t) // SUBLANE + j * hb - 1, 0), 0)),
                  pl.BlockSpec((1, 1, paw), lambda i, j: (i, 0, 0)),
                  vec(paw), vec(ca), mat(LANE), vec(ca), mat(LANE), mat(P["g2"].shape[1]),
                  vec(ca), vec(ca), vec(ca)],
        out_specs=[row] * 5 + [cmaj, cmaj, row, row],
        out_shape=[tmaj(BF16)] * 5 + [jax.ShapeDtypeStruct((b, t // chunk, ca, LANE), BF16),
                                      jax.ShapeDtypeStruct((b, t // chunk, ca, LANE), F32), tmaj(F32), tmaj(F32)],
        compiler_params=_cparams(2),
        name="rwkv_prep",
    )(pa, pa, shift_p.reshape(b, 1, paw), P["mu"][l], P["w0"][l], P["w2"][l], P["a0"][l], P["a2"][l],
      P["g2"][l], P["k_k"][l], P["k_a"][l], P["r_k"][l])


def _rwkv_scan_kernel(at_ref, rt_ref, v_ref, bt_ref, kt_ref, bkt_ref, gct_ref, s0_ref, y_ref, s_ref, *,
                      n_pairs, chunk, unroll):
    c = pl.program_id(1)

    @pl.when(c == 0)
    def _():
        s_ref[...] = s0_ref[...]

    rr = lax.broadcasted_iota(jnp.int32, (chunk, chunk), 0)
    cc = lax.broadcasted_iota(jnp.int32, (chunk, chunk), 1)
    lower_strict = cc < rr
    lower_incl = cc <= rr
    eye = jnp.where(rr == cc, 1.0, 0.0)
    first = lax.broadcasted_iota(jnp.int32, (chunk, LANE), 1) < N_A
    br = lax.broadcasted_iota(jnp.int32, (LANE, LANE), 0)
    bc = lax.broadcasted_iota(jnp.int32, (LANE, LANE), 1)
    block_diag = (br // N_A) == (bc // N_A)
    n_double = max(int(math.log2(chunk)) - 1, 0)
    pad = N_A - chunk

    def load(p):
        sl = pl.ds(pl.multiple_of(p * LANE, LANE), LANE)
        return (sl,
                at_ref[0, :, sl],
                rt_ref[0, :, sl],
                v_ref[0, :, sl],
                bt_ref[0, :, sl],
                kt_ref[0, :, sl],
                bkt_ref[0, 0, sl, :],
                s_ref[0, sl, :] * gct_ref[0, 0, sl, :])

    def group(g, carry):
        loaded = [load(g * unroll + j) for j in range(unroll)]
        sls, a, r, v, b, k, bk_t, sb = (list(col) for col in zip(*loaded))
        pairs = range(unroll)
        heads = [(i, hh) for i in pairs for hh in range(2)]
        own = lambda hh: first if hh == 0 else jnp.logical_not(first)
        sb16 = [sb[i].astype(BF16) for i in pairs]
        x = [jnp.dot(a[i], sb16[i], preferred_element_type=F32) for i in pairs]
        y = [jnp.dot(r[i], sb16[i], preferred_element_type=F32) for i in pairs]
        a_h = [jnp.where(own(hh), a[i], jnp.zeros_like(a[i])) for i, hh in heads]
        r_h = [jnp.where(own(hh), r[i], jnp.zeros_like(r[i])) for i, hh in heads]
        n_ab = [jnp.where(lower_strict, _dot_nt(a_h[j], b[i]), 0.0) for j, (i, _) in enumerate(heads)]
        n_ak = [jnp.where(lower_strict, _dot_nt(a_h[j], k[i]), 0.0) for j, (i, _) in enumerate(heads)]
        m_rb = [jnp.where(lower_incl, _dot_nt(r_h[j], b[i]), 0.0) for j, (i, _) in enumerate(heads)]
        m_rk = [jnp.where(lower_incl, _dot_nt(r_h[j], k[i]), 0.0) for j, (i, _) in enumerate(heads)]
        inv = [eye + n for n in n_ab]
        npow = n_ab
        for _ in range(n_double):
            npow = [_dot(n, n) for n in npow]
            inv = [iv + _dot(iv, n) for iv, n in zip(inv, npow)]
        w = [x[i] + _dot(n_ak[j], v[i]) for j, (i, _) in enumerate(heads)]
        u = [_dot(inv[j], w[j]) for j in range(len(heads))]
        y_h = [_dot(m_rb[j], u[j]) + _dot(m_rk[j], v[i]) for j, (i, _) in enumerate(heads)]
        for i in pairs:
            u_p = jnp.where(first, u[2 * i], u[2 * i + 1])
            y_new = y[i] + jnp.where(first, y_h[2 * i], y_h[2 * i + 1])
            v32 = v[i].astype(F32)
            if pad:
                zeros = jnp.zeros((pad, LANE), F32)
                stacked = jnp.concatenate([u_p, zeros, v32, zeros], axis=0)
            else:
                stacked = jnp.concatenate([u_p, v32], axis=0)
            s_new = sb[i] + jnp.where(block_diag, _dot(bk_t[i], stacked), 0.0)
            y_ref[0, :, sls[i]] = y_new
            s_ref[0, sls[i], :] = s_new
        return carry

    lax.fori_loop(0, n_pairs // unroll, group, 0)


def _rwkv_scan(rt, at, bt, kt, v, bkt, gct, s0, chunk):
    b, t, ca = rt.shape
    h = ca // N_A
    nc = t // chunk
    eye2 = jnp.eye(2, dtype=F32)
    s0t = jnp.swapaxes(s0, -1, -2).reshape(b, h // 2, 2, N_A, 1, N_A)
    sb0 = (s0t * eye2[None, None, :, None, :, None]).reshape(b, ca, LANE)
    tmaj = pl.BlockSpec((1, chunk, ca), lambda i, j: (i, j, 0))
    cmaj = pl.BlockSpec((1, 1, ca, LANE), lambda i, j: (i, j, 0, 0))
    st = pl.BlockSpec((1, ca, LANE), lambda i, j: (i, 0, 0))
    y, sb = pl.pallas_call(
        functools.partial(_rwkv_scan_kernel, n_pairs=h // 2, chunk=chunk, unroll=_tile(h // 2, 16, 1)),
        grid=(b, nc),
        in_specs=[tmaj, tmaj, tmaj, tmaj, tmaj, cmaj, cmaj, st],
        out_specs=[tmaj, st],
        out_shape=[jax.ShapeDtypeStruct((b, t, ca), F32), jax.ShapeDtypeStruct((b, ca, LANE), F32)],
        compiler_params=_cparams(2),
        name="rwkv_scan",
    )(at, rt, v, bt, kt, bkt, gct, sb0)
    sb = sb.reshape(b, h // 2, 2, N_A, 2, N_A)
    s_t = jnp.stack([sb[:, :, 0, :, 0, :], sb[:, :, 1, :, 1, :]], axis=2).reshape(b, h, N_A, N_A)
    return y, jnp.swapaxes(s_t, -1, -2)


def _rwkv_post_kernel(y_ref, bonus_ref, g_ref, lnw_ref, lnb_ref, o_ref):
    jmat = _head_ones()
    y = y_ref[...]
    mean = _head_sum(y, jmat) * (1.0 / N_A)
    d = y - mean
    var = _head_sum(d * d, jmat) * (1.0 / N_A)
    out = (d * lax.rsqrt(var + GN_EPS) * lnw_ref[...] + lnb_ref[...] + bonus_ref[...]) * g_ref[...]
    o_ref[...] = out.astype(o_ref.dtype)


def _rwkv_post(y, bonus, g, ln_w, ln_b):
    m, ca = y.shape
    tm = _tile(m, 256, SUBLANE)
    row = pl.BlockSpec((tm, ca), lambda i: (i, 0))
    vec = pl.BlockSpec((1, ca), lambda i: (0, 0))
    return pl.pallas_call(
        _rwkv_post_kernel,
        grid=(m // tm,),
        in_specs=[row, row, row, vec, vec],
        out_specs=row,
        out_shape=jax.ShapeDtypeStruct((m, ca), BF16),
        compiler_params=_cparams(1),
        name="rwkv_post",
    )(y, bonus, g, ln_w, ln_b)


def _pool_kernel(z_ref, hist_ref, w_ref, scale_ref, o_ref, x_scr, *, t, tm, group, pos0):
    halo = 2 * SUBLANE
    x_scr[0:halo, :] = hist_ref[0]
    x_scr[halo:halo + t, :] = z_ref[0]
    for i in range(t // tm):
        r0 = i * tm
        pos = pos0 + r0 + lax.broadcasted_iota(jnp.int32, (tm, 1), 0)
        for gi, win in enumerate(POOL_WINDOWS):
            lo, hi = gi * group, (gi + 1) * group
            cur = x_scr[halo + r0:halo + r0 + tm, lo:hi]
            tot = cur
            for back in range(1, win):
                tot = tot + x_scr[halo + r0 - back:halo + r0 - back + tm, lo:hi]
            cnt = jnp.minimum(pos + 1, win).astype(F32)
            d = tot / cnt - cur
            y = _dot(d, w_ref[gi]) * scale_ref[:, lo:hi]
            o_ref[0, r0:r0 + tm, lo:hi] = y.astype(o_ref.dtype)


def _pool(mid_all, row0, b, t, hist16, pool_w, pool_scale, pos0, dims):
    assert row0 % t == 0
    mid = mid_all.reshape((1,) + mid_all.shape)
    cb = dims["C_B"]
    group = pool_w.shape[1]
    assert len(POOL_WINDOWS) * group == cb
    tm = _tile(t, 256, SUBLANE)
    return pl.pallas_call(
        functools.partial(_pool_kernel, t=t, tm=tm, group=group, pos0=pos0),
        grid=(b,),
        in_specs=[pl.BlockSpec((1, t, cb), lambda i: (0, row0 // t + i, 0)),
                  pl.BlockSpec((1, 2 * SUBLANE, cb), lambda i: (i, 0, 0)),
                  pl.BlockSpec(pool_w.shape, lambda i: (0, 0, 0)),
                  pl.BlockSpec((1, cb), lambda i: (0, 0))],
        out_specs=pl.BlockSpec((1, t, cb), lambda i: (i, 0, 0)),
        out_shape=jax.ShapeDtypeStruct((b, t, cb), BF16),
        scratch_shapes=[pltpu.VMEM((t + 2 * SUBLANE, cb), F32)],
        compiler_params=_cparams(1),
        name="pool",
    )(mid, hist16, pool_w, pool_scale.reshape(1, cb))


def _monotone_key(x):
    x = jnp.where(x == 0.0, 0.0, x)
    bits = lax.bitcast_convert_type(x, jnp.int32)
    return jnp.where(bits < 0, bits ^ jnp.int32(0x7FFFFFFF), bits)


def _kth_largest_key(key, k):
    def body(i, tau):
        cand = tau + jnp.left_shift(jnp.int32(1), jnp.int32(31) - i)
        cnt = jnp.sum(jnp.where(key >= cand, 1.0, 0.0), axis=-1, keepdims=True)
        return jnp.where(cnt >= k, cand, tau)

    tau0 = jnp.full((key.shape[0], 1), -2 ** 31, jnp.int32)
    return lax.fori_loop(0, 32, body, tau0)


def _prefix_count(ind):
    r = lax.broadcasted_iota(jnp.int32, (LANE, LANE), 0)
    c = lax.broadcasted_iota(jnp.int32, (LANE, LANE), 1)
    tri = jnp.where(r <= c, 1.0, 0.0).astype(BF16)
    run = jnp.zeros((ind.shape[0], 1), F32)
    outs = []
    for j in range(ind.shape[1] // LANE):
        pj = jnp.dot(ind[:, j * LANE:(j + 1) * LANE].astype(BF16), tri, preferred_element_type=F32)
        outs.append(pj + run)
        run = run + pj[:, LANE - 1:LANE]
    return outs[0] if len(outs) == 1 else jnp.concatenate(outs, axis=1)


def _topk_mask(scores, k):
    key = _monotone_key(scores)
    tau = _kth_largest_key(key, float(k))
    gt = key > tau
    eq = key == tau
    need = float(k) - jnp.sum(jnp.where(gt, 1.0, 0.0), axis=-1, keepdims=True)
    prefix = _prefix_count(jnp.where(eq, 1.0, 0.0))
    return gt | (eq & (prefix <= need))


def _dsa_prompt_kernel(q_ref, k_ref, v_ref, qi_ref, ki_ref, wi_ref, o_ref, key_s, sc_s, m_s, l_s, acc_s, *,
                       tq, t, kc, topk, h_idx, d_idx, n_heads, dh):
    q0 = pl.program_id(1) * tq
    n_kc = t // kc
    needed = (q0 + tq + kc - 1) // kc
    qpos = q0 + lax.broadcasted_iota(jnp.int32, (tq, 1), 0)
    qi = [qi_ref[0, :, h * d_idx:(h + 1) * d_idx].astype(BF16) for h in range(h_idx)]
    wi = [wi_ref[0, :, h:h + 1] for h in range(h_idx)]

    def score_chunk(c, carry):
        keys = ki_ref[0, pl.ds(pl.multiple_of(c * kc, kc), kc), :].astype(BF16)
        acc = jnp.zeros((tq, kc), F32)
        for h in range(h_idx):
            acc = acc + jnp.maximum(_dot_nt(qi[h], keys), 0.0) * wi[h]
        spos = c * kc + lax.broadcasted_iota(jnp.int32, (1, kc), 1)
        key_s[c] = _monotone_key(jnp.where(spos <= qpos, acc, -jnp.inf))
        return carry

    lax.fori_loop(0, needed, score_chunk, 0)

    def count(pred):
        def body(c, tot):
            ind = jnp.where(pred(key_s[c]), 1.0, 0.0)
            for j in range(kc // LANE):
                tot = tot + ind[:, j * LANE:(j + 1) * LANE]
            return tot
        lanes = lax.fori_loop(0, needed, body, jnp.zeros((tq, LANE), F32))
        return jnp.sum(lanes, axis=-1, keepdims=True)

    def search(i, tau):
        cand = tau + jnp.left_shift(jnp.int32(1), jnp.int32(31) - i)
        return jnp.where(count(lambda key: key >= cand) >= float(topk), cand, tau)

    tau = lax.fori_loop(0, 32, search, jnp.full((tq, 1), -2 ** 31, jnp.int32))
    need = float(topk) - count(lambda key: key > tau)

    def mask_chunk(c, run):
        key = key_s[c]
        eq = jnp.where(key == tau, 1.0, 0.0)
        prefix = _prefix_count(eq) + run
        spos = c * kc + lax.broadcasted_iota(jnp.int32, (1, kc), 1)
        sel = (key > tau) | ((key == tau) & (prefix <= need))
        sc_s[c] = jnp.where(sel & (spos <= qpos), 1.0, 0.0)
        return prefix[:, kc - 1:kc]

    lax.fori_loop(0, needed, mask_chunk, jnp.zeros((tq, 1), F32))

    m_s[...] = jnp.full(m_s.shape, NEG_BIG, F32)
    l_s[...] = jnp.zeros(l_s.shape, F32)
    acc_s[...] = jnp.zeros(acc_s.shape, F32)
    scale = dh ** -0.5

    def attn_chunk(c, carry):
        rows = pl.ds(pl.multiple_of(c * kc, kc), kc)
        ok = sc_s[c] > 0.5
        heads = range(n_heads)
        sls = [slice(h * dh, (h + 1) * dh) for h in heads]
        lg = [_dot_nt(q_ref[0, :, sls[h]], k_ref[0, rows, sls[h]]) for h in heads]
        lg = [jnp.where(ok, x * scale, NEG_BIG) for x in lg]
        m_old = [m_s[h] for h in heads]
        m_new = [jnp.maximum(m_old[h], jnp.max(lg[h], axis=-1, keepdims=True)) for h in heads]
        p = [jnp.where(ok, jnp.exp(lg[h] - m_new[h]), 0.0) for h in heads]
        pv = [_dot(p[h], v_ref[0, rows, sls[h]]) for h in heads]
        for h in heads:
            alpha = jnp.exp(m_old[h] - m_new[h])
            l_s[h] = alpha * l_s[h] + jnp.sum(p[h], axis=-1, keepdims=True)
            acc_s[:, sls[h]] = alpha * acc_s[:, sls[h]] + pv[h]
            m_s[h] = m_new[h]
        return carry

    lax.fori_loop(0, needed, attn_chunk, 0)
    for h in range(n_heads):
        sl = slice(h * dh, (h + 1) * dh)
        o_ref[0, :, sl] = (acc_s[:, sl] / l_s[h]).astype(o_ref.dtype)


def _dsa_prompt(q, k, v, qi, ki, wi, dims):
    b, t, cc = q.shape
    tq = _tile(t, 256, SUBLANE)
    kc = _tile(t, 512, tq)
    topk = min(TOPK_MAX, t // 4)
    assert kc >= topk
    h_idx, d_idx, n_heads = dims["H_IDX"], dims["D_IDX"], dims["H_C"]
    qb = lambda w: pl.BlockSpec((1, tq, w), lambda i, j: (i, j, 0))
    full = lambda w: pl.BlockSpec((1, t, w), lambda i, j: (i, 0, 0))
    return pl.pallas_call(
        functools.partial(_dsa_prompt_kernel, tq=tq, t=t, kc=kc, topk=topk, h_idx=h_idx, d_idx=d_idx,
                          n_heads=n_heads, dh=dims["DH_C"]),
        grid=(b, t // tq),
        in_specs=[qb(cc), full(cc), full(cc), qb(h_idx * d_idx), full(d_idx), qb(h_idx)],
        out_specs=qb(cc),
        out_shape=jax.ShapeDtypeStruct((b, t, cc), BF16),
        scratch_shapes=[pltpu.VMEM((t // kc, tq, kc), jnp.int32), pltpu.VMEM((t // kc, tq, kc), F32),
                        pltpu.VMEM((n_heads, tq, 1), F32),
                        pltpu.VMEM((n_heads, tq, 1), F32), pltpu.VMEM((tq, cc), F32)],
        compiler_params=_cparams(2),
        name="dsa_prompt",
    )(q, k, v, qi, ki, wi)


def _idx_rows_scores(qi2, wi2, keys, tq, h_idx, keys_t=False):
    s = jnp.maximum(_dot(qi2, keys) if keys_t else _dot_nt(qi2, keys), 0.0) * wi2
    acc = s[0:tq]
    for h in range(1, h_idx):
        acc = acc + s[h * tq:(h + 1) * tq]
    return acc


def _dsa_sample_scores_kernel(pt_ref, qi_ref, wi_ref, *refs, tq, h_idx):
    o_ref = refs[-1]
    for g, kc_ref in enumerate(refs[:-1]):
        o_ref[0, :, g * PAGE:(g + 1) * PAGE] = _idx_rows_scores(qi_ref[0], wi_ref[0], kc_ref[0, 0], tq, h_idx,
                                                                keys_t=True)


def _dsa_sample_scores(page_table, qi2, wi2, cache_kidx, l, tq, h_idx):
    b, n_pages = page_table.shape
    d_idx = cache_kidx.shape[-1]
    rows = qi2.shape[1]
    group = _tile(n_pages, 16, 1)
    cache_kidx = jnp.swapaxes(cache_kidx, 2, 3)
    page = lambda g: pl.BlockSpec((1, 1, d_idx, PAGE), lambda i, p, pt: (l, pt[i, p * group + g], 0, 0))
    grid_spec = pltpu.PrefetchScalarGridSpec(
        num_scalar_prefetch=1,
        grid=(b, n_pages // group),
        in_specs=[pl.BlockSpec((1, rows, d_idx), lambda i, p, pt: (i, 0, 0)),
                  pl.BlockSpec((1, rows, 1), lambda i, p, pt: (i, 0, 0))] + [page(g) for g in range(group)],
        out_specs=pl.BlockSpec((1, tq, group * PAGE), lambda i, p, pt: (i, 0, p)),
    )
    return pl.pallas_call(
        functools.partial(_dsa_sample_scores_kernel, tq=tq, h_idx=h_idx),
        grid_spec=grid_spec,
        out_shape=jax.ShapeDtypeStruct((b, tq, n_pages * PAGE), F32),
        compiler_params=_cparams(2),
        name="dsa_sample_scores",
    )(page_table, qi2, wi2, *([cache_kidx] * group))


def _dsa_sample_select_kernel(sc_ref, qi_ref, wi_ref, kn_ref, o_ref, on_ref, *, tq, h_idx, topk, past):
    new = _idx_rows_scores(qi_ref[0], wi_ref[0], kn_ref[0], tq, h_idx)
    qrow = lax.broadcasted_iota(jnp.int32, (tq, PAGE), 0)
    col = lax.broadcasted_iota(jnp.int32, (tq, PAGE), 1)
    new_ok = col <= qrow
    scores = jnp.concatenate([sc_ref[0], jnp.where(new_ok, new, -jnp.inf)], axis=1)
    sel = jnp.where(_topk_mask(scores, topk), 1.0, 0.0)
    o_ref[0] = sel[:, :past]
    on_ref[0] = jnp.where(new_ok, sel[:, past:], 0.0)


def _dsa_sample_select(scores, qi2, wi2, ki_new_pad, tq, h_idx):
    b, _, past = scores.shape
    topk = min(TOPK_MAX, (past + tq) // 4)
    rows, d_idx = qi2.shape[1], qi2.shape[2]
    return pl.pallas_call(
        functools.partial(_dsa_sample_select_kernel, tq=tq, h_idx=h_idx, topk=topk, past=past),
        grid=(b,),
        in_specs=[pl.BlockSpec((1, tq, past), lambda i: (i, 0, 0)),
                  pl.BlockSpec((1, rows, d_idx), lambda i: (i, 0, 0)),
                  pl.BlockSpec((1, rows, 1), lambda i: (i, 0, 0)),
                  pl.BlockSpec((1, PAGE, d_idx), lambda i: (i, 0, 0))],
        out_specs=[pl.BlockSpec((1, tq, past), lambda i: (i, 0, 0)), pl.BlockSpec((1, tq, PAGE), lambda i: (i, 0, 0))],
        out_shape=[jax.ShapeDtypeStruct((b, tq, past), F32), jax.ShapeDtypeStruct((b, tq, PAGE), F32)],
        compiler_params=_cparams(1),
        name="dsa_sample_select",
    )(scores, qi2, wi2, ki_new_pad)


def _dsa_sample_attn_kernel(pt_ref, q_ref, kn_ref, vn_ref, m_ref, mn_ref, *refs, n_steps, group, n_heads, dh, tq):
    kc_refs, vc_refs = refs[:group], refs[group:2 * group]
    o_ref, q2_s, exp_s, hm_s, m_s, l_s, acc_s = refs[2 * group:]
    p = pl.program_id(1)
    rows = n_heads * tq
    cols = PAGE * n_heads

    @pl.when(p == 0)
    def _():
        for h in range(n_heads):
            q2_s[h * tq:(h + 1) * tq, :] = q_ref[0, :, h * dh:(h + 1) * dh]
        pos = lax.broadcasted_iota(jnp.int32, (PAGE, cols), 0)
        col = lax.broadcasted_iota(jnp.int32, (PAGE, cols), 1)
        exp_s[...] = jnp.where(col // n_heads == pos, 1.0, 0.0).astype(exp_s.dtype)
        rr = lax.broadcasted_iota(jnp.int32, (rows, cols), 0)
        cc = lax.broadcasted_iota(jnp.int32, (rows, cols), 1)
        hm_s[...] = jnp.where((cc % n_heads) == (rr // tq), 1.0, 0.0)
        m_s[...] = jnp.full(m_s.shape, NEG_BIG, F32)
        l_s[...] = jnp.zeros(l_s.shape, F32)
        acc_s[...] = jnp.zeros(acc_s.shape, F32)

    def update(kmats, vmats, sels):
        q2, hm, exp_m = q2_s[...], hm_s[...], exp_s[...]
        lgs = [_dot_nt(q2, km) * (dh ** -0.5) for km in kmats]
        sel8 = [jnp.dot(s.astype(BF16), exp_m, preferred_element_type=F32) for s in sels]
        msks = [jnp.concatenate([s8] * n_heads, axis=0) * hm > 0.5 for s8 in sel8]
        lgm = [jnp.where(mk, lg, NEG_BIG) for mk, lg in zip(msks, lgs)]
        m_new = m_s[...]
        for x in lgm:
            m_new = jnp.maximum(m_new, jnp.max(x, axis=-1, keepdims=True))
        alpha = jnp.exp(m_s[...] - m_new)
        pms = [jnp.where(mk, jnp.exp(x - m_new), 0.0) for mk, x in zip(msks, lgm)]
        pvs = [_dot(pm, vm) for pm, vm in zip(pms, vmats)]
        l_new = alpha * l_s[...]
        acc_new = alpha * acc_s[...]
        for pm, pv in zip(pms, pvs):
            l_new = l_new + jnp.sum(pm, axis=-1, keepdims=True)
            acc_new = acc_new + pv
        l_s[...] = l_new
        acc_s[...] = acc_new
        m_s[...] = m_new

    @pl.when(p < n_steps)
    def _():
        update([kc_refs[g][0, 0].reshape(cols, dh) for g in range(group)],
               [vc_refs[g][0, 0].reshape(cols, dh) for g in range(group)],
               [m_ref[0, :, g * PAGE:(g + 1) * PAGE] for g in range(group)])

    @pl.when(p == n_steps)
    def _():
        update([kn_ref[0].reshape(cols, dh)], [vn_ref[0].reshape(cols, dh)], [mn_ref[0]])
        res = acc_s[...] / l_s[...]
        for h in range(n_heads):
            o_ref[0, :, h * dh:(h + 1) * dh] = res[h * tq:(h + 1) * tq, :].astype(o_ref.dtype)


def _dsa_sample_attn(page_table, q, cache_k, cache_v, k_new_pad, v_new_pad, mask, mask_new, l, dims):
    b, n_pages = page_table.shape
    tq = q.shape[1]
    n_heads, dh = dims["H_C"], dims["DH_C"]
    cc = n_heads * dh
    group = _tile(n_pages, 8, 1)
    n_steps = n_pages // group
    page = lambda g: pl.BlockSpec(
        (1, 1, PAGE, n_heads, dh),
        lambda i, p, pt: (l, pt[i, jnp.minimum(p * group + g, n_pages - 1)], 0, 0, 0))
    new = pl.BlockSpec((1, PAGE, n_heads, dh), lambda i, p, pt: (i, 0, 0, 0))
    grid_spec = pltpu.PrefetchScalarGridSpec(
        num_scalar_prefetch=1,
        grid=(b, n_steps + 1),
        in_specs=[pl.BlockSpec((1, tq, cc), lambda i, p, pt: (i, 0, 0)), new, new,
                  pl.BlockSpec((1, tq, group * PAGE), lambda i, p, pt: (i, 0, jnp.minimum(p, n_steps - 1))),
                  pl.BlockSpec((1, tq, PAGE), lambda i, p, pt: (i, 0, 0))]
                 + [page(g) for g in range(group)] * 2,
        out_specs=pl.BlockSpec((1, tq, cc), lambda i, p, pt: (i, 0, 0)),
        scratch_shapes=[pltpu.VMEM((n_heads * tq, dh), BF16), pltpu.VMEM((PAGE, PAGE * n_heads), BF16),
                        pltpu.VMEM((n_heads * tq, PAGE * n_heads), F32), pltpu.VMEM((n_heads * tq, 1), F32),
                        pltpu.VMEM((n_heads * tq, 1), F32), pltpu.VMEM((n_heads * tq, dh), F32)],
    )
    return pl.pallas_call(
        functools.partial(_dsa_sample_attn_kernel, n_steps=n_steps, group=group, n_heads=n_heads, dh=dh, tq=tq),
        grid_spec=grid_spec,
        out_shape=jax.ShapeDtypeStruct((b, tq, cc), BF16),
        compiler_params=_cparams(2),
        name="dsa_sample_attn",
    )(page_table, q, k_new_pad, v_new_pad, mask, mask_new, *([cache_k] * group), *([cache_v] * group))


def _dsa_sample(q, k_f32, v_f32, qi, ki, wi, cache_k, cache_v, cache_kidx, page_table, l, dims):
    b, tq, cc = q.shape
    h_idx, d_idx, n_heads, dh = dims["H_IDX"], dims["D_IDX"], dims["H_C"], dims["DH_C"]
    qi2 = jnp.transpose(qi.reshape(b, tq, h_idx, d_idx), (0, 2, 1, 3)).reshape(b, h_idx * tq, d_idx)
    wi2 = jnp.transpose(wi, (0, 2, 1)).reshape(b, h_idx * tq, 1)
    pad_rows = lambda x: jnp.pad(x, ((0, 0), (0, PAGE - tq)) + ((0, 0),) * (x.ndim - 2))
    scores = _dsa_sample_scores(page_table, qi2, wi2, cache_kidx, l, tq, h_idx)
    mask, mask_new = _dsa_sample_select(scores, qi2, wi2, pad_rows(ki), tq, h_idx)
    return _dsa_sample_attn(page_table, q, cache_k, cache_v, pad_rows(k_f32), pad_rows(v_f32), mask, mask_new, l,
                            dims)


def _prepare_weights(raw, dims):
    ca, cb, cc, qiw, d_idx, h_idx, d = (dims[k] for k in ("C_A", "C_B", "C_C", "QIW", "D_IDX", "H_IDX", "D"))
    dd, da, dg = dims["D_DECAY"], dims["D_AAA"], dims["D_GATE"]
    assert dd <= LANE and da <= LANE and dg % LANE == 0
    a_cols = 3 * ca + dd + da + dg
    o = [0, a_cols, a_cols + cb, a_cols + cb + cc, a_cols + cb + 2 * cc, a_cols + cb + 3 * cc]
    o += [o[-1] + qiw, o[-1] + qiw + d_idx, o[-1] + qiw + d_idx + h_idx]
    w_in = raw["w_in"]

    def pack_pa(x):
        zeros = lambda n: jnp.zeros(x.shape[:-1] + (n,), x.dtype)
        return jnp.concatenate([x[..., :3 * ca], x[..., 3 * ca:3 * ca + dd], zeros(LANE - dd),
                                x[..., 3 * ca + dd:3 * ca + dd + da], zeros(LANE - da),
                                x[..., 3 * ca + dd + da:a_cols]], axis=-1)

    assert d_idx + h_idx <= LANE and cb % LANE == 0 and cc % LANE == 0 and qiw % LANE == 0 and d % LANE == 0
    w_in_t = jnp.swapaxes(w_in, 1, 2)
    zero_rows = lambda n: jnp.zeros((w_in.shape[0], n, w_in.shape[1]), w_in.dtype)
    w_pa = jnp.concatenate([w_in_t[:, :3 * ca + dd], zero_rows(LANE - dd), w_in_t[:, 3 * ca + dd:3 * ca + dd + da],
                            zero_rows(LANE - da), w_in_t[:, 3 * ca + dd + da:a_cols]], axis=1)
    dims["MID_ROWS"] = (o[1], cb + 3 * cc + qiw + LANE)
    dims["GATE_ROWS"] = (o[8], 3 * d)
    assert o[1] + dims["MID_ROWS"][1] <= w_in.shape[2]
    pad_rows = lambda x: jnp.pad(x, ((0, 0), (0, LANE - x.shape[1]), (0, 0)))
    depth = w_in.shape[0]
    vec = lambda x: x.reshape(depth, 1, -1)
    P = dict(
        w_pa=w_pa, w_in_t=w_in_t,
        mu=vec(pack_pa(raw["rwkv_mu"])), w0=vec(raw["rwkv_w0"]), w2=pad_rows(raw["rwkv_w2"]).astype(BF16),
        a0=vec(raw["rwkv_a0"]), a2=pad_rows(raw["rwkv_a2"]).astype(BF16), g2=raw["rwkv_g2"].astype(BF16),
        k_k=vec(raw["rwkv_k_k"]), k_a=vec(raw["rwkv_k_a"]), r_k=vec(raw["rwkv_r_k"]),
        ln_w=vec(raw["rwkv_ln_w"]), ln_b=vec(raw["rwkv_ln_b"]),
        pool_w=raw["pool_w"].astype(BF16), pool_scale=raw["pool_scale"],
    )
    for name in ("ffn1_w_down", "ffn2_w_down", "w_br_a", "w_br_b", "w_br_c", "w_out"):
        P[name] = raw[name].astype(BF16)
    for name in ("ffn1_w_gate", "ffn1_w_up", "ffn2_w_gate", "ffn2_w_up", "ffn1_norm", "mix_norm", "ffn2_norm"):
        P[name] = raw[name]
    P["pack_pa"] = pack_pa
    return P


def _unpack_pa(x, dims):
    ca, dd, da = dims["C_A"], dims["D_DECAY"], dims["D_AAA"]
    return jnp.concatenate([x[..., :3 * ca], x[..., 3 * ca:3 * ca + dd],
                            x[..., 3 * ca + LANE:3 * ca + LANE + da], x[..., 3 * ca + 2 * LANE:]], axis=-1)


def _branches(pa, mid, grp, P, l, dims):
    row0, b, t, pos0 = grp["row0"], grp["b"], grp["t"], grp["pos0"]
    m = b * t
    ca, cb, cc = dims["C_A"], dims["C_B"], dims["C_C"]

    def last_rows(x, n, width):
        idx = (row0 + t - n + jnp.arange(b)[:, None] * t + jnp.arange(n)[None, :]).reshape(-1)
        return jnp.take(x, idx, axis=0)[:, :width].reshape(b, n, width)

    chunk = min(N_A, t)
    rt, at, bt, kt, v_a, bkt, gct, bonus, g = _rwkv_prep(pa, row0, b, t, P["pack_pa"](grp["shift"]), P, l, dims,
                                                          chunk)
    y_a, new_wkv = _rwkv_scan(rt, at, bt, kt, v_a, bkt, gct, grp["wkv"], chunk)
    ya = _rwkv_post(y_a.reshape(m, ca), bonus.reshape(m, ca), g.reshape(m, ca), P["ln_w"][l], P["ln_b"][l])
    new_shift = _unpack_pa(last_rows(pa, 1, pa.shape[1])[:, 0], dims)

    hist16 = jnp.pad(grp["pool"], ((0, 0), (2 * SUBLANE - POOL_HIST, 0), (0, 0)))
    yb = _pool(mid, row0, b, t, hist16, P["pool_w"][l], P["pool_scale"][l], pos0, dims)
    new_pool = jnp.concatenate([grp["pool"], last_rows(mid, min(t, POOL_HIST), cb)], axis=1)[:, -POOL_HIST:]

    q, k, v, k_bf, v_bf, qi, kiwi = _rope_all(mid, row0, b, t, pos0, dims)
    ki = kiwi[..., :dims["D_IDX"]]
    wi = kiwi[..., dims["D_IDX"]:dims["D_IDX"] + dims["H_IDX"]]
    if grp["cache"] is None:
        yc = _dsa_prompt(q, k_bf, v_bf, qi, ki, wi, dims)
    else:
        cache_k, cache_v, cache_kidx, page_table = grp["cache"]
        yc = _dsa_sample(q, k, v, qi, ki, wi, cache_k, cache_v, cache_kidx, page_table, l, dims)
    return (ya, yb.reshape(m, cb), yc.reshape(m, cc)), (k, v, ki, new_wkv, new_shift, new_pool)


def _layer(carry, groups, P, l, next_g, dims):
    h, hb, ssq = carry
    h, u, ssq = _ffn(h, hb, ssq, P["ffn1_w_gate"], P["ffn1_w_up"], P["ffn1_w_down"], l, P["mix_norm"][l])
    pa = _mm(u, P["w_pa"], l, out_dtype=F32, ssq=ssq, w_t=True, name="proj_pa")
    mid = _mm(u, P["w_in_t"], l, out_dtype=F32, ssq=ssq, tm_target=1024, w_t=True, w_rows=dims["MID_ROWS"],
              name="proj_mid")
    gates = _mm(u, P["w_in_t"], l, out_dtype=BF16, ssq=ssq, act="sigmoid", tn_target=512, w_t=True,
                w_rows=dims["GATE_ROWS"], name="proj_gate")
    outs = [_branches(pa, mid, grp, P, l, dims) for grp in groups]
    ya, yb, yc = (jnp.concatenate([o[0][i] for o in outs], axis=0) for i in range(3))
    merged = _merge(ya, yb, yc, P["w_br_a"], P["w_br_b"], P["w_br_c"], l, gates)
    h, hb, ssq = _mm(merged, P["w_out"], l, out_dtype=F32, res=h, next_g=P["ffn2_norm"][l], tn_target=512,
                     name="w_out")
    carry = _ffn(h, hb, ssq, P["ffn2_w_gate"], P["ffn2_w_up"], P["ffn2_w_down"], l, next_g)
    return carry, [o[1] for o in outs]


def kernel(x_prompt, x_sample, cache_k, cache_v, cache_kidx, state_wkv, state_shift, state_pool, page_table,
           ffn1_norm, ffn1_w_gate, ffn1_w_up, ffn1_w_down, mix_norm, w_in, rwkv_mu, rwkv_w0, rwkv_w2,
           rwkv_a0, rwkv_a2, rwkv_g2, rwkv_k_k, rwkv_k_a, rwkv_r_k, rwkv_ln_w, rwkv_ln_b, w_br_a, pool_w,
           pool_scale, w_br_b, w_br_c, w_out, ffn2_norm, ffn2_w_gate, ffn2_w_up, ffn2_w_down, final_norm):
    raw = dict(ffn1_norm=ffn1_norm, ffn1_w_gate=ffn1_w_gate, ffn1_w_up=ffn1_w_up, ffn1_w_down=ffn1_w_down,
               mix_norm=mix_norm, w_in=w_in, rwkv_mu=rwkv_mu, rwkv_w0=rwkv_w0, rwkv_w2=rwkv_w2, rwkv_a0=rwkv_a0,
               rwkv_a2=rwkv_a2, rwkv_g2=rwkv_g2, rwkv_k_k=rwkv_k_k, rwkv_k_a=rwkv_k_a, rwkv_r_k=rwkv_r_k,
               rwkv_ln_w=rwkv_ln_w, rwkv_ln_b=rwkv_ln_b, w_br_a=w_br_a, pool_w=pool_w, pool_scale=pool_scale,
               w_br_b=w_br_b, w_br_c=w_br_c, w_out=w_out, ffn2_norm=ffn2_norm, ffn2_w_gate=ffn2_w_gate,
               ffn2_w_up=ffn2_w_up, ffn2_w_down=ffn2_w_down)
    depth, d = mix_norm.shape
    ca = rwkv_w0.shape[-1]
    cb = pool_scale.shape[-1]
    n_heads, dh = cache_k.shape[3], cache_k.shape[4]
    cc = n_heads * dh
    d_idx = cache_kidx.shape[-1]
    dd, da, dg = rwkv_w2.shape[1], rwkv_a2.shape[1], rwkv_g2.shape[1]
    a_cols = 3 * ca + dd + da + dg
    h_idx = (w_in.shape[-1] - a_cols - cb - 3 * cc - d_idx - 3 * d) // (d_idx + 1)
    dims = dict(D=d, C_A=ca, C_B=cb, C_C=cc, H_C=n_heads, DH_C=dh, D_IDX=d_idx, H_IDX=h_idx,
                QIW=h_idx * d_idx, D_DECAY=dd, D_AAA=da, D_GATE=dg)
    assert w_in.shape[-1] == a_cols + cb + 3 * cc + h_idx * d_idx + d_idx + h_idx + 3 * d
    P = _prepare_weights(raw, dims)

    bp = x_prompt.shape[0]
    past = page_table.shape[1] * PAGE
    bs, ts = x_sample.shape[:2]
    tp = x_prompt.shape[1]
    mp, ms = bp * tp, bs * ts
    x_all = jnp.concatenate([x_prompt.reshape(mp, d), x_sample.reshape(ms, d)], axis=0)
    carry = (x_all,) + tuple(_prenorm(x_all, ffn1_norm[0]))
    st_p, st_s = [], []
    for l in range(depth):
        next_g = ffn1_norm[l + 1] if l + 1 < depth else final_norm
        groups = [
            dict(row0=0, b=bp, t=tp, pos0=0, cache=None, shift=jnp.zeros((bp, a_cols), F32),
                 wkv=jnp.zeros((bp, ca // N_A, N_A, N_A), F32), pool=jnp.zeros((bp, POOL_HIST, cb), F32)),
            dict(row0=mp, b=bs, t=ts, pos0=past, cache=(cache_k, cache_v, cache_kidx, page_table),
                 shift=state_shift[l], wkv=state_wkv[l], pool=state_pool[l]),
        ]
        carry, (sp, ss) = _layer(carry, groups, P, l, next_g, dims)
        st_p.append(sp)
        st_s.append(ss)
    y_p = _rmsnorm(carry[0], 0, mp, final_norm, F32).reshape(x_prompt.shape)
    y_s = _rmsnorm(carry[0], mp, ms, final_norm, F32).reshape(x_sample.shape)
    stk = lambda sts, i: jnp.stack([s[i] for s in sts])
    return (y_p, y_s,
            stk(st_p, 0), stk(st_p, 1), stk(st_p, 2), stk(st_p, 3), stk(st_p, 4), stk(st_p, 5),
            stk(st_s, 0), stk(st_s, 1), stk(st_s, 2), stk(st_s, 3), stk(st_s, 4), stk(st_s, 5))
```
